```python
import jax, jax.numpy as jnp
from jax import lax
import numpy as np

D_MODEL = 1024
BATCH = 4
SEQ = 4096
DEPTH = 1
DEC_BATCH = 16
DEC_SEQ = 4096
PAST_LEN = 128

N_MEM = 256
D_MIX = D_MODEL
D_LRU = D_MIX // 2
N_LRU_HEADS = 8
LRU_HEAD_DIM = D_LRU // N_LRU_HEADS
CONV_WIDTH = 4
LRU_C = 8.0
D_FFT = D_MIX - D_LRU
N_FFT_GROUPS = 4
FFT_GROUP_DIM = D_FFT // N_FFT_GROUPS
D_IN_PROJ = 2 * D_LRU + D_FFT
N_XATTN_HEADS = 4
XATTN_HEAD_DIM = D_MODEL // N_XATTN_HEADS
N_EXPERTS = 32
TOP_K = 4
D_FF = D_MODEL
SWIGLU_LIMIT = 7.0
SWIGLU_ALPHA = 1.702
EXPERT_BLOCK = 128
EPS = 1e-6

kernel_name = 'hybrid_rglru_fnet_moe_encoder'


def _rmsnorm(x, g):
    xf = x.astype(jnp.float32)
    y = xf * lax.rsqrt(jnp.mean(xf * xf, axis=-1, keepdims=True) + EPS)
    return (y * g.astype(jnp.float32)).astype(x.dtype)


def _centred_depthwise_conv(x, w, b):
    S = x.shape[1]
    left = CONV_WIDTH // 2
    right = CONV_WIDTH - 1 - left
    xp = jnp.pad(x, ((0, 0), (left, right), (0, 0)))
    out = b
    for j in range(CONV_WIDTH):
        out = out + xp[:, j:j + S, :] * w[j]
    return out


def _linear_combine(c1, c2):
    a1, b1 = c1
    a2, b2 = c2
    return a1 * a2, a2 * b1 + b2


def _rglru_bidir(x, w_a, b_a, w_x, b_x, lam):
    B, S, _ = x.shape
    f32 = jnp.float32
    xh = x.reshape(B, S, N_LRU_HEADS, LRU_HEAD_DIM)
    r = jax.nn.sigmoid(jnp.einsum('bshi,dhij->dbshj', xh, w_a.astype(f32)).reshape(2, B, S, D_LRU)
                       + b_a.astype(f32)[:, None, None, :])
    i = jax.nn.sigmoid(jnp.einsum('bshi,dhij->dbshj', xh, w_x.astype(f32)).reshape(2, B, S, D_LRU)
                       + b_x.astype(f32)[:, None, None, :])
    log_a = -LRU_C * r * jax.nn.softplus(-lam.astype(f32))[:, None, None, :]
    a = jnp.exp(log_a)
    u = jnp.sqrt(-jnp.expm1(2.0 * log_a)) * (i * x[None])
    _, h_fwd = lax.associative_scan(_linear_combine, (a[0], u[0]), axis=1)
    _, h_bwd = lax.associative_scan(_linear_combine, (a[1], u[1]), axis=1, reverse=True)
    return h_fwd + h_bwd


def _fourier_mix(u):
    B, S, _ = u.shape
    ug = u.astype(jnp.float32).reshape(B, S, N_FFT_GROUPS, FFT_GROUP_DIM)
    f = jnp.fft.fft2(ug, axes=(1, 3), norm='ortho').real
    return f.reshape(B, S, D_FFT)


def _memory_cross_attention(h, mem_n, w_q, w_kv, w_o):
    B, S, _ = h.shape
    M = mem_n.shape[1]
    q = (h @ w_q).reshape(B, S, N_XATTN_HEADS, XATTN_HEAD_DIM)
    kv = mem_n @ w_kv
    k = kv[..., :D_MODEL].reshape(B, M, N_XATTN_HEADS, XATTN_HEAD_DIM)
    v = kv[..., D_MODEL:].reshape(B, M, N_XATTN_HEADS, XATTN_HEAD_DIM)
    s = jnp.einsum('bshd,bmhd->bhsm', q.astype(jnp.float32), k.astype(jnp.float32)) * (XATTN_HEAD_DIM ** -0.5)
    p = jax.nn.softmax(s, axis=-1).astype(v.dtype)
    o = jnp.einsum('bhsm,bmhd->bshd', p, v).reshape(B, S, D_MODEL)
    return o @ w_o


def _moe(h, w_router, b_router, w_gate, b_gate, w_up, b_up, w_down, b_down):
    B, S, D = h.shape
    T = B * S
    A = T * TOP_K
    M = EXPERT_BLOCK
    NB = -(-A // M) + N_EXPERTS
    P = NB * M
    x2d = h.reshape(T, D)
    logits = x2d.astype(jnp.float32) @ w_router.astype(jnp.float32) + b_router.astype(jnp.float32)
    top_vals, top_idx = lax.top_k(logits, TOP_K)
    gates = jax.nn.softmax(top_vals, axis=-1)
    flat_e = top_idx.reshape(-1).astype(jnp.int32)
    flat_tok = jnp.arange(A, dtype=jnp.int32) // TOP_K
    flat_gate = gates.reshape(-1)
    order = jnp.argsort(flat_e)
    sorted_e = flat_e[order]
    sorted_tok = flat_tok[order]
    counts = jnp.bincount(flat_e, length=N_EXPERTS).astype(jnp.int32)
    padded = ((counts + M - 1) // M) * M
    padded_end = jnp.cumsum(padded)
    padded_start = padded_end - padded
    group_start = jnp.cumsum(counts) - counts
    pos = jnp.arange(A, dtype=jnp.int32) - group_start[sorted_e]
    dest = padded_start[sorted_e] + pos
    row_tok = jnp.full((P,), T, jnp.int32).at[dest].set(sorted_tok)
    row_gate = jnp.zeros((P,), jnp.float32).at[dest].set(flat_gate[order])
    block_expert = jnp.clip(jnp.searchsorted(padded_end, jnp.arange(NB, dtype=jnp.int32) * M, side='right'),
                            0, N_EXPERTS - 1).astype(jnp.int32)
    x_pad = jnp.concatenate([x2d, jnp.zeros((1, D), x2d.dtype)], axis=0)

    def block_step(y, blk):
        rows, g, e = blk
        xb = x_pad[rows]
        gt = xb @ w_gate[e] + b_gate[e]
        up = xb @ w_up[e] + b_up[e]
        gt = jnp.minimum(gt, SWIGLU_LIMIT)
        up = jnp.clip(up, -SWIGLU_LIMIT, SWIGLU_LIMIT)
        act = gt * jax.nn.sigmoid(SWIGLU_ALPHA * gt) * (up + 1.0)
        out = act @ w_down[e] + b_down[e]
        return y.at[rows].add(out.astype(jnp.float32) * g[:, None]), None

    y0 = jnp.zeros((T + 1, D), jnp.float32)
    y, _ = lax.scan(block_step, y0, (row_tok.reshape(NB, M), row_gate.reshape(NB, M), block_expert))
    return y[:T].astype(h.dtype).reshape(B, S, D)


def _encoder(x, mem, norm_mix, w_in, conv_w, conv_b, lru_w_a, lru_b_a, lru_w_x, lru_b_x, lru_lambda,
             norm_lru_out, norm_fft_out, w_out, norm_xattn, norm_mem, w_q, w_kv, w_o,
             norm_ffn, w_router, b_router, w_gate, b_gate, w_up, b_up, w_down, b_down, norm_final):
    for l in range(DEPTH):
        h = _rmsnorm(x, norm_mix[l])
        proj = h @ w_in[l]
        x_rec = proj[..., :D_LRU]
        gate = proj[..., D_LRU:2 * D_LRU]
        x_fft = proj[..., 2 * D_LRU:]
        xc = _centred_depthwise_conv(x_rec, conv_w[l], conv_b[l]).astype(jnp.float32)
        y_lru = _rglru_bidir(xc, lru_w_a[l], lru_b_a[l], lru_w_x[l], lru_b_x[l], lru_lambda[l]) \
            * jax.nn.gelu(gate.astype(jnp.float32))
        y_fft = _fourier_mix(x_fft)
        mixed = jnp.concatenate([_rmsnorm(y_lru.astype(x.dtype), norm_lru_out[l]),
                                 _rmsnorm(y_fft.astype(x.dtype), norm_fft_out[l])], axis=-1)
        x = x + mixed @ w_out[l]
        x = x + _memory_cross_attention(_rmsnorm(x, norm_xattn[l]), _rmsnorm(mem, norm_mem[l]),
                                        w_q[l], w_kv[l], w_o[l])
        x = x + _moe(_rmsnorm(x, norm_ffn[l]), w_router[l], b_router[l], w_gate[l], b_gate[l],
                     w_up[l], b_up[l], w_down[l], b_down[l])
    return _rmsnorm(x, norm_final)


def setup_inputs(seed: int = 0) -> dict:
    key = jax.random.key(seed)
    ks = jax.random.split(key, 32)
    f32 = jnp.float32
    L = DEPTH

    def nrm(k, shape, scale):
        return jax.random.normal(k, shape, f32) * scale

    def gain(k, shape):
        return 1.0 + 0.01 * jax.random.normal(k, shape, f32)

    a_c = jax.random.uniform(ks[10], (L, 2, D_LRU), f32, 0.9, 0.999)
    s = a_c ** (1.0 / LRU_C)
    lam = jnp.log(s) - jnp.log1p(-s)
    return {
        'x_prompt': nrm(ks[0], (BATCH, SEQ, D_MODEL), 1.0),
        'x_sample': nrm(ks[1], (DEC_BATCH, DEC_SEQ, D_MODEL), 1.0),
        'mem_prompt': nrm(ks[2], (BATCH, N_MEM, D_MODEL), 1.0),
        'mem_sample': nrm(ks[3], (DEC_BATCH, N_MEM, D_MODEL), 1.0),
        'norm_mix': gain(ks[4], (L, D_MODEL)),
        'w_in': nrm(ks[5], (L, D_MODEL, D_IN_PROJ), D_MODEL ** -0.5),
        'conv_w': nrm(ks[6], (L, CONV_WIDTH, D_LRU), CONV_WIDTH ** -0.5),
        'conv_b': nrm(ks[7], (L, D_LRU), 0.01),
        'lru_w_a': nrm(ks[8], (L, 2, N_LRU_HEADS, LRU_HEAD_DIM, LRU_HEAD_DIM), LRU_HEAD_DIM ** -0.5),
        'lru_b_a': nrm(ks[9], (L, 2, D_LRU), 0.01),
        'lru_w_x': nrm(ks[11], (L, 2, N_LRU_HEADS, LRU_HEAD_DIM, LRU_HEAD_DIM), LRU_HEAD_DIM ** -0.5),
        'lru_b_x': nrm(ks[12], (L, 2, D_LRU), 0.01),
        'lru_lambda': lam,
        'norm_lru_out': gain(ks[13], (L, D_LRU)),
        'norm_fft_out': gain(ks[14], (L, D_FFT)),
        'w_out': nrm(ks[15], (L, D_MIX, D_MODEL), D_MIX ** -0.5),
        'norm_xattn': gain(ks[16], (L, D_MODEL)),
        'norm_mem': gain(ks[17], (L, D_MODEL)),
        'w_q': nrm(ks[18], (L, D_MODEL, D_MODEL), D_MODEL ** -0.5),
        'w_kv': nrm(ks[19], (L, D_MODEL, 2 * D_MODEL), D_MODEL ** -0.5),
        'w_o': nrm(ks[20], (L, D_MODEL, D_MODEL), D_MODEL ** -0.5),
        'norm_ffn': gain(ks[21], (L, D_MODEL)),
        'w_router': nrm(ks[22], (L, D_MODEL, N_EXPERTS), D_MODEL ** -0.5),
        'b_router': nrm(ks[23], (L, N_EXPERTS), 0.01),
        'w_gate': nrm(ks[24], (L, N_EXPERTS, D_MODEL, D_FF), D_MODEL ** -0.5),
        'b_gate': nrm(ks[25], (L, N_EXPERTS, D_FF), 0.01),
        'w_up': nrm(ks[26], (L, N_EXPERTS, D_MODEL, D_FF), D_MODEL ** -0.5),
        'b_up': nrm(ks[27], (L, N_EXPERTS, D_FF), 0.01),
        'w_down': nrm(ks[28], (L, N_EXPERTS, D_FF, D_MODEL), D_FF ** -0.5),
        'b_down': nrm(ks[29], (L, N_EXPERTS, D_MODEL), 0.01),
        'norm_final': gain(ks[30], (D_MODEL,)),
    }


def reference(x_prompt, x_sample, mem_prompt, mem_sample, norm_mix, w_in, conv_w, conv_b,
              lru_w_a, lru_b_a, lru_w_x, lru_b_x, lru_lambda, norm_lru_out, norm_fft_out, w_out,
              norm_xattn, norm_mem, w_q, w_kv, w_o, norm_ffn, w_router, b_router,
              w_gate, b_gate, w_up, b_up, w_down, b_down, norm_final):
    y_prompt = _encoder(x_prompt, mem_prompt, norm_mix, w_in, conv_w, conv_b, lru_w_a, lru_b_a, lru_w_x,
                        lru_b_x, lru_lambda, norm_lru_out, norm_fft_out, w_out, norm_xattn, norm_mem,
                        w_q, w_kv, w_o, norm_ffn, w_router, b_router, w_gate, b_gate, w_up, b_up,
                        w_down, b_down, norm_final)
    y_sample = _encoder(x_sample, mem_sample, norm_mix, w_in, conv_w, conv_b, lru_w_a, lru_b_a, lru_w_x,
                        lru_b_x, lru_lambda, norm_lru_out, norm_fft_out, w_out, norm_xattn, norm_mem,
                        w_q, w_kv, w_o, norm_ffn, w_router, b_router, w_gate, b_gate, w_up, b_up,
                        w_down, b_down, norm_final)
    return (y_prompt, y_sample)
```

```python
import functools
import math

import numpy as np
import jax
import jax.numpy as jnp
from jax import lax
from jax.experimental import pallas as pl
from jax.experimental.pallas import tpu as pltpu

F32, BF16, I32 = jnp.float32, jnp.bfloat16, jnp.int32

D_MODEL = 1024
D_LRU = 512
LRU_HEAD_DIM = 64
CONV_WIDTH = 4
LRU_C = 8.0
D_FFT = 512
FFT_GROUP_DIM = 128
N_XATTN_HEADS = 4
XATTN_HEAD_DIM = 256
N_EXPERTS = 32
TOP_K = 4
D_FF = 1024
SWIGLU_LIMIT = 7.0
SWIGLU_ALPHA = 1.702
EPS = 1e-6

LANES = 128
SUBLANES = 8
ROW_TILES = D_MODEL // LANES
TOKEN_TILE = 512
EXPERT_TILE = 512
FF_CHUNK = 256
LRU_BLOCK = 128
SCAN_CHUNK = 128
SCAN_PITCH = SCAN_CHUNK + SUBLANES
VMEM_LIMIT = 56 * 1024 * 1024


def _params(*sem):
    return pltpu.CompilerParams(dimension_semantics=sem, vmem_limit_bytes=VMEM_LIMIT)


def _rms(xf):
    return xf * lax.rsqrt(jnp.mean(xf * xf, axis=-1, keepdims=True) + EPS)


def _sigmoid(x):
    return 1.0 / (1.0 + jnp.exp(-x))


def _gelu_tanh(x):
    return 0.5 * x * (1.0 + jnp.tanh(math.sqrt(2.0 / math.pi) * (x + 0.044715 * (x * x * x))))


def _kv_kernel(mem_ref, g_ref, wkt_ref, wv_ref, kt_ref, v_ref):
    mn = (_rms(mem_ref[0]) * g_ref[...]).astype(BF16)
    kt = lax.dot_general(wkt_ref[...], mn, (((1,), (1,)), ((), ())), preferred_element_type=F32)
    kt_ref[0] = kt.astype(BF16)
    v_ref[0] = jnp.dot(mn, wv_ref[...], preferred_element_type=F32).astype(BF16)


def _kv_proj(mem, g, wkt, wv):
    nb, m, d = mem.shape
    return pl.pallas_call(
        _kv_kernel,
        out_shape=(jax.ShapeDtypeStruct((nb, d, m), BF16), jax.ShapeDtypeStruct((nb, m, d), BF16)),
        grid=(nb,),
        in_specs=[pl.BlockSpec((1, m, d), lambda b: (b, 0, 0)),
                  pl.BlockSpec((1, d), lambda b: (0, 0)),
                  pl.BlockSpec((d, d), lambda b: (0, 0)),
                  pl.BlockSpec((d, d), lambda b: (0, 0))],
        out_specs=(pl.BlockSpec((1, d, m), lambda b: (b, 0, 0)),
                   pl.BlockSpec((1, m, d), lambda b: (b, 0, 0))),
        compiler_params=_params("arbitrary"),
        name="kv_proj",
    )(mem, g, wkt, wv)


def _inproj_kernel(x_ref, g_ref, w_ref, dft_ref, xrec_ref, gate_ref, zr_ref, zi_ref):
    h = (_rms(x_ref[...]) * g_ref[...]).astype(BF16)
    proj = jnp.dot(h, w_ref[...], preferred_element_type=F32)
    xrec_ref[...] = proj[:, :D_LRU]
    gate_ref[...] = proj[:, D_LRU:2 * D_LRU].astype(BF16)
    z = jnp.dot(proj[:, 2 * D_LRU:].astype(BF16), dft_ref[...], preferred_element_type=F32)
    zr_ref[...] = z[:, :D_FFT]
    zi_ref[...] = z[:, D_FFT:]


def _in_proj(x2d, g, w_in, dft_c):
    t = x2d.shape[0]
    tm = TOKEN_TILE
    row = lambda i: (i, 0)
    full = lambda i: (0, 0)
    return pl.pallas_call(
        _inproj_kernel,
        out_shape=(jax.ShapeDtypeStruct((t, D_LRU), F32), jax.ShapeDtypeStruct((t, D_LRU), BF16),
                   jax.ShapeDtypeStruct((t, D_FFT), F32), jax.ShapeDtypeStruct((t, D_FFT), F32)),
        grid=(t // tm,),
        in_specs=[pl.BlockSpec((tm, D_MODEL), row), pl.BlockSpec((1, D_MODEL), full),
                  pl.BlockSpec(w_in.shape, full), pl.BlockSpec(dft_c.shape, full)],
        out_specs=(pl.BlockSpec((tm, D_LRU), row), pl.BlockSpec((tm, D_LRU), row),
                   pl.BlockSpec((tm, D_FFT), row), pl.BlockSpec((tm, D_FFT), row)),
        compiler_params=_params("arbitrary"),
        name="in_proj",
    )(x2d, g, w_in, dft_c)


def _lru_kernel(x_ref, gate_ref, cw_ref, cb_ref, w_ref, b_ref, lam_ref, y_ref,
                af_ref, uf_ref, ab_ref, ub_ref, cf_ref, cbk_ref, *, seq):
    nch = seq // SCAN_CHUNK
    cw = cw_ref[0]
    cb = cb_ref[0]
    lam = lam_ref[0]
    sp = jnp.maximum(-lam, 0.0) + jnp.log1p(jnp.exp(-jnp.abs(lam)))
    bias = b_ref[0]
    c = LRU_BLOCK

    def gates_body(j, _):
        r0 = pl.multiple_of(j * SCAN_CHUNK, SCAN_CHUNK)
        main = x_ref[0, pl.ds(r0, SCAN_CHUNK), :]
        prev = x_ref[0, pl.ds(jnp.maximum(r0 - SUBLANES, 0), SUBLANES), :]
        nxt = x_ref[0, pl.ds(jnp.minimum(r0 + SCAN_CHUNK, seq - SUBLANES), SUBLANES), :]
        prev = jnp.where(j > 0, prev, 0.0)
        nxt = jnp.where(j < nch - 1, nxt, 0.0)
        win = jnp.concatenate([prev, main, nxt], axis=0)
        base = SUBLANES - CONV_WIDTH // 2
        xc = cb
        for tap in range(CONV_WIDTH):
            xc = xc + win[base + tap:base + tap + SCAN_CHUNK, :] * cw[tap:tap + 1, :]
        xcb = xc.astype(BF16)
        o0 = pl.multiple_of(j * SCAN_PITCH, SUBLANES)
        for d, (a_ref, u_ref) in enumerate(((af_ref, uf_ref), (ab_ref, ub_ref))):
            ga = jnp.dot(xcb, w_ref[0, :, d * c:(d + 1) * c], preferred_element_type=F32) \
                + bias[:, d * c:(d + 1) * c]
            gx = jnp.dot(xcb, w_ref[0, :, (2 + d) * c:(3 + d) * c], preferred_element_type=F32) \
                + bias[:, (2 + d) * c:(3 + d) * c]
            log_a = (-LRU_C) * _sigmoid(ga) * sp[d:d + 1, :]
            a = jnp.exp(log_a)
            u = jnp.sqrt(-jnp.tanh(log_a) * (1.0 + a * a)) * (_sigmoid(gx) * xc)
            a_ref[pl.ds(o0, SCAN_CHUNK), :] = a
            u_ref[pl.ds(o0, SCAN_CHUNK), :] = u
        return 0

    lax.fori_loop(0, nch, gates_body, 0)

    def scan_body(t, carry):
        hf, pf, hb, pb = carry
        fwd = pl.ds(t, nch, stride=SCAN_PITCH)
        bwd = pl.ds(SCAN_CHUNK - 1 - t, nch, stride=SCAN_PITCH)
        a = af_ref[fwd, :]
        hf = a * hf + uf_ref[fwd, :]
        pf = a * pf
        uf_ref[fwd, :] = hf
        af_ref[fwd, :] = pf
        a = ab_ref[bwd, :]
        hb = a * hb + ub_ref[bwd, :]
        pb = a * pb
        ub_ref[bwd, :] = hb
        ab_ref[bwd, :] = pb
        return hf, pf, hb, pb

    zeros = jnp.zeros((nch, c), F32)
    ones = jnp.ones((nch, c), F32)
    hf, pf, hb, pb = lax.fori_loop(0, SCAN_CHUNK, scan_body, (zeros, ones, zeros, ones))

    carry = jnp.zeros((1, c), F32)
    cf_ref[0:1, :] = carry
    for j in range(1, nch):
        carry = hf[j - 1:j, :] + pf[j - 1:j, :] * carry
        cf_ref[j:j + 1, :] = carry
    carry = jnp.zeros((1, c), F32)
    cbk_ref[nch - 1:nch, :] = carry
    for j in range(nch - 2, -1, -1):
        carry = hb[j + 1:j + 2, :] + pb[j + 1:j + 2, :] * carry
        cbk_ref[j:j + 1, :] = carry

    def out_body(j, _):
        r0 = pl.multiple_of(j * SCAN_CHUNK, SCAN_CHUNK)
        o0 = pl.multiple_of(j * SCAN_PITCH, SUBLANES)
        rows = pl.ds(o0, SCAN_CHUNK)
        h = (uf_ref[rows, :] + af_ref[rows, :] * cf_ref[pl.ds(j, 1), :]
             + ub_ref[rows, :] + ab_ref[rows, :] * cbk_ref[pl.ds(j, 1), :])
        g = gate_ref[0, pl.ds(r0, SCAN_CHUNK), :].astype(F32)
        y_ref[0, pl.ds(r0, SCAN_CHUNK), :] = (h * _gelu_tanh(g)).astype(BF16)
        return 0

    lax.fori_loop(0, nch, out_body, 0)


def _lru_mix(xrec, gate, conv_w, conv_b, w_gates, b_gates, lam):
    nb, seq, _ = xrec.shape
    ncb = D_LRU // LRU_BLOCK
    c = LRU_BLOCK
    nch = seq // SCAN_CHUNK
    act = lambda b, k: (b, 0, k)
    par = lambda b, k: (k, 0, 0)
    scan_buf = pltpu.VMEM((nch * SCAN_PITCH, c), F32)
    return pl.pallas_call(
        functools.partial(_lru_kernel, seq=seq),
        out_shape=jax.ShapeDtypeStruct((nb, seq, D_LRU), BF16),
        grid=(nb, ncb),
        in_specs=[pl.BlockSpec((1, seq, c), act), pl.BlockSpec((1, seq, c), act),
                  pl.BlockSpec((1, CONV_WIDTH, c), par), pl.BlockSpec((1, 1, c), par),
                  pl.BlockSpec((1, c, 4 * c), par), pl.BlockSpec((1, 1, 4 * c), par),
                  pl.BlockSpec((1, 2, c), par)],
        out_specs=pl.BlockSpec((1, seq, c), act),
        scratch_shapes=[scan_buf, scan_buf, scan_buf, scan_buf,
                        pltpu.VMEM((nch, c), F32), pltpu.VMEM((nch, c), F32)],
        compiler_params=_params("arbitrary", "arbitrary"),
        name="lru_mix",
    )(xrec, gate, conv_w, conv_b, w_gates, b_gates, lam)


def _seqdft_kernel(zr_ref, zi_ref, f1_ref, m2_ref, y_ref, ar_ref, ai_ref, *, n1, n2):
    f1 = f1_ref[...]

    def stage1(i2, _):
        rows = pl.ds(i2, n1, stride=n2)
        slab = jnp.concatenate([zr_ref[0, rows, :], zi_ref[0, rows, :]], axis=0).astype(BF16)
        a = jnp.dot(f1, slab, preferred_element_type=F32)
        o0 = pl.multiple_of(i2 * n1, n1)
        ar_ref[pl.ds(o0, n1), :] = a[:n1]
        ai_ref[pl.ds(o0, n1), :] = a[n1:]
        return 0

    lax.fori_loop(0, n2, stage1, 0)

    def stage2(k1, _):
        rows = pl.ds(k1, n2, stride=n1)
        slab = jnp.concatenate([ar_ref[rows, :], ai_ref[rows, :]], axis=0).astype(BF16)
        o = jnp.dot(m2_ref[k1], slab, preferred_element_type=F32)
        y_ref[0, pl.ds(k1, n2, stride=n1), :] = o
        return 0

    lax.fori_loop(0, n1, stage2, 0)


def _seq_dft(zr, zi, f1, m2):
    nb, seq, _ = zr.shape
    n1 = f1.shape[0] // 2
    n2 = seq // n1
    cw = LANES
    act = lambda b, k: (b, 0, k)
    return pl.pallas_call(
        functools.partial(_seqdft_kernel, n1=n1, n2=n2),
        out_shape=jax.ShapeDtypeStruct((nb, seq, D_FFT), F32),
        grid=(nb, D_FFT // cw),
        in_specs=[pl.BlockSpec((1, seq, cw), act), pl.BlockSpec((1, seq, cw), act),
                  pl.BlockSpec(f1.shape, lambda b, k: (0, 0)),
                  pl.BlockSpec(m2.shape, lambda b, k: (0, 0, 0))],
        out_specs=pl.BlockSpec((1, seq, cw), act),
        scratch_shapes=[pltpu.VMEM((seq, cw), F32), pltpu.VMEM((seq, cw), F32)],
        compiler_params=_params("arbitrary", "arbitrary"),
        name="seq_dft",
    )(zr, zi, f1, m2)


def _dft_tables(seq):
    n1 = int(round(math.sqrt(seq)))
    n2 = seq // n1
    assert n1 * n2 == seq
    g = FFT_GROUP_DIM
    ang = 2.0 * np.pi * np.outer(np.arange(g), np.arange(g)) / g
    cg, sg = np.cos(ang) / math.sqrt(g), np.sin(ang) / math.sqrt(g)
    ngroups = D_FFT // g
    dft_c = np.zeros((D_FFT, 2 * D_FFT))
    for q in range(ngroups):
        dft_c[q * g:(q + 1) * g, q * g:(q + 1) * g] = cg
        dft_c[q * g:(q + 1) * g, D_FFT + q * g:D_FFT + (q + 1) * g] = -sg
    ang1 = 2.0 * np.pi * np.outer(np.arange(n1), np.arange(n1)) / n1
    c1, s1 = np.cos(ang1) / math.sqrt(n1), np.sin(ang1) / math.sqrt(n1)
    f1 = np.block([[c1, s1], [-s1, c1]])
    k1 = np.arange(n1)[:, None, None]
    k2 = np.arange(n2)[None, :, None]
    i2 = np.arange(n2)[None, None, :]
    ang2 = 2.0 * np.pi * (i2 * k2 / n2 + i2 * k1 / seq)
    m2 = np.concatenate([np.cos(ang2), np.sin(ang2)], axis=2) / math.sqrt(n2)
    return (jnp.asarray(dft_c, F32), jnp.asarray(f1, F32), jnp.asarray(m2, F32))


def _attn_route_kernel(x_ref, ylru_ref, yfft_ref, kt_ref, v_ref, gl_ref, gf_ref, wout_ref,
                       gx_ref, wq_ref, wo_ref, gffn_ref, wr_ref, br_ref,
                       x2_ref, h3_ref, idx_ref, gate_ref, rank_ref, cnt_ref, run_ref):
    first = jnp.logical_and(pl.program_id(0) == 0, pl.program_id(1) == 0)

    @pl.when(first)
    def _():
        run_ref[...] = jnp.zeros_like(run_ref)

    tm = x_ref.shape[1]
    m_lru = _rms(ylru_ref[0].astype(F32)) * gl_ref[...]
    m_fft = _rms(yfft_ref[0]) * gf_ref[...]
    mixed = jnp.concatenate([m_lru, m_fft], axis=-1).astype(BF16)
    x1 = x_ref[0] + jnp.dot(mixed, wout_ref[...], preferred_element_type=F32)

    hq = (_rms(x1) * gx_ref[...]).astype(BF16)
    q = jnp.dot(hq, wq_ref[...], preferred_element_type=F32) * (XATTN_HEAD_DIM ** -0.5)
    qb = q.astype(BF16)
    heads = []
    for h in range(N_XATTN_HEADS):
        sl = slice(h * XATTN_HEAD_DIM, (h + 1) * XATTN_HEAD_DIM)
        s = jnp.dot(qb[:, sl], kt_ref[0, sl, :], preferred_element_type=F32)
        p = jnp.exp(s - jnp.max(s, axis=-1, keepdims=True))
        inv = 1.0 / jnp.sum(p, axis=-1, keepdims=True)
        o = jnp.dot(p.astype(BF16), v_ref[0, :, sl], preferred_element_type=F32) * inv
        heads.append(o.astype(BF16))
    att = jnp.concatenate(heads, axis=-1)
    x2 = x1 + jnp.dot(att, wo_ref[...], preferred_element_type=F32)
    x2_ref[0] = x2

    h3 = _rms(x2) * gffn_ref[...]
    for s in range(ROW_TILES):
        h3_ref[pl.ds(s, tm, stride=ROW_TILES), :] = h3[:, s * LANES:(s + 1) * LANES]

    logits = jnp.dot(h3, wr_ref[...], preferred_element_type=F32,
                     precision=lax.Precision.HIGHEST) + br_ref[...]
    lane_e = lax.broadcasted_iota(I32, logits.shape, 1)
    lane_k = lax.broadcasted_iota(I32, (tm, TOP_K), 1)
    lg = logits
    vals, onehots = [], []
    idx_out = jnp.zeros((tm, TOP_K), I32)
    for k in range(TOP_K):
        m = jnp.max(lg, axis=-1, keepdims=True)
        idx = jnp.min(jnp.where(lg == m, lane_e, N_EXPERTS), axis=-1, keepdims=True)
        oh = lane_e == idx
        vals.append(m)
        onehots.append(oh)
        idx_out = jnp.where(lane_k == k, idx, idx_out)
        lg = jnp.where(oh, -jnp.inf, lg)
    idx_ref[0] = idx_out

    exps = [jnp.exp(v - vals[0]) for v in vals]
    inv = 1.0 / (exps[0] + exps[1] + exps[2] + exps[3])
    gate_out = jnp.zeros((tm, TOP_K), F32)
    for k in range(TOP_K):
        gate_out = jnp.where(lane_k == k, exps[k] * inv, gate_out)
    gate_ref[0] = gate_out

    member = jnp.zeros(logits.shape, F32)
    for oh in onehots:
        member = member + oh.astype(F32)
    r_i = lax.broadcasted_iota(I32, (tm, tm), 0)
    c_i = lax.broadcasted_iota(I32, (tm, tm), 1)
    ltri = jnp.where(r_i > c_i, 1.0, 0.0).astype(BF16)
    before = jnp.dot(ltri, member.astype(BF16), preferred_element_type=F32) + run_ref[...]
    rank_out = jnp.zeros((tm, TOP_K), I32)
    for k in range(TOP_K):
        rk = jnp.sum(jnp.where(onehots[k], before, 0.0), axis=-1, keepdims=True)
        rank_out = jnp.where(lane_k == k, rk.astype(I32), rank_out)
    rank_ref[0] = rank_out
    total = run_ref[...] + jnp.sum(member, axis=0, keepdims=True)
    run_ref[...] = total
    cnt_ref[...] = total.astype(I32)


def _attn_route(x, ylru, yfft, kt, v, gl, gf, w_out, gx, wq, wo, gffn, wr, br):
    nb, seq, d = x.shape
    tm = TOKEN_TILE
    nt = seq // tm
    m = v.shape[1]
    tok = lambda b, i: (b, i, 0)
    full2 = lambda b, i: (0, 0)
    per_b = lambda b, i: (b, 0, 0)
    t = nb * seq
    return pl.pallas_call(
        _attn_route_kernel,
        out_shape=(jax.ShapeDtypeStruct((nb, seq, d), F32),
                   jax.ShapeDtypeStruct((t * ROW_TILES, LANES), F32),
                   jax.ShapeDtypeStruct((nb, seq, TOP_K), I32),
                   jax.ShapeDtypeStruct((nb, seq, TOP_K), F32),
                   jax.ShapeDtypeStruct((nb, seq, TOP_K), I32),
                   jax.ShapeDtypeStruct((1, N_EXPERTS), I32)),
        grid=(nb, nt),
        in_specs=[pl.BlockSpec((1, tm, d), tok), pl.BlockSpec((1, tm, D_LRU), tok),
                  pl.BlockSpec((1, tm, D_FFT), tok),
                  pl.BlockSpec((1, d, m), per_b), pl.BlockSpec((1, m, d), per_b),
                  pl.BlockSpec((1, D_LRU), full2), pl.BlockSpec((1, D_FFT), full2),
                  pl.BlockSpec((d, d), full2), pl.BlockSpec((1, d), full2),
                  pl.BlockSpec((d, d), full2), pl.BlockSpec((d, d), full2),
                  pl.BlockSpec((1, d), full2), pl.BlockSpec((d, N_EXPERTS), full2),
                  pl.BlockSpec((1, N_EXPERTS), full2)],
        out_specs=(pl.BlockSpec((1, tm, d), tok),
                   pl.BlockSpec((tm * ROW_TILES, LANES), lambda b, i: (b * nt + i, 0)),
                   pl.BlockSpec((1, tm, TOP_K), tok), pl.BlockSpec((1, tm, TOP_K), tok),
                   pl.BlockSpec((1, tm, TOP_K), tok),
                   pl.BlockSpec((1, N_EXPERTS), full2)),
        scratch_shapes=[pltpu.VMEM((1, N_EXPERTS), F32)],
        compiler_params=_params("arbitrary", "arbitrary"),
        name="attn_route",
    )(x, ylru, yfft, kt, v, gl, gf, w_out, gx, wq, wo, gffn, wr, br)


def _row_copy(src_ref, src_row, dst_ref, dst_row, sem):
    return pltpu.make_async_copy(
        src_ref.at[pl.ds(pl.multiple_of(src_row * ROW_TILES, ROW_TILES), ROW_TILES), :],
        dst_ref.at[pl.ds(pl.multiple_of(dst_row * ROW_TILES, ROW_TILES), ROW_TILES), :],
        sem)


def _dispatch_kernel(dest_hbm, h3_ref, xs_hbm, dsm, isem, sem):
    i = pl.program_id(0)
    tm = h3_ref.shape[0] // ROW_TILES
    n = tm * TOP_K
    idx_cp = pltpu.make_async_copy(dest_hbm.at[pl.ds(pl.multiple_of(i * n, n), n)], dsm, isem)
    idx_cp.start()
    idx_cp.wait()

    def issue(a, _):
        _row_copy(h3_ref, a // TOP_K, xs_hbm, dsm[a], sem).start()
        return 0

    lax.fori_loop(0, n, issue, 0, unroll=8)
    for _ in range(TOP_K):
        pltpu.make_async_copy(h3_ref, xs_hbm.at[pl.ds(0, tm * ROW_TILES), :], sem).wait()


def _moe_dispatch(dest, h3, p_rows):
    t = h3.shape[0] // ROW_TILES
    tm = TOKEN_TILE
    return pl.pallas_call(
        _dispatch_kernel,
        out_shape=jax.ShapeDtypeStruct((p_rows * ROW_TILES, LANES), F32),
        grid=(t // tm,),
        in_specs=[pl.BlockSpec(memory_space=pl.ANY),
                  pl.BlockSpec((tm * ROW_TILES, LANES), lambda i: (i, 0))],
        out_specs=pl.BlockSpec(memory_space=pl.ANY),
        scratch_shapes=[pltpu.SMEM((tm * TOP_K,), I32), pltpu.SemaphoreType.DMA,
                        pltpu.SemaphoreType.DMA],
        compiler_params=_params("arbitrary"),
        name="moe_dispatch",
    )(dest, h3)


def _experts_kernel(bexp_ref, bvalid_ref, nblk_ref, xs_ref, wg_ref, bg_ref, wu_ref, bu_ref,
                    wd_ref, bd_ref, ys_ref, acc_ref):
    i = pl.program_id(0)

    @pl.when(i < nblk_ref[0])
    def _():
        tm = xs_ref.shape[0] // ROW_TILES
        x = jnp.concatenate([xs_ref[pl.ds(s, tm, stride=ROW_TILES), :] for s in range(ROW_TILES)],
                            axis=-1)
        row = lax.broadcasted_iota(I32, (tm, 1), 0)
        xb = jnp.where(row < bvalid_ref[i], x, 0.0).astype(BF16)
        for cidx in range(D_FF // FF_CHUNK):
            sl = slice(cidx * FF_CHUNK, (cidx + 1) * FF_CHUNK)
            gt = jnp.dot(xb, wg_ref[0, :, sl], preferred_element_type=F32) + bg_ref[0, :, sl]
            up = jnp.dot(xb, wu_ref[0, :, sl], preferred_element_type=F32) + bu_ref[0, :, sl]
            gt = jnp.minimum(gt, SWIGLU_LIMIT)
            up = jnp.clip(up, -SWIGLU_LIMIT, SWIGLU_LIMIT)
            act = (gt * _sigmoid(SWIGLU_ALPHA * gt) * (up + 1.0)).astype(BF16)
            part = jnp.dot(act, wd_ref[0, sl, :], preferred_element_type=F32)
            if cidx == 0:
                acc_ref[...] = part + bd_ref[0]
            else:
                acc_ref[...] += part
        for s in range(ROW_TILES):
            ys_ref[pl.ds(s, tm, stride=ROW_TILES), :] = acc_ref[:, s * LANES:(s + 1) * LANES]


def _moe_experts(bexp, bvalid, nblk, xs, wg, bg, wu, bu, wd, bd):
    nblocks = bexp.shape[0]
    tm = EXPERT_TILE
    blk = lambda i, be, bv, nb: (jnp.minimum(i, nb[0] - 1), 0)
    wsel = lambda i, be, bv, nb: (be[jnp.minimum(i, nb[0] - 1)], 0, 0)
    wspec = pl.BlockSpec((1, D_MODEL, D_FF), wsel)
    bspec = pl.BlockSpec((1, 1, D_FF), wsel)
    return pl.pallas_call(
        _experts_kernel,
        out_shape=jax.ShapeDtypeStruct(xs.shape, F32),
        grid_spec=pltpu.PrefetchScalarGridSpec(
            num_scalar_prefetch=3,
            grid=(nblocks,),
            in_specs=[pl.BlockSpec((tm * ROW_TILES, LANES), blk),
                      wspec, bspec, wspec, bspec, wspec, bspec],
            out_specs=pl.BlockSpec((tm * ROW_TILES, LANES), blk),
            scratch_shapes=[pltpu.VMEM((tm, D_MODEL), F32)]),
        compiler_params=_params("arbitrary"),
        name="moe_experts",
    )(bexp, bvalid, nblk, xs, wg, bg, wu, bu, wd, bd)


def _combine_kernel(dest_hbm, ys_hbm, x2_ref, gate_ref, gfin_ref, out_ref, dsm, cbuf, isem, sem):
    i = pl.program_id(0)
    tm = x2_ref.shape[0]
    n = tm * TOP_K
    idx_cp = pltpu.make_async_copy(dest_hbm.at[pl.ds(pl.multiple_of(i * n, n), n)], dsm, isem)
    idx_cp.start()
    idx_cp.wait()

    def issue(tk, _):
        for k in range(TOP_K):
            _row_copy(ys_hbm, dsm[tk * TOP_K + k], cbuf.at[k], tk, sem).start()
        return 0

    lax.fori_loop(0, tm, issue, 0, unroll=2)
    for k in range(TOP_K):
        pltpu.make_async_copy(ys_hbm.at[pl.ds(0, tm * ROW_TILES), :], cbuf.at[k], sem).wait()

    acc = x2_ref[...]
    g = gate_ref[...]
    for k in range(TOP_K):
        yk = jnp.concatenate([cbuf[k, pl.ds(s, tm, stride=ROW_TILES), :] for s in range(ROW_TILES)],
                             axis=-1)
        acc = acc + g[:, k:k + 1] * yk
    out_ref[...] = _rms(acc) * gfin_ref[...]


def _moe_combine(dest, ys, x2, gates, gfin):
    t, d = x2.shape
    tm = TOKEN_TILE
    return pl.pallas_call(
        _combine_kernel,
        out_shape=jax.ShapeDtypeStruct((t, d), F32),
        grid=(t // tm,),
        in_specs=[pl.BlockSpec(memory_space=pl.ANY), pl.BlockSpec(memory_space=pl.ANY),
                  pl.BlockSpec((tm, d), lambda i: (i, 0)),
                  pl.BlockSpec((tm, TOP_K), lambda i: (i, 0)),
                  pl.BlockSpec((1, d), lambda i: (0, 0))],
        out_specs=pl.BlockSpec((tm, d), lambda i: (i, 0)),
        scratch_shapes=[pltpu.SMEM((tm * TOP_K,), I32),
                        pltpu.VMEM((TOP_K, tm * ROW_TILES, LANES), F32),
                        pltpu.SemaphoreType.DMA, pltpu.SemaphoreType.DMA],
        compiler_params=_params("arbitrary"),
        name="moe_combine",
    )(dest, ys, x2, gates, gfin)


def _lru_gate_weights(w_a, b_a, w_x, b_x):
    hpb = LRU_BLOCK // LRU_HEAD_DIM
    ncb = D_LRU // LRU_BLOCK

    def blockdiag(w):
        w = w.reshape(ncb, hpb, LRU_HEAD_DIM, LRU_HEAD_DIM)
        eye = jnp.eye(hpb, dtype=w.dtype)
        full = jnp.einsum('nhij,hg->nhigj', w, eye)
        return full.reshape(ncb, LRU_BLOCK, LRU_BLOCK)

    mats = [blockdiag(w_a[0]), blockdiag(w_a[1]), blockdiag(w_x[0]), blockdiag(w_x[1])]
    w = jnp.concatenate(mats, axis=-1).astype(BF16)
    bs = [b.reshape(ncb, 1, LRU_BLOCK) for b in (b_a[0], b_a[1], b_x[0], b_x[1])]
    return w, jnp.concatenate(bs, axis=-1).astype(F32)


def _encoder_layer(x, mem, norm_mix, w_in, conv_w, conv_b, lru_w_a, lru_b_a, lru_w_x, lru_b_x,
                   lru_lambda, norm_lru_out, norm_fft_out, w_out, norm_xattn, norm_mem, w_q, w_kv,
                   w_o, norm_ffn, w_router, b_router, w_gate, b_gate, w_up, b_up, w_down, b_down,
                   norm_final):
    nb, seq, d = x.shape
    t = nb * seq
    ncb = D_LRU // LRU_BLOCK
    row = lambda v: v.reshape(1, -1).astype(F32)

    dft_c, f1, m2 = _dft_tables(seq)
    kt, v = _kv_proj(mem, row(norm_mem), w_kv[:, :d].T.astype(BF16), w_kv[:, d:].astype(BF16))

    xrec, gate, zr, zi = _in_proj(x.reshape(t, d), row(norm_mix), w_in.astype(BF16),
                                  dft_c.astype(BF16))

    w_gates, b_gates = _lru_gate_weights(lru_w_a, lru_b_a, lru_w_x, lru_b_x)
    cw = conv_w.reshape(CONV_WIDTH, ncb, LRU_BLOCK).transpose(1, 0, 2)
    cb = conv_b.reshape(ncb, 1, LRU_BLOCK)
    lam = lru_lambda.reshape(2, ncb, LRU_BLOCK).transpose(1, 0, 2)
    ylru = _lru_mix(xrec.reshape(nb, seq, D_LRU), gate.reshape(nb, seq, D_LRU), cw, cb,
                    w_gates, b_gates, lam)

    yfft = _seq_dft(zr.reshape(nb, seq, D_FFT), zi.reshape(nb, seq, D_FFT),
                    f1.astype(BF16), m2.astype(BF16))

    x2, h3, top_idx, gates, rank, counts = _attn_route(
        x, ylru, yfft, kt, v, row(norm_lru_out), row(norm_fft_out), w_out.astype(BF16),
        row(norm_xattn), w_q.astype(BF16), w_o.astype(BF16), row(norm_ffn),
        w_router.astype(F32), row(b_router))

    tmb = EXPERT_TILE
    nblocks = -(-(t * TOP_K) // tmb) + N_EXPERTS
    counts = counts.reshape(N_EXPERTS)
    padded = ((counts + tmb - 1) // tmb) * tmb
    pend = jnp.cumsum(padded)
    pstart = pend - padded
    top_idx = top_idx.reshape(t, TOP_K)
    onehot = top_idx[..., None] == jnp.arange(N_EXPERTS, dtype=I32)
    dest = rank.reshape(t, TOP_K) + jnp.sum(jnp.where(onehot, pstart, 0), axis=-1)
    dest = dest.reshape(t * TOP_K).astype(I32)
    blk_start = jnp.arange(nblocks, dtype=I32) * tmb
    bexp = jnp.clip(jnp.searchsorted(pend, blk_start, side='right'), 0, N_EXPERTS - 1).astype(I32)
    bvalid = jnp.clip(counts[bexp] - (blk_start - pstart[bexp]), 0, tmb).astype(I32)
    nblk = (pend[-1:] // tmb).astype(I32)

    xs = _moe_dispatch(dest, h3, nblocks * tmb)
    ys = _moe_experts(bexp, bvalid, nblk, xs,
                      w_gate.astype(BF16), b_gate.reshape(N_EXPERTS, 1, D_FF),
                      w_up.astype(BF16), b_up.reshape(N_EXPERTS, 1, D_FF),
                      w_down.astype(BF16), b_down.reshape(N_EXPERTS, 1, D_MODEL))
    out = _moe_combine(dest, ys, x2.reshape(t, d), gates.reshape(t, TOP_K), row(norm_final))
    return out.reshape(nb, seq, d)


def kernel(x_prompt, x_sample, mem_prompt, mem_sample, norm_mix, w_in, conv_w, conv_b, lru_w_a,
           lru_b_a, lru_w_x, lru_b_x, lru_lambda, norm_lru_out, norm_fft_out, w_out, norm_xattn,
           norm_mem, w_q, w_kv, w_o, norm_ffn, w_router, b_router, w_gate, b_gate, w_up, b_up,
           w_down, b_down, norm_final):
    assert x_prompt.shape[1:] == x_sample.shape[1:], "both groups must share (SEQ, D_MODEL)"
    assert w_in.shape[0] == 1, "single-layer block"
    nbp = x_prompt.shape[0]
    x = jnp.concatenate([x_prompt, x_sample], axis=0)
    mem = jnp.concatenate([mem_prompt, mem_sample], axis=0)
    y = _encoder_layer(x, mem, norm_mix[0], w_in[0], conv_w[0], conv_b[0], lru_w_a[0], lru_b_a[0],
                       lru_w_x[0], lru_b_x[0], lru_lambda[0], norm_lru_out[0], norm_fft_out[0],
                       w_out[0], norm_xattn[0], norm_mem[0], w_q[0], w_kv[0], w_o[0], norm_ffn[0],
                       w_router[0], b_router[0], w_gate[0], b_gate[0], w_up[0], b_up[0],
                       w_down[0], b_down[0], norm_final)
    return y[:nbp], y[nbp:]
```

```python
import functools
import math

import numpy as np
import jax
import jax.numpy as jnp
from jax import lax
from jax.experimental import pallas as pl
from jax.experimental.pallas import tpu as pltpu

F32, BF16, I32 = jnp.float32, jnp.bfloat16, jnp.int32

D_MODEL = 1024
D_LRU = 512
LRU_HEAD_DIM = 64
CONV_WIDTH = 4
LRU_C = 8.0
D_FFT = 512
FFT_GROUP_DIM = 128
N_XATTN_HEADS = 4
XATTN_HEAD_DIM = 256
N_EXPERTS = 32
TOP_K = 4
D_FF = 1024
SWIGLU_LIMIT = 7.0
SWIGLU_ALPHA = 1.702
EPS = 1e-6

LANES = 128
SUBLANES = 8
ROW_TILES = D_MODEL // LANES
TOKEN_TILE = 512
EXPERT_TILE = 512
FF_CHUNK = 256
LRU_BLOCK = 128
SCAN_CHUNK = 128
SCAN_PITCH = SCAN_CHUNK + SUBLANES
DFT_RADIX = 64
DFT_PITCH = DFT_RADIX + SUBLANES
DFT_UNROLL = 4
VMEM_LIMIT = 56 * 1024 * 1024


def _params(*sem):
    return pltpu.CompilerParams(dimension_semantics=sem, vmem_limit_bytes=VMEM_LIMIT)


def _rms(xf):
    return xf * lax.rsqrt(jnp.mean(xf * xf, axis=-1, keepdims=True) + EPS)


def _sigmoid(x):
    return 1.0 / (1.0 + jnp.exp(-x))


def _gelu_tanh(x):
    return 0.5 * x * (1.0 + jnp.tanh(math.sqrt(2.0 / math.pi) * (x + 0.044715 * (x * x * x))))


def _kv_kernel(mem_ref, g_ref, wkt_ref, wv_ref, kt_ref, v_ref):
    mn = (_rms(mem_ref[0]) * g_ref[...]).astype(BF16)
    kt = lax.dot_general(wkt_ref[...], mn, (((1,), (1,)), ((), ())), preferred_element_type=F32)
    kt_ref[0] = kt.astype(BF16)
    v_ref[0] = jnp.dot(mn, wv_ref[...], preferred_element_type=F32).astype(BF16)


def _kv_proj(mem, g, wkt, wv):
    nb, m, d = mem.shape
    return pl.pallas_call(
        _kv_kernel,
        out_shape=(jax.ShapeDtypeStruct((nb, d, m), BF16), jax.ShapeDtypeStruct((nb, m, d), BF16)),
        grid=(nb,),
        in_specs=[pl.BlockSpec((1, m, d), lambda b: (b, 0, 0)),
                  pl.BlockSpec((1, d), lambda b: (0, 0)),
                  pl.BlockSpec((d, d), lambda b: (0, 0)),
                  pl.BlockSpec((d, d), lambda b: (0, 0))],
        out_specs=(pl.BlockSpec((1, d, m), lambda b: (b, 0, 0)),
                   pl.BlockSpec((1, m, d), lambda b: (b, 0, 0))),
        compiler_params=_params("arbitrary"),
        name="kv_proj",
    )(mem, g, wkt, wv)


def _inproj_kernel(xp_ref, xs_ref, g_ref, w_ref, dft_ref, xrec_ref, gate_ref, zr_ref, zi_ref,
                   *, prompt_tiles):
    x = jnp.where(pl.program_id(0) < prompt_tiles, xp_ref[...], xs_ref[...])
    h = (_rms(x) * g_ref[...]).astype(BF16)
    proj = jnp.dot(h, w_ref[...], preferred_element_type=F32)
    xrec_ref[...] = proj[:, :D_LRU]
    gate_ref[...] = proj[:, D_LRU:2 * D_LRU].astype(BF16)
    z = jnp.dot(proj[:, 2 * D_LRU:].astype(BF16), dft_ref[...], preferred_element_type=F32)
    pad = jnp.zeros((DFT_PITCH - DFT_RADIX, D_FFT), F32)
    for m in range(x.shape[0] // DFT_RADIX):
        rows = slice(m * DFT_RADIX, (m + 1) * DFT_RADIX)
        dst = slice(m * DFT_PITCH, m * DFT_PITCH + DFT_RADIX)
        gap = slice(m * DFT_PITCH + DFT_RADIX, (m + 1) * DFT_PITCH)
        zr_ref[dst, :] = z[rows, :D_FFT]
        zi_ref[dst, :] = z[rows, D_FFT:]
        zr_ref[gap, :] = pad
        zi_ref[gap, :] = pad


def _in_proj(xp2d, xs2d, g, w_in, dft_c):
    tm = TOKEN_TILE
    ptiles = xp2d.shape[0] // tm
    t = xp2d.shape[0] + xs2d.shape[0]
    tz = t // DFT_RADIX * DFT_PITCH
    tmz = tm // DFT_RADIX * DFT_PITCH
    row = lambda i: (i, 0)
    full = lambda i: (0, 0)
    return pl.pallas_call(
        functools.partial(_inproj_kernel, prompt_tiles=ptiles),
        out_shape=(jax.ShapeDtypeStruct((t, D_LRU), F32), jax.ShapeDtypeStruct((t, D_LRU), BF16),
                   jax.ShapeDtypeStruct((tz, D_FFT), F32), jax.ShapeDtypeStruct((tz, D_FFT), F32)),
        grid=(t // tm,),
        in_specs=[pl.BlockSpec((tm, D_MODEL), lambda i: (jnp.minimum(i, ptiles - 1), 0)),
                  pl.BlockSpec((tm, D_MODEL), lambda i: (jnp.maximum(i - ptiles, 0), 0)),
                  pl.BlockSpec((1, D_MODEL), full),
                  pl.BlockSpec(w_in.shape, full), pl.BlockSpec(dft_c.shape, full)],
        out_specs=(pl.BlockSpec((tm, D_LRU), row), pl.BlockSpec((tm, D_LRU), row),
                   pl.BlockSpec((tmz, D_FFT), row), pl.BlockSpec((tmz, D_FFT), row)),
        compiler_params=_params("arbitrary"),
        name="in_proj",
    )(xp2d, xs2d, g, w_in, dft_c)


def _lru_kernel(x_ref, gate_ref, cw_ref, cb_ref, w_ref, b_ref, lam_ref, y_ref,
                af_ref, uf_ref, ab_ref, ub_ref, cf_ref, cbk_ref, *, seq):
    nch = seq // SCAN_CHUNK
    cw = cw_ref[0]
    cb = cb_ref[0]
    lam = lam_ref[0]
    sp = jnp.maximum(-lam, 0.0) + jnp.log1p(jnp.exp(-jnp.abs(lam)))
    bias = b_ref[0]
    c = LRU_BLOCK

    def gates_body(j, _):
        r0 = pl.multiple_of(j * SCAN_CHUNK, SCAN_CHUNK)
        main = x_ref[0, pl.ds(r0, SCAN_CHUNK), :]
        prev = x_ref[0, pl.ds(jnp.maximum(r0 - SUBLANES, 0), SUBLANES), :]
        nxt = x_ref[0, pl.ds(jnp.minimum(r0 + SCAN_CHUNK, seq - SUBLANES), SUBLANES), :]
        prev = jnp.where(j > 0, prev, 0.0)
        nxt = jnp.where(j < nch - 1, nxt, 0.0)
        win = jnp.concatenate([prev, main, nxt], axis=0)
        base = SUBLANES - CONV_WIDTH // 2
        xc = cb
        for tap in range(CONV_WIDTH):
            xc = xc + win[base + tap:base + tap + SCAN_CHUNK, :] * cw[tap:tap + 1, :]
        xcb = xc.astype(BF16)
        o0 = pl.multiple_of(j * SCAN_PITCH, SUBLANES)
        for d, (a_ref, u_ref) in enumerate(((af_ref, uf_ref), (ab_ref, ub_ref))):
            ga = jnp.dot(xcb, w_ref[0, :, d * c:(d + 1) * c], preferred_element_type=F32) \
                + bias[:, d * c:(d + 1) * c]
            gx = jnp.dot(xcb, w_ref[0, :, (2 + d) * c:(3 + d) * c], preferred_element_type=F32) \
                + bias[:, (2 + d) * c:(3 + d) * c]
            log_a = (-LRU_C) * _sigmoid(ga) * sp[d:d + 1, :]
            a = jnp.exp(log_a)
            u = jnp.sqrt(-jnp.tanh(log_a) * (1.0 + a * a)) * (_sigmoid(gx) * xc)
            a_ref[pl.ds(o0, SCAN_CHUNK), :] = a
            u_ref[pl.ds(o0, SCAN_CHUNK), :] = u
        return 0

    lax.fori_loop(0, nch, gates_body, 0)

    def scan_body(t, carry):
        hf, pf, hb, pb = carry
        fwd = pl.ds(t, nch, stride=SCAN_PITCH)
        bwd = pl.ds(SCAN_CHUNK - 1 - t, nch, stride=SCAN_PITCH)
        a = af_ref[fwd, :]
        hf = a * hf + uf_ref[fwd, :]
        pf = a * pf
        uf_ref[fwd, :] = hf
        af_ref[fwd, :] = pf
        a = ab_ref[bwd, :]
        hb = a * hb + ub_ref[bwd, :]
        pb = a * pb
        ub_ref[bwd, :] = hb
        ab_ref[bwd, :] = pb
        return hf, pf, hb, pb

    zeros = jnp.zeros((nch, c), F32)
    ones = jnp.ones((nch, c), F32)
    hf, pf, hb, pb = lax.fori_loop(0, SCAN_CHUNK, scan_body, (zeros, ones, zeros, ones))

    carry = jnp.zeros((1, c), F32)
    cf_ref[0:1, :] = carry
    for j in range(1, nch):
        carry = hf[j - 1:j, :] + pf[j - 1:j, :] * carry
        cf_ref[j:j + 1, :] = carry
    carry = jnp.zeros((1, c), F32)
    cbk_ref[nch - 1:nch, :] = carry
    for j in range(nch - 2, -1, -1):
        carry = hb[j + 1:j + 2, :] + pb[j + 1:j + 2, :] * carry
        cbk_ref[j:j + 1, :] = carry

    def out_body(j, _):
        r0 = pl.multiple_of(j * SCAN_CHUNK, SCAN_CHUNK)
        o0 = pl.multiple_of(j * SCAN_PITCH, SUBLANES)
        rows = pl.ds(o0, SCAN_CHUNK)
        h = (uf_ref[rows, :] + af_ref[rows, :] * cf_ref[pl.ds(j, 1), :]
             + ub_ref[rows, :] + ab_ref[rows, :] * cbk_ref[pl.ds(j, 1), :])
        g = gate_ref[0, pl.ds(r0, SCAN_CHUNK), :].astype(F32)
        y_ref[0, pl.ds(r0, SCAN_CHUNK), :] = (h * _gelu_tanh(g)).astype(BF16)
        return 0

    lax.fori_loop(0, nch, out_body, 0)


def _lru_mix(xrec, gate, conv_w, conv_b, w_gates, b_gates, lam):
    nb, seq, _ = xrec.shape
    ncb = D_LRU // LRU_BLOCK
    c = LRU_BLOCK
    nch = seq // SCAN_CHUNK
    act = lambda b, k: (b, 0, k)
    par = lambda b, k: (k, 0, 0)
    scan_buf = pltpu.VMEM((nch * SCAN_PITCH, c), F32)
    return pl.pallas_call(
        functools.partial(_lru_kernel, seq=seq),
        out_shape=jax.ShapeDtypeStruct((nb, seq, D_LRU), BF16),
        grid=(nb, ncb),
        in_specs=[pl.BlockSpec((1, seq, c), act), pl.BlockSpec((1, seq, c), act),
                  pl.BlockSpec((1, CONV_WIDTH, c), par), pl.BlockSpec((1, 1, c), par),
                  pl.BlockSpec((1, c, 4 * c), par), pl.BlockSpec((1, 1, 4 * c), par),
                  pl.BlockSpec((1, 2, c), par)],
        out_specs=pl.BlockSpec((1, seq, c), act),
        scratch_shapes=[scan_buf, scan_buf, scan_buf, scan_buf,
                        pltpu.VMEM((nch, c), F32), pltpu.VMEM((nch, c), F32)],
        compiler_params=_params("arbitrary", "arbitrary"),
        name="lru_mix",
    )(xrec, gate, conv_w, conv_b, w_gates, b_gates, lam)


def _seqdft_kernel(zr_ref, zi_ref, f1_ref, m2_ref, y_ref, ar_ref, ai_ref):
    r, pitch, unroll = DFT_RADIX, DFT_PITCH, DFT_UNROLL
    f1 = f1_ref[...]

    def stage1(it, _):
        i2 = it * unroll
        slabs = []
        for q in range(unroll):
            rows = pl.ds(i2 + q, r, stride=pitch)
            slabs.append(jnp.concatenate([zr_ref[0, rows, :], zi_ref[0, rows, :]], axis=0))
        rhs = jnp.concatenate(slabs, axis=1).astype(BF16)
        a = jnp.dot(f1, rhs, preferred_element_type=F32)
        for q in range(unroll):
            rows = pl.ds(i2 + q, r, stride=pitch)
            ar_ref[rows, :] = a[:r, q * LANES:(q + 1) * LANES]
            ai_ref[rows, :] = a[r:, q * LANES:(q + 1) * LANES]
        return 0

    lax.fori_loop(0, r // unroll, stage1, 0)

    def stage2(it, _):
        for q in range(unroll):
            k1 = it * unroll + q
            src = pl.ds(pl.multiple_of(k1 * pitch, SUBLANES), r)
            slab = jnp.concatenate([ar_ref[src, :], ai_ref[src, :]], axis=0).astype(BF16)
            o = jnp.dot(m2_ref[k1], slab, preferred_element_type=F32)
            y_ref[0, pl.ds(k1, r, stride=pitch), :] = o
        return 0

    lax.fori_loop(0, r // unroll, stage2, 0)
    for gap in range(r, pitch):
        y_ref[0, pl.ds(gap, r, stride=pitch), :] = jnp.zeros((r, LANES), F32)


def _seq_dft(zr, zi, f1, m2):
    nb, rows, _ = zr.shape
    cw = LANES
    act = lambda b, k: (b, 0, k)
    return pl.pallas_call(
        _seqdft_kernel,
        out_shape=jax.ShapeDtypeStruct((nb, rows, D_FFT), F32),
        grid=(nb, D_FFT // cw),
        in_specs=[pl.BlockSpec((1, rows, cw), act), pl.BlockSpec((1, rows, cw), act),
                  pl.BlockSpec(f1.shape, lambda b, k: (0, 0)),
                  pl.BlockSpec(m2.shape, lambda b, k: (0, 0, 0))],
        out_specs=pl.BlockSpec((1, rows, cw), act),
        scratch_shapes=[pltpu.VMEM((rows, cw), F32), pltpu.VMEM((rows, cw), F32)],
        compiler_params=_params("arbitrary", "arbitrary"),
        name="seq_dft",
    )(zr, zi, f1, m2)


def _dft_tables(seq):
    n1 = n2 = DFT_RADIX
    assert n1 * n2 == seq, "sequence DFT is factored as DFT_RADIX x DFT_RADIX"
    g = FFT_GROUP_DIM
    ang = 2.0 * np.pi * np.outer(np.arange(g), np.arange(g)) / g
    cg, sg = np.cos(ang) / math.sqrt(g), np.sin(ang) / math.sqrt(g)
    ngroups = D_FFT // g
    dft_c = np.zeros((D_FFT, 2 * D_FFT))
    for q in range(ngroups):
        dft_c[q * g:(q + 1) * g, q * g:(q + 1) * g] = cg
        dft_c[q * g:(q + 1) * g, D_FFT + q * g:D_FFT + (q + 1) * g] = -sg
    ang1 = 2.0 * np.pi * np.outer(np.arange(n1), np.arange(n1)) / n1
    c1, s1 = np.cos(ang1) / math.sqrt(n1), np.sin(ang1) / math.sqrt(n1)
    f1 = np.block([[c1, s1], [-s1, c1]])
    k1 = np.arange(n1)[:, None, None]
    k2 = np.arange(n2)[None, :, None]
    i2 = np.arange(n2)[None, None, :]
    ang2 = 2.0 * np.pi * (i2 * k2 / n2 + i2 * k1 / seq)
    m2 = np.concatenate([np.cos(ang2), np.sin(ang2)], axis=2) / math.sqrt(n2)
    return (jnp.asarray(dft_c, F32), jnp.asarray(f1, F32), jnp.asarray(m2, F32))


def _attn_route_kernel(xp_ref, xs_ref, ylru_ref, yfft_ref, kt_ref, v_ref, gl_ref, gf_ref, wout_ref,
                       gx_ref, wq_ref, wo_ref, gffn_ref, wr_ref, br_ref,
                       x2_ref, h3_ref, idx_ref, gate_ref, rank_ref, cnt_ref, run_ref,
                       *, prompt_batches):
    first = jnp.logical_and(pl.program_id(0) == 0, pl.program_id(1) == 0)

    @pl.when(first)
    def _():
        run_ref[...] = jnp.zeros_like(run_ref)

    tm = xp_ref.shape[1]
    x = jnp.where(pl.program_id(0) < prompt_batches, xp_ref[0], xs_ref[0])
    m_lru = _rms(ylru_ref[0].astype(F32)) * gl_ref[...]
    yfft = jnp.concatenate([yfft_ref[0, m * DFT_PITCH:m * DFT_PITCH + DFT_RADIX, :]
                            for m in range(tm // DFT_RADIX)], axis=0)
    m_fft = _rms(yfft) * gf_ref[...]
    mixed = jnp.concatenate([m_lru, m_fft], axis=-1).astype(BF16)
    x1 = x + jnp.dot(mixed, wout_ref[...], preferred_element_type=F32)

    hq = (_rms(x1) * gx_ref[...]).astype(BF16)
    q = jnp.dot(hq, wq_ref[...], preferred_element_type=F32) * (XATTN_HEAD_DIM ** -0.5)
    qb = q.astype(BF16)
    heads = []
    for h in range(N_XATTN_HEADS):
        sl = slice(h * XATTN_HEAD_DIM, (h + 1) * XATTN_HEAD_DIM)
        s = jnp.dot(qb[:, sl], kt_ref[0, sl, :], preferred_element_type=F32)
        p = jnp.exp(s - jnp.max(s, axis=-1, keepdims=True))
        inv = 1.0 / jnp.sum(p, axis=-1, keepdims=True)
        o = jnp.dot(p.astype(BF16), v_ref[0, :, sl], preferred_element_type=F32) * inv
        heads.append(o.astype(BF16))
    att = jnp.concatenate(heads, axis=-1)
    x2 = x1 + jnp.dot(att, wo_ref[...], preferred_element_type=F32)
    x2_ref[0] = x2

    h3 = _rms(x2) * gffn_ref[...]
    for s in range(ROW_TILES):
        h3_ref[pl.ds(s, tm, stride=ROW_TILES), :] = h3[:, s * LANES:(s + 1) * LANES]

    logits = jnp.dot(h3, wr_ref[...], preferred_element_type=F32,
                     precision=lax.Precision.HIGHEST) + br_ref[...]
    lane_e = lax.broadcasted_iota(I32, logits.shape, 1)
    lane_k = lax.broadcasted_iota(I32, (tm, TOP_K), 1)
    lg = logits
    vals, onehots = [], []
    idx_out = jnp.zeros((tm, TOP_K), I32)
    for k in range(TOP_K):
        m = jnp.max(lg, axis=-1, keepdims=True)
        idx = jnp.min(jnp.where(lg == m, lane_e, N_EXPERTS), axis=-1, keepdims=True)
        oh = lane_e == idx
        vals.append(m)
        onehots.append(oh)
        idx_out = jnp.where(lane_k == k, idx, idx_out)
        lg = jnp.where(oh, -jnp.inf, lg)
    idx_ref[0] = idx_out

    exps = [jnp.exp(v - vals[0]) for v in vals]
    inv = 1.0 / (exps[0] + exps[1] + exps[2] + exps[3])
    gate_out = jnp.zeros((tm, TOP_K), F32)
    for k in range(TOP_K):
        gate_out = jnp.where(lane_k == k, exps[k] * inv, gate_out)
    gate_ref[0] = gate_out

    member = jnp.zeros(logits.shape, F32)
    for oh in onehots:
        member = member + oh.astype(F32)
    r_i = lax.broadcasted_iota(I32, (tm, tm), 0)
    c_i = lax.broadcasted_iota(I32, (tm, tm), 1)
    ltri = jnp.where(r_i > c_i, 1.0, 0.0).astype(BF16)
    before = jnp.dot(ltri, member.astype(BF16), preferred_element_type=F32) + run_ref[...]
    rank_out = jnp.zeros((tm, TOP_K), I32)
    for k in range(TOP_K):
        rk = jnp.sum(jnp.where(onehots[k], before, 0.0), axis=-1, keepdims=True)
        rank_out = jnp.where(lane_k == k, rk.astype(I32), rank_out)
    rank_ref[0] = rank_out
    total = run_ref[...] + jnp.sum(member, axis=0, keepdims=True)
    run_ref[...] = total
    cnt_ref[...] = total.astype(I32)


def _attn_route(xp, xs, ylru, yfft, kt, v, gl, gf, w_out, gx, wq, wo, gffn, wr, br):
    nbp, seq, d = xp.shape
    nb = nbp + xs.shape[0]
    tm = TOKEN_TILE
    tmz = tm // DFT_RADIX * DFT_PITCH
    nt = seq // tm
    m = v.shape[1]
    tok = lambda b, i: (b, i, 0)
    full2 = lambda b, i: (0, 0)
    per_b = lambda b, i: (b, 0, 0)
    t = nb * seq
    return pl.pallas_call(
        functools.partial(_attn_route_kernel, prompt_batches=nbp),
        out_shape=(jax.ShapeDtypeStruct((nb, seq, d), F32),
                   jax.ShapeDtypeStruct((t * ROW_TILES, LANES), F32),
                   jax.ShapeDtypeStruct((nb, seq, TOP_K), I32),
                   jax.ShapeDtypeStruct((nb, seq, TOP_K), F32),
                   jax.ShapeDtypeStruct((nb, seq, TOP_K), I32),
                   jax.ShapeDtypeStruct((1, N_EXPERTS), I32)),
        grid=(nb, nt),
        in_specs=[pl.BlockSpec((1, tm, d), lambda b, i: (jnp.minimum(b, nbp - 1), i, 0)),
                  pl.BlockSpec((1, tm, d), lambda b, i: (jnp.maximum(b - nbp, 0), i, 0)),
                  pl.BlockSpec((1, tm, D_LRU), tok),
                  pl.BlockSpec((1, tmz, D_FFT), tok),
                  pl.BlockSpec((1, d, m), per_b), pl.BlockSpec((1, m, d), per_b),
                  pl.BlockSpec((1, D_LRU), full2), pl.BlockSpec((1, D_FFT), full2),
                  pl.BlockSpec((d, d), full2), pl.BlockSpec((1, d), full2),
                  pl.BlockSpec((d, d), full2), pl.BlockSpec((d, d), full2),
                  pl.BlockSpec((1, d), full2), pl.BlockSpec((d, N_EXPERTS), full2),
                  pl.BlockSpec((1, N_EXPERTS), full2)],
        out_specs=(pl.BlockSpec((1, tm, d), tok),
                   pl.BlockSpec((tm * ROW_TILES, LANES), lambda b, i: (b * nt + i, 0)),
                   pl.BlockSpec((1, tm, TOP_K), tok), pl.BlockSpec((1, tm, TOP_K), tok),
                   pl.BlockSpec((1, tm, TOP_K), tok),
                   pl.BlockSpec((1, N_EXPERTS), full2)),
        scratch_shapes=[pltpu.VMEM((1, N_EXPERTS), F32)],
        compiler_params=_params("arbitrary", "arbitrary"),
        name="attn_route",
    )(xp, xs, ylru, yfft, kt, v, gl, gf, w_out, gx, wq, wo, gffn, wr, br)


def _row_copy(src_ref, src_row, dst_ref, dst_row, sem):
    return pltpu.make_async_copy(
        src_ref.at[pl.ds(pl.multiple_of(src_row * ROW_TILES, ROW_TILES), ROW_TILES), :],
        dst_ref.at[pl.ds(pl.multiple_of(dst_row * ROW_TILES, ROW_TILES), ROW_TILES), :],
        sem)


def _dispatch_kernel(dest_hbm, h3_ref, xs_hbm, dsm, isem, sem):
    i = pl.program_id(0)
    tm = h3_ref.shape[0] // ROW_TILES
    n = tm * TOP_K
    idx_cp = pltpu.make_async_copy(dest_hbm.at[pl.ds(pl.multiple_of(i * n, n), n)], dsm, isem)
    idx_cp.start()
    idx_cp.wait()

    def issue(a, _):
        _row_copy(h3_ref, a // TOP_K, xs_hbm, dsm[a], sem).start()
        return 0

    lax.fori_loop(0, n, issue, 0, unroll=8)
    for _ in range(TOP_K):
        pltpu.make_async_copy(h3_ref, xs_hbm.at[pl.ds(0, tm * ROW_TILES), :], sem).wait()


def _moe_dispatch(dest, h3, p_rows):
    t = h3.shape[0] // ROW_TILES
    tm = TOKEN_TILE
    return pl.pallas_call(
        _dispatch_kernel,
        out_shape=jax.ShapeDtypeStruct((p_rows * ROW_TILES, LANES), F32),
        grid=(t // tm,),
        in_specs=[pl.BlockSpec(memory_space=pl.ANY),
                  pl.BlockSpec((tm * ROW_TILES, LANES), lambda i: (i, 0))],
        out_specs=pl.BlockSpec(memory_space=pl.ANY),
        scratch_shapes=[pltpu.SMEM((tm * TOP_K,), I32), pltpu.SemaphoreType.DMA,
                        pltpu.SemaphoreType.DMA],
        compiler_params=_params("arbitrary"),
        name="moe_dispatch",
    )(dest, h3)


def _experts_kernel(bexp_ref, bvalid_ref, nblk_ref, xs_ref, wg_ref, bg_ref, wu_ref, bu_ref,
                    wd_ref, bd_ref, ys_ref, acc_ref):
    i = pl.program_id(0)

    @pl.when(i < nblk_ref[0])
    def _():
        tm = xs_ref.shape[0] // ROW_TILES
        x = jnp.concatenate([xs_ref[pl.ds(s, tm, stride=ROW_TILES), :] for s in range(ROW_TILES)],
                            axis=-1)
        row = lax.broadcasted_iota(I32, (tm, 1), 0)
        xb = jnp.where(row < bvalid_ref[i], x, 0.0).astype(BF16)
        for cidx in range(D_FF // FF_CHUNK):
            sl = slice(cidx * FF_CHUNK, (cidx + 1) * FF_CHUNK)
            gt = jnp.dot(xb, wg_ref[0, :, sl], preferred_element_type=F32) + bg_ref[0, :, sl]
            up = jnp.dot(xb, wu_ref[0, :, sl], preferred_element_type=F32) + bu_ref[0, :, sl]
            gt = jnp.minimum(gt, SWIGLU_LIMIT)
            up = jnp.clip(up, -SWIGLU_LIMIT, SWIGLU_LIMIT)
            act = (gt * _sigmoid(SWIGLU_ALPHA * gt) * (up + 1.0)).astype(BF16)
            part = jnp.dot(act, wd_ref[0, sl, :], preferred_element_type=F32)
            if cidx == 0:
                acc_ref[...] = part + bd_ref[0]
            else:
                acc_ref[...] += part
        for s in range(ROW_TILES):
            ys_ref[pl.ds(s, tm, stride=ROW_TILES), :] = acc_ref[:, s * LANES:(s + 1) * LANES]


def _moe_experts(bexp, bvalid, nblk, xs, wg, bg, wu, bu, wd, bd):
    nblocks = bexp.shape[0]
    tm = EXPERT_TILE
    blk = lambda i, be, bv, nb: (jnp.minimum(i, nb[0] - 1), 0)
    wsel = lambda i, be, bv, nb: (be[jnp.minimum(i, nb[0] - 1)], 0, 0)
    wspec = pl.BlockSpec((1, D_MODEL, D_FF), wsel)
    bspec = pl.BlockSpec((1, 1, D_FF), wsel)
    return pl.pallas_call(
        _experts_kernel,
        out_shape=jax.ShapeDtypeStruct(xs.shape, F32),
        grid_spec=pltpu.PrefetchScalarGridSpec(
            num_scalar_prefetch=3,
            grid=(nblocks,),
            in_specs=[pl.BlockSpec((tm * ROW_TILES, LANES), blk),
                      wspec, bspec, wspec, bspec, wspec, bspec],
            out_specs=pl.BlockSpec((tm * ROW_TILES, LANES), blk),
            scratch_shapes=[pltpu.VMEM((tm, D_MODEL), F32)]),
        compiler_params=_params("arbitrary"),
        name="moe_experts",
    )(bexp, bvalid, nblk, xs, wg, bg, wu, bu, wd, bd)


def _combine_kernel(dest_hbm, ys_hbm, x2_ref, gate_ref, gfin_ref, outp_ref, outs_ref, dsm, cbuf,
                    isem, sem, *, prompt_tiles):
    i = pl.program_id(0)
    tm = x2_ref.shape[0]
    n = tm * TOP_K
    idx_cp = pltpu.make_async_copy(dest_hbm.at[pl.ds(pl.multiple_of(i * n, n), n)], dsm, isem)
    idx_cp.start()
    idx_cp.wait()

    def issue(tk, _):
        for k in range(TOP_K):
            _row_copy(ys_hbm, dsm[tk * TOP_K + k], cbuf.at[k], tk, sem).start()
        return 0

    lax.fori_loop(0, tm, issue, 0, unroll=2)
    for k in range(TOP_K):
        pltpu.make_async_copy(ys_hbm.at[pl.ds(0, tm * ROW_TILES), :], cbuf.at[k], sem).wait()

    acc = x2_ref[...]
    g = gate_ref[...]
    for k in range(TOP_K):
        yk = jnp.concatenate([cbuf[k, pl.ds(s, tm, stride=ROW_TILES), :] for s in range(ROW_TILES)],
                             axis=-1)
        acc = acc + g[:, k:k + 1] * yk
    y = _rms(acc) * gfin_ref[...]

    @pl.when(i < prompt_tiles)
    def _():
        outp_ref[...] = y

    @pl.when(i >= prompt_tiles)
    def _():
        outs_ref[...] = y


def _moe_combine(dest, ys, x2, gates, gfin, t_prompt):
    t, d = x2.shape
    tm = TOKEN_TILE
    ptiles = t_prompt // tm
    return pl.pallas_call(
        functools.partial(_combine_kernel, prompt_tiles=ptiles),
        out_shape=(jax.ShapeDtypeStruct((t_prompt, d), F32),
                   jax.ShapeDtypeStruct((t - t_prompt, d), F32)),
        grid=(t // tm,),
        in_specs=[pl.BlockSpec(memory_space=pl.ANY), pl.BlockSpec(memory_space=pl.ANY),
                  pl.BlockSpec((tm, d), lambda i: (i, 0)),
                  pl.BlockSpec((tm, TOP_K), lambda i: (i, 0)),
                  pl.BlockSpec((1, d), lambda i: (0, 0))],
        out_specs=(pl.BlockSpec((tm, d), lambda i: (jnp.minimum(i, ptiles - 1), 0)),
                   pl.BlockSpec((tm, d), lambda i: (jnp.maximum(i - ptiles, 0), 0))),
        scratch_shapes=[pltpu.SMEM((tm * TOP_K,), I32),
                        pltpu.VMEM((TOP_K, tm * ROW_TILES, LANES), F32),
                        pltpu.SemaphoreType.DMA, pltpu.SemaphoreType.DMA],
        compiler_params=_params("arbitrary"),
        name="moe_combine",
    )(dest, ys, x2, gates, gfin)


def _lru_gate_weights(w_a, b_a, w_x, b_x):
    hpb = LRU_BLOCK // LRU_HEAD_DIM
    ncb = D_LRU // LRU_BLOCK

    def blockdiag(w):
        w = w.reshape(ncb, hpb, LRU_HEAD_DIM, LRU_HEAD_DIM)
        eye = jnp.eye(hpb, dtype=w.dtype)
        full = jnp.einsum('nhij,hg->nhigj', w, eye)
        return full.reshape(ncb, LRU_BLOCK, LRU_BLOCK)

    mats = [blockdiag(w_a[0]), blockdiag(w_a[1]), blockdiag(w_x[0]), blockdiag(w_x[1])]
    w = jnp.concatenate(mats, axis=-1).astype(BF16)
    bs = [b.reshape(ncb, 1, LRU_BLOCK) for b in (b_a[0], b_a[1], b_x[0], b_x[1])]
    return w, jnp.concatenate(bs, axis=-1).astype(F32)


def _encoder_layer(xp, xs, mem, norm_mix, w_in, conv_w, conv_b, lru_w_a, lru_b_a, lru_w_x, lru_b_x,
                   lru_lambda, norm_lru_out, norm_fft_out, w_out, norm_xattn, norm_mem, w_q, w_kv,
                   w_o, norm_ffn, w_router, b_router, w_gate, b_gate, w_up, b_up, w_down, b_down,
                   norm_final):
    nbp, seq, d = xp.shape
    nb = nbp + xs.shape[0]
    t = nb * seq
    seqz = seq // DFT_RADIX * DFT_PITCH
    ncb = D_LRU // LRU_BLOCK
    row = lambda v: v.reshape(1, -1).astype(F32)

    dft_c, f1, m2 = _dft_tables(seq)
    kt, v = _kv_proj(mem, row(norm_mem), w_kv[:, :d].T.astype(BF16), w_kv[:, d:].astype(BF16))

    xrec, gate, zr, zi = _in_proj(xp.reshape(nbp * seq, d), xs.reshape(t - nbp * seq, d),
                                  row(norm_mix), w_in.astype(BF16), dft_c.astype(BF16))

    w_gates, b_gates = _lru_gate_weights(lru_w_a, lru_b_a, lru_w_x, lru_b_x)
    cw = conv_w.reshape(CONV_WIDTH, ncb, LRU_BLOCK).transpose(1, 0, 2)
    cb = conv_b.reshape(ncb, 1, LRU_BLOCK)
    lam = lru_lambda.reshape(2, ncb, LRU_BLOCK).transpose(1, 0, 2)
    ylru = _lru_mix(xrec.reshape(nb, seq, D_LRU), gate.reshape(nb, seq, D_LRU), cw, cb,
                    w_gates, b_gates, lam)

    yfft = _seq_dft(zr.reshape(nb, seqz, D_FFT), zi.reshape(nb, seqz, D_FFT),
                    f1.astype(BF16), m2.astype(BF16))

    x2, h3, top_idx, gates, rank, counts = _attn_route(
        xp, xs, ylru, yfft, kt, v, row(norm_lru_out), row(norm_fft_out), w_out.astype(BF16),
        row(norm_xattn), w_q.astype(BF16), w_o.astype(BF16), row(norm_ffn),
        w_router.astype(F32), row(b_router))

    tmb = EXPERT_TILE
    nblocks = -(-(t * TOP_K) // tmb) + N_EXPERTS
    counts = counts.reshape(N_EXPERTS)
    padded = ((counts + tmb - 1) // tmb) * tmb
    pend = jnp.cumsum(padded)
    pstart = pend - padded
    top_idx = top_idx.reshape(t, TOP_K)
    onehot = top_idx[..., None] == jnp.arange(N_EXPERTS, dtype=I32)
    dest = rank.reshape(t, TOP_K) + jnp.sum(jnp.where(onehot, pstart, 0), axis=-1)
    dest = dest.reshape(t * TOP_K).astype(I32)
    blk_start = jnp.arange(nblocks, dtype=I32) * tmb
    bexp = jnp.minimum(jnp.sum((pend[None, :] <= blk_start[:, None]).astype(I32), axis=1),
                       N_EXPERTS - 1)
    in_exp = bexp[:, None] == jnp.arange(N_EXPERTS, dtype=I32)[None, :]
    left = jnp.sum(jnp.where(in_exp, (pstart + counts)[None, :], 0), axis=1) - blk_start
    bvalid = jnp.clip(left, 0, tmb).astype(I32)
    nblk = (pend[-1:] // tmb).astype(I32)

    xs = _moe_dispatch(dest, h3, nblocks * tmb)
    ys = _moe_experts(bexp, bvalid, nblk, xs,
                      w_gate.astype(BF16), b_gate.reshape(N_EXPERTS, 1, D_FF),
                      w_up.astype(BF16), b_up.reshape(N_EXPERTS, 1, D_FF),
                      w_down.astype(BF16), b_down.reshape(N_EXPERTS, 1, D_MODEL))
    outp, outs = _moe_combine(dest, ys, x2.reshape(t, d), gates.reshape(t, TOP_K), row(norm_final),
                              nbp * seq)
    return outp.reshape(nbp, seq, d), outs.reshape(nb - nbp, seq, d)


def kernel(x_prompt, x_sample, mem_prompt, mem_sample, norm_mix, w_in, conv_w, conv_b, lru_w_a,
           lru_b_a, lru_w_x, lru_b_x, lru_lambda, norm_lru_out, norm_fft_out, w_out, norm_xattn,
           norm_mem, w_q, w_kv, w_o, norm_ffn, w_router, b_router, w_gate, b_gate, w_up, b_up,
           w_down, b_down, norm_final):
    assert x_prompt.shape[1:] == x_sample.shape[1:], "both groups must share (SEQ, D_MODEL)"
    assert w_in.shape[0] == 1, "single-layer block"
    mem = jnp.concatenate([mem_prompt, mem_sample], axis=0)
    return _encoder_layer(x_prompt, x_sample, mem, norm_mix[0], w_in[0], conv_w[0], conv_b[0],
                          lru_w_a[0], lru_b_a[0], lru_w_x[0], lru_b_x[0], lru_lambda[0],
                          norm_lru_out[0], norm_fft_out[0], w_out[0], norm_xattn[0], norm_mem[0],
                          w_q[0], w_kv[0], w_o[0], norm_ffn[0], w_router[0], b_router[0],
                          w_gate[0], b_gate[0], w_up[0], b_up[0], w_down[0], b_down[0], norm_final)
```

```python
import functools
import math

import numpy as np
import jax
import jax.numpy as jnp
from jax import lax
from jax.experimental import pallas as pl
from jax.experimental.pallas import tpu as pltpu

F32, BF16, I32 = jnp.float32, jnp.bfloat16, jnp.int32

D_MODEL = 1024
D_LRU = 512
LRU_HEAD_DIM = 64
CONV_WIDTH = 4
LRU_C = 8.0
D_FFT = 512
FFT_GROUP_DIM = 128
N_XATTN_HEADS = 4
XATTN_HEAD_DIM = 256
N_EXPERTS = 32
TOP_K = 4
D_FF = 1024
SWIGLU_LIMIT = 7.0
SWIGLU_ALPHA = 1.702
EPS = 1e-6

LANES = 128
SUBLANES = 8
ROW_TILES = D_MODEL // LANES
TOKEN_TILE = 512
EXPERT_TILE = 512
FF_CHUNK = 256
LRU_BLOCK = 128
SCAN_CHUNK = 128
SCAN_PITCH = SCAN_CHUNK + SUBLANES
DFT_RADIX = 64
DFT_PITCH = DFT_RADIX + SUBLANES
DFT_UNROLL = 4
VMEM_LIMIT = 56 * 1024 * 1024


def _params(*sem):
    return pltpu.CompilerParams(dimension_semantics=sem, vmem_limit_bytes=VMEM_LIMIT)


def _rms(xf):
    return xf * lax.rsqrt(jnp.mean(xf * xf, axis=-1, keepdims=True) + EPS)


def _sigmoid(x):
    return 1.0 / (1.0 + jnp.exp(-x))


def _gelu_tanh(x):
    return 0.5 * x * (1.0 + jnp.tanh(math.sqrt(2.0 / math.pi) * (x + 0.044715 * (x * x * x))))


def _kv_kernel(mem_ref, g_ref, wkt_ref, wv_ref, kt_ref, v_ref):
    mn = (_rms(mem_ref[0]) * g_ref[...]).astype(BF16)
    kt = lax.dot_general(wkt_ref[...], mn, (((1,), (1,)), ((), ())), preferred_element_type=F32)
    kt_ref[0] = kt.astype(BF16)
    v_ref[0] = jnp.dot(mn, wv_ref[...], preferred_element_type=F32).astype(BF16)


def _kv_proj(mem, g, wkt, wv):
    nb, m, d = mem.shape
    return pl.pallas_call(
        _kv_kernel,
        out_shape=(jax.ShapeDtypeStruct((nb, d, m), BF16), jax.ShapeDtypeStruct((nb, m, d), BF16)),
        grid=(nb,),
        in_specs=[pl.BlockSpec((1, m, d), lambda b: (b, 0, 0)),
                  pl.BlockSpec((1, d), lambda b: (0, 0)),
                  pl.BlockSpec((d, d), lambda b: (0, 0)),
                  pl.BlockSpec((d, d), lambda b: (0, 0))],
        out_specs=(pl.BlockSpec((1, d, m), lambda b: (b, 0, 0)),
                   pl.BlockSpec((1, m, d), lambda b: (b, 0, 0))),
        compiler_params=_params("arbitrary"),
        name="kv_proj",
    )(mem, g, wkt, wv)


def _inproj_kernel(xp_ref, xs_ref, g_ref, w_ref, dft_ref, xrec_ref, gate_ref, zr_ref, zi_ref,
                   *, prompt_tiles):
    x = jnp.where(pl.program_id(0) < prompt_tiles, xp_ref[...], xs_ref[...])
    h = (_rms(x) * g_ref[...]).astype(BF16)
    proj = jnp.dot(h, w_ref[...], preferred_element_type=F32)
    xrec_ref[...] = proj[:, :D_LRU]
    gate_ref[...] = proj[:, D_LRU:2 * D_LRU].astype(BF16)
    z = jnp.dot(proj[:, 2 * D_LRU:].astype(BF16), dft_ref[...], preferred_element_type=F32)
    pad = jnp.zeros((DFT_PITCH - DFT_RADIX, D_FFT), F32)
    for m in range(x.shape[0] // DFT_RADIX):
        rows = slice(m * DFT_RADIX, (m + 1) * DFT_RADIX)
        dst = slice(m * DFT_PITCH, m * DFT_PITCH + DFT_RADIX)
        gap = slice(m * DFT_PITCH + DFT_RADIX, (m + 1) * DFT_PITCH)
        zr_ref[dst, :] = z[rows, :D_FFT]
        zi_ref[dst, :] = z[rows, D_FFT:]
        zr_ref[gap, :] = pad
        zi_ref[gap, :] = pad


def _in_proj(xp2d, xs2d, g, w_in, dft_c):
    tm = TOKEN_TILE
    ptiles = xp2d.shape[0] // tm
    t = xp2d.shape[0] + xs2d.shape[0]
    tz = t // DFT_RADIX * DFT_PITCH
    tmz = tm // DFT_RADIX * DFT_PITCH
    row = lambda i: (i, 0)
    full = lambda i: (0, 0)
    return pl.pallas_call(
        functools.partial(_inproj_kernel, prompt_tiles=ptiles),
        out_shape=(jax.ShapeDtypeStruct((t, D_LRU), F32), jax.ShapeDtypeStruct((t, D_LRU), BF16),
                   jax.ShapeDtypeStruct((tz, D_FFT), F32), jax.ShapeDtypeStruct((tz, D_FFT), F32)),
        grid=(t // tm,),
        in_specs=[pl.BlockSpec((tm, D_MODEL), lambda i: (jnp.minimum(i, ptiles - 1), 0)),
                  pl.BlockSpec((tm, D_MODEL), lambda i: (jnp.maximum(i - ptiles, 0), 0)),
                  pl.BlockSpec((1, D_MODEL), full),
                  pl.BlockSpec(w_in.shape, full), pl.BlockSpec(dft_c.shape, full)],
        out_specs=(pl.BlockSpec((tm, D_LRU), row), pl.BlockSpec((tm, D_LRU), row),
                   pl.BlockSpec((tmz, D_FFT), row), pl.BlockSpec((tmz, D_FFT), row)),
        compiler_params=_params("arbitrary"),
        name="in_proj",
    )(xp2d, xs2d, g, w_in, dft_c)


def _lru_kernel(x_ref, gate_ref, cw_ref, cb_ref, w_ref, b_ref, lam_ref, y_ref,
                af_ref, uf_ref, ab_ref, ub_ref, cf_ref, cbk_ref, *, seq):
    nch = seq // SCAN_CHUNK
    cw = cw_ref[0]
    cb = cb_ref[0]
    lam = lam_ref[0]
    sp = jnp.maximum(-lam, 0.0) + jnp.log1p(jnp.exp(-jnp.abs(lam)))
    bias = b_ref[0]
    c = LRU_BLOCK

    def gates_body(j, _):
        r0 = pl.multiple_of(j * SCAN_CHUNK, SCAN_CHUNK)
        main = x_ref[0, pl.ds(r0, SCAN_CHUNK), :]
        prev = x_ref[0, pl.ds(jnp.maximum(r0 - SUBLANES, 0), SUBLANES), :]
        nxt = x_ref[0, pl.ds(jnp.minimum(r0 + SCAN_CHUNK, seq - SUBLANES), SUBLANES), :]
        prev = jnp.where(j > 0, prev, 0.0)
        nxt = jnp.where(j < nch - 1, nxt, 0.0)
        win = jnp.concatenate([prev, main, nxt], axis=0)
        base = SUBLANES - CONV_WIDTH // 2
        xc = cb
        for tap in range(CONV_WIDTH):
            xc = xc + win[base + tap:base + tap + SCAN_CHUNK, :] * cw[tap:tap + 1, :]
        xcb = xc.astype(BF16)
        o0 = pl.multiple_of(j * SCAN_PITCH, SUBLANES)
        for d, (a_ref, u_ref) in enumerate(((af_ref, uf_ref), (ab_ref, ub_ref))):
            ga = jnp.dot(xcb, w_ref[0, :, d * c:(d + 1) * c], preferred_element_type=F32) \
                + bias[:, d * c:(d + 1) * c]
            gx = jnp.dot(xcb, w_ref[0, :, (2 + d) * c:(3 + d) * c], preferred_element_type=F32) \
                + bias[:, (2 + d) * c:(3 + d) * c]
            log_a = (-LRU_C) * _sigmoid(ga) * sp[d:d + 1, :]
            a = jnp.exp(log_a)
            u = jnp.sqrt(-jnp.tanh(log_a) * (1.0 + a * a)) * (_sigmoid(gx) * xc)
            a_ref[pl.ds(o0, SCAN_CHUNK), :] = a
            u_ref[pl.ds(o0, SCAN_CHUNK), :] = u
        return 0

    lax.fori_loop(0, nch, gates_body, 0)

    def scan_body(t, carry):
        hf, pf, hb, pb = carry
        fwd = pl.ds(t, nch, stride=SCAN_PITCH)
        bwd = pl.ds(SCAN_CHUNK - 1 - t, nch, stride=SCAN_PITCH)
        a = af_ref[fwd, :]
        hf = a * hf + uf_ref[fwd, :]
        pf = a * pf
        uf_ref[fwd, :] = hf
        af_ref[fwd, :] = pf
        a = ab_ref[bwd, :]
        hb = a * hb + ub_ref[bwd, :]
        pb = a * pb
        ub_ref[bwd, :] = hb
        ab_ref[bwd, :] = pb
        return hf, pf, hb, pb

    zeros = jnp.zeros((nch, c), F32)
    ones = jnp.ones((nch, c), F32)
    hf, pf, hb, pb = lax.fori_loop(0, SCAN_CHUNK, scan_body, (zeros, ones, zeros, ones))

    carry = jnp.zeros((1, c), F32)
    cf_ref[0:1, :] = carry
    for j in range(1, nch):
        carry = hf[j - 1:j, :] + pf[j - 1:j, :] * carry
        cf_ref[j:j + 1, :] = carry
    carry = jnp.zeros((1, c), F32)
    cbk_ref[nch - 1:nch, :] = carry
    for j in range(nch - 2, -1, -1):
        carry = hb[j + 1:j + 2, :] + pb[j + 1:j + 2, :] * carry
        cbk_ref[j:j + 1, :] = carry

    def out_body(j, _):
        r0 = pl.multiple_of(j * SCAN_CHUNK, SCAN_CHUNK)
        o0 = pl.multiple_of(j * SCAN_PITCH, SUBLANES)
        rows = pl.ds(o0, SCAN_CHUNK)
        h = (uf_ref[rows, :] + af_ref[rows, :] * cf_ref[pl.ds(j, 1), :]
             + ub_ref[rows, :] + ab_ref[rows, :] * cbk_ref[pl.ds(j, 1), :])
        g = gate_ref[0, pl.ds(r0, SCAN_CHUNK), :].astype(F32)
        y_ref[0, pl.ds(r0, SCAN_CHUNK), :] = (h * _gelu_tanh(g)).astype(BF16)
        return 0

    lax.fori_loop(0, nch, out_body, 0)


def _lru_mix(xrec, gate, conv_w, conv_b, w_gates, b_gates, lam):
    nb, seq, _ = xrec.shape
    ncb = D_LRU // LRU_BLOCK
    c = LRU_BLOCK
    nch = seq // SCAN_CHUNK
    act = lambda b, k: (b, 0, k)
    par = lambda b, k: (k, 0, 0)
    scan_buf = pltpu.VMEM((nch * SCAN_PITCH, c), F32)
    return pl.pallas_call(
        functools.partial(_lru_kernel, seq=seq),
        out_shape=jax.ShapeDtypeStruct((nb, seq, D_LRU), BF16),
        grid=(nb, ncb),
        in_specs=[pl.BlockSpec((1, seq, c), act), pl.BlockSpec((1, seq, c), act),
                  pl.BlockSpec((1, CONV_WIDTH, c), par), pl.BlockSpec((1, 1, c), par),
                  pl.BlockSpec((1, c, 4 * c), par), pl.BlockSpec((1, 1, 4 * c), par),
                  pl.BlockSpec((1, 2, c), par)],
        out_specs=pl.BlockSpec((1, seq, c), act),
        scratch_shapes=[scan_buf, scan_buf, scan_buf, scan_buf,
                        pltpu.VMEM((nch, c), F32), pltpu.VMEM((nch, c), F32)],
        compiler_params=_params("arbitrary", "arbitrary"),
        name="lru_mix",
    )(xrec, gate, conv_w, conv_b, w_gates, b_gates, lam)


def _seqdft_kernel(zr_ref, zi_ref, f1_ref, m2_ref, y_ref, ar_ref, ai_ref):
    r, pitch, unroll = DFT_RADIX, DFT_PITCH, DFT_UNROLL
    f1 = f1_ref[...]

    def stage1(it, _):
        i2 = it * unroll
        slabs = []
        for q in range(unroll):
            rows = pl.ds(i2 + q, r, stride=pitch)
            slabs.append(jnp.concatenate([zr_ref[0, rows, :], zi_ref[0, rows, :]], axis=0))
        rhs = jnp.concatenate(slabs, axis=1).astype(BF16)
        a = jnp.dot(f1, rhs, preferred_element_type=F32)
        for q in range(unroll):
            rows = pl.ds(i2 + q, r, stride=pitch)
            ar_ref[rows, :] = a[:r, q * LANES:(q + 1) * LANES]
            ai_ref[rows, :] = a[r:, q * LANES:(q + 1) * LANES]
        return 0

    lax.fori_loop(0, r // unroll, stage1, 0)

    def stage2(it, _):
        for q in range(unroll):
            k1 = it * unroll + q
            src = pl.ds(pl.multiple_of(k1 * pitch, SUBLANES), r)
            slab = jnp.concatenate([ar_ref[src, :], ai_ref[src, :]], axis=0).astype(BF16)
            o = jnp.dot(m2_ref[k1], slab, preferred_element_type=F32)
            y_ref[0, pl.ds(k1, r, stride=pitch), :] = o
        return 0

    lax.fori_loop(0, r // unroll, stage2, 0)
    for gap in range(r, pitch):
        y_ref[0, pl.ds(gap, r, stride=pitch), :] = jnp.zeros((r, LANES), F32)


def _seq_dft(zr, zi, f1, m2):
    nb, rows, _ = zr.shape
    cw = LANES
    act = lambda b, k: (b, 0, k)
    return pl.pallas_call(
        _seqdft_kernel,
        out_shape=jax.ShapeDtypeStruct((nb, rows, D_FFT), F32),
        grid=(nb, D_FFT // cw),
        in_specs=[pl.BlockSpec((1, rows, cw), act), pl.BlockSpec((1, rows, cw), act),
                  pl.BlockSpec(f1.shape, lambda b, k: (0, 0)),
                  pl.BlockSpec(m2.shape, lambda b, k: (0, 0, 0))],
        out_specs=pl.BlockSpec((1, rows, cw), act),
        scratch_shapes=[pltpu.VMEM((rows, cw), F32), pltpu.VMEM((rows, cw), F32)],
        compiler_params=_params("arbitrary", "arbitrary"),
        name="seq_dft",
    )(zr, zi, f1, m2)


def _dft_tables(seq):
    n1 = n2 = DFT_RADIX
    assert n1 * n2 == seq, "sequence DFT is factored as DFT_RADIX x DFT_RADIX"
    g = FFT_GROUP_DIM
    ang = 2.0 * np.pi * np.outer(np.arange(g), np.arange(g)) / g
    cg, sg = np.cos(ang) / math.sqrt(g), np.sin(ang) / math.sqrt(g)
    ngroups = D_FFT // g
    dft_c = np.zeros((D_FFT, 2 * D_FFT))
    for q in range(ngroups):
        dft_c[q * g:(q + 1) * g, q * g:(q + 1) * g] = cg
        dft_c[q * g:(q + 1) * g, D_FFT + q * g:D_FFT + (q + 1) * g] = -sg
    ang1 = 2.0 * np.pi * np.outer(np.arange(n1), np.arange(n1)) / n1
    c1, s1 = np.cos(ang1) / math.sqrt(n1), np.sin(ang1) / math.sqrt(n1)
    f1 = np.block([[c1, s1], [-s1, c1]])
    k1 = np.arange(n1)[:, None, None]
    k2 = np.arange(n2)[None, :, None]
    i2 = np.arange(n2)[None, None, :]
    ang2 = 2.0 * np.pi * (i2 * k2 / n2 + i2 * k1 / seq)
    m2 = np.concatenate([np.cos(ang2), np.sin(ang2)], axis=2) / math.sqrt(n2)
    return (jnp.asarray(dft_c, F32), jnp.asarray(f1, F32), jnp.asarray(m2, F32))


def _attn_route_kernel(xp_ref, xs_ref, ylru_ref, yfft_ref, kt_ref, v_ref, gl_ref, gf_ref, wout_ref,
                       gx_ref, wq_ref, wo_ref, gffn_ref, wr_ref, br_ref,
                       x2_ref, xs_out_ref, slot_ref, gate_ref, cnt_ref, h3tl_ref, slot_smem, ssem,
                       *, prompt_batches):
    tm = xp_ref.shape[1]
    x = jnp.where(pl.program_id(0) < prompt_batches, xp_ref[0], xs_ref[0])
    m_lru = _rms(ylru_ref[0].astype(F32)) * gl_ref[...]
    yfft = jnp.concatenate([yfft_ref[0, m * DFT_PITCH:m * DFT_PITCH + DFT_RADIX, :]
                            for m in range(tm // DFT_RADIX)], axis=0)
    m_fft = _rms(yfft) * gf_ref[...]
    mixed = jnp.concatenate([m_lru, m_fft], axis=-1).astype(BF16)
    x1 = x + jnp.dot(mixed, wout_ref[...], preferred_element_type=F32)

    hq = (_rms(x1) * gx_ref[...]).astype(BF16)
    q = jnp.dot(hq, wq_ref[...], preferred_element_type=F32) * (XATTN_HEAD_DIM ** -0.5)
    qb = q.astype(BF16)
    heads = []
    for h in range(N_XATTN_HEADS):
        sl = slice(h * XATTN_HEAD_DIM, (h + 1) * XATTN_HEAD_DIM)
        s = jnp.dot(qb[:, sl], kt_ref[0, sl, :], preferred_element_type=F32)
        p = jnp.exp(s - jnp.max(s, axis=-1, keepdims=True))
        inv = 1.0 / jnp.sum(p, axis=-1, keepdims=True)
        o = jnp.dot(p.astype(BF16), v_ref[0, :, sl], preferred_element_type=F32) * inv
        heads.append(o.astype(BF16))
    att = jnp.concatenate(heads, axis=-1)
    x2 = x1 + jnp.dot(att, wo_ref[...], preferred_element_type=F32)
    x2_ref[0] = x2

    h3 = _rms(x2) * gffn_ref[...]
    for s in range(ROW_TILES):
        h3tl_ref[pl.ds(s, tm, stride=ROW_TILES), :] = h3[:, s * LANES:(s + 1) * LANES]

    logits = jnp.dot(h3, wr_ref[...], preferred_element_type=F32,
                     precision=lax.Precision.HIGHEST) + br_ref[...]
    lg = logits.T[:N_EXPERTS, :]
    e_iota = lax.broadcasted_iota(I32, lg.shape, 0)
    vals, onehots = [], []
    for k in range(TOP_K):
        m = jnp.max(lg, axis=0, keepdims=True)
        idx = jnp.min(jnp.where(lg == m, e_iota, N_EXPERTS), axis=0, keepdims=True)
        oh = e_iota == idx
        vals.append(m)
        onehots.append(oh)
        lg = jnp.where(oh, -jnp.inf, lg)
    exps = [jnp.exp(v - vals[0]) for v in vals]
    inv = 1.0 / (exps[0] + exps[1] + exps[2] + exps[3])
    gate_ref[0] = jnp.concatenate([e * inv for e in exps], axis=0)

    member = jnp.zeros(lg.shape, F32)
    for oh in onehots:
        member = member + oh.astype(F32)
    mb = member.astype(BF16)
    r_i = lax.broadcasted_iota(I32, (tm, tm), 0)
    c_i = lax.broadcasted_iota(I32, (tm, tm), 1)
    earlier = jnp.where(r_i < c_i, 1.0, 0.0).astype(BF16)
    tok_rank = jnp.dot(mb, earlier, preferred_element_type=F32)
    er = lax.broadcasted_iota(I32, (N_EXPERTS, N_EXPERTS), 0)
    ec = lax.broadcasted_iota(I32, (N_EXPERTS, N_EXPERTS), 1)
    lower = jnp.where(ec < er, 1.0, 0.0).astype(BF16)
    exp_off = jnp.sum(jnp.dot(lower, mb, preferred_element_type=F32), axis=1, keepdims=True)
    pos = exp_off + tok_rank
    slots = [jnp.sum(jnp.where(oh, pos, 0.0), axis=0, keepdims=True) for oh in onehots]
    slot_ref[0] = jnp.concatenate(slots, axis=0).astype(I32)
    ones = jnp.ones((SUBLANES, tm), BF16)
    cnt = lax.dot_general(ones, mb, (((1,), (1,)), ((), ())), preferred_element_type=F32)
    cnt_ref[0] = cnt[0:1, :].astype(I32)

    cp = pltpu.make_async_copy(slot_ref.at[0], slot_smem, ssem)
    cp.start()
    cp.wait()

    def place(t, _):
        row = h3tl_ref[pl.ds(pl.multiple_of(t * ROW_TILES, ROW_TILES), ROW_TILES), :]
        for k in range(TOP_K):
            dst = pl.multiple_of(slot_smem[k, t] * ROW_TILES, ROW_TILES)
            xs_out_ref[pl.ds(dst, ROW_TILES), :] = row
        return 0

    lax.fori_loop(0, tm, place, 0, unroll=4)


def _attn_route(xp, xs, ylru, yfft, kt, v, gl, gf, w_out, gx, wq, wo, gffn, wr, br):
    nbp, seq, d = xp.shape
    nb = nbp + xs.shape[0]
    tm = TOKEN_TILE
    tmz = tm // DFT_RADIX * DFT_PITCH
    nt = seq // tm
    m = v.shape[1]
    tok = lambda b, i: (b, i, 0)
    full2 = lambda b, i: (0, 0)
    per_b = lambda b, i: (b, 0, 0)
    tiles = nb * nt
    tile_id = lambda b, i: (b * nt + i, 0, 0)
    nw = wr.shape[1]
    return pl.pallas_call(
        functools.partial(_attn_route_kernel, prompt_batches=nbp),
        out_shape=(jax.ShapeDtypeStruct((nb, seq, d), F32),
                   jax.ShapeDtypeStruct((tiles * tm * TOP_K * ROW_TILES, LANES), F32),
                   jax.ShapeDtypeStruct((tiles, TOP_K, tm), I32),
                   jax.ShapeDtypeStruct((tiles, TOP_K, tm), F32),
                   jax.ShapeDtypeStruct((tiles, 1, N_EXPERTS), I32)),
        grid=(nb, nt),
        in_specs=[pl.BlockSpec((1, tm, d), lambda b, i: (jnp.minimum(b, nbp - 1), i, 0)),
                  pl.BlockSpec((1, tm, d), lambda b, i: (jnp.maximum(b - nbp, 0), i, 0)),
                  pl.BlockSpec((1, tm, D_LRU), tok),
                  pl.BlockSpec((1, tmz, D_FFT), tok),
                  pl.BlockSpec((1, d, m), per_b), pl.BlockSpec((1, m, d), per_b),
                  pl.BlockSpec((1, D_LRU), full2), pl.BlockSpec((1, D_FFT), full2),
                  pl.BlockSpec((d, d), full2), pl.BlockSpec((1, d), full2),
                  pl.BlockSpec((d, d), full2), pl.BlockSpec((d, d), full2),
                  pl.BlockSpec((1, d), full2), pl.BlockSpec((d, nw), full2),
                  pl.BlockSpec((1, nw), full2)],
        out_specs=(pl.BlockSpec((1, tm, d), tok),
                   pl.BlockSpec((tm * TOP_K * ROW_TILES, LANES), lambda b, i: (b * nt + i, 0)),
                   pl.BlockSpec((1, TOP_K, tm), tile_id), pl.BlockSpec((1, TOP_K, tm), tile_id),
                   pl.BlockSpec((1, 1, N_EXPERTS), tile_id)),
        scratch_shapes=[pltpu.VMEM((tm * ROW_TILES, LANES), F32),
                        pltpu.SMEM((TOP_K, tm), I32), pltpu.SemaphoreType.DMA],
        compiler_params=_params("arbitrary", "arbitrary"),
        name="attn_route",
    )(xp, xs, ylru, yfft, kt, v, gl, gf, w_out, gx, wq, wo, gffn, wr, br)


def _for_each_run(blk, bexp_ref, ostart_ref, t0_ref, t1_ref, cum_ref, cnt_ref, off_ref, fn):
    e = bexp_ref[blk]
    o0 = ostart_ref[blk]

    def body(i, _):
        j = i * N_EXPERTS + e
        c = cum_ref[j]
        lo = jnp.maximum(o0, c)
        hi = jnp.minimum(o0 + EXPERT_TILE, c + cnt_ref[j])

        @pl.when(hi > lo)
        def _():
            fn(i * (TOKEN_TILE * TOP_K) + off_ref[j] + (lo - c), lo - o0, hi - lo)
        return 0

    lax.fori_loop(t0_ref[blk], t1_ref[blk] + 1, body, 0)


def _rows(ref, row, n_rows):
    return ref.at[pl.ds(pl.multiple_of(row * ROW_TILES, ROW_TILES), n_rows * ROW_TILES), :]


def _experts_kernel(bexp_ref, bvalid_ref, nblk_ref, ostart_ref, t0_ref, t1_ref, cum_ref, cnt_ref,
                    off_ref, xs_hbm, wg_ref, bg_ref, wu_ref, bu_ref, wd_ref, bd_ref, ys_hbm,
                    xbuf, ybuf, acc_ref, gsem, osem):
    b = pl.program_id(0)
    nblk = nblk_ref[0]
    slot = lax.rem(b, 2)
    tabs = (bexp_ref, ostart_ref, t0_ref, t1_ref, cum_ref, cnt_ref, off_ref)

    def gather(blk, sl):
        _for_each_run(blk, *tabs, lambda src, dst, n: pltpu.make_async_copy(
            _rows(xs_hbm, src, n), _rows(xbuf.at[sl], dst, n), gsem.at[sl]).start())

    def wait_in(blk, sl):
        n = bvalid_ref[blk]
        pltpu.make_async_copy(_rows(xs_hbm, 0, n), _rows(xbuf.at[sl], 0, n), gsem.at[sl]).wait()

    def wait_out(blk, sl):
        n = bvalid_ref[blk]
        pltpu.make_async_copy(_rows(ybuf.at[sl], 0, n), _rows(ys_hbm, 0, n), osem.at[sl]).wait()

    @pl.when(b == 0)
    def _():
        gather(0, 0)

    @pl.when(b + 1 < nblk)
    def _():
        gather(b + 1, 1 - slot)

    @pl.when(b < nblk)
    def _():
        wait_in(b, slot)

        @pl.when(b >= 2)
        def _():
            wait_out(b - 2, slot)

        tm = EXPERT_TILE
        x = jnp.concatenate([xbuf[slot, pl.ds(s, tm, stride=ROW_TILES), :]
                             for s in range(ROW_TILES)], axis=-1)
        row = lax.broadcasted_iota(I32, (tm, 1), 0)
        xb = jnp.where(row < bvalid_ref[b], x, 0.0).astype(BF16)
        for cidx in range(D_FF // FF_CHUNK):
            sl = slice(cidx * FF_CHUNK, (cidx + 1) * FF_CHUNK)
            gt = jnp.dot(xb, wg_ref[0, :, sl], preferred_element_type=F32) + bg_ref[0, :, sl]
            up = jnp.dot(xb, wu_ref[0, :, sl], preferred_element_type=F32) + bu_ref[0, :, sl]
            gt = jnp.minimum(gt, SWIGLU_LIMIT)
            up = jnp.clip(up, -SWIGLU_LIMIT, SWIGLU_LIMIT)
            act = (gt * _sigmoid(SWIGLU_ALPHA * gt) * (up + 1.0)).astype(BF16)
            part = jnp.dot(act, wd_ref[0, sl, :], preferred_element_type=F32)
            if cidx == 0:
                acc_ref[...] = part + bd_ref[0]
            else:
                acc_ref[...] += part
        for s in range(ROW_TILES):
            ybuf[slot, pl.ds(s, tm, stride=ROW_TILES), :] = acc_ref[:, s * LANES:(s + 1) * LANES]
        _for_each_run(b, *tabs, lambda dst, src, n: pltpu.make_async_copy(
            _rows(ybuf.at[slot], src, n), _rows(ys_hbm, dst, n), osem.at[slot]).start())

        @pl.when(b == nblk - 1)
        def _():
            wait_out(b, slot)

            @pl.when(b >= 1)
            def _():
                wait_out(b - 1, 1 - slot)


def _moe_experts(tables, xs, wg, bg, wu, bu, wd, bd):
    bexp = tables[0]
    nblocks = bexp.shape[0]
    tm = EXPERT_TILE
    wsel = lambda i, be, bv, nb, *_: (be[jnp.minimum(i, nb[0] - 1)], 0, 0)
    wspec = pl.BlockSpec((1, D_MODEL, D_FF), wsel)
    bspec = pl.BlockSpec((1, 1, D_FF), wsel)
    hbm = pl.BlockSpec(memory_space=pl.ANY)
    return pl.pallas_call(
        _experts_kernel,
        out_shape=jax.ShapeDtypeStruct(xs.shape, F32),
        grid_spec=pltpu.PrefetchScalarGridSpec(
            num_scalar_prefetch=len(tables),
            grid=(nblocks,),
            in_specs=[hbm, wspec, bspec, wspec, bspec, wspec, bspec],
            out_specs=hbm,
            scratch_shapes=[pltpu.VMEM((2, tm * ROW_TILES, LANES), F32),
                            pltpu.VMEM((2, tm * ROW_TILES, LANES), F32),
                            pltpu.VMEM((tm, D_MODEL), F32),
                            pltpu.SemaphoreType.DMA((2,)), pltpu.SemaphoreType.DMA((2,))]),
        compiler_params=_params("arbitrary"),
        name="moe_experts",
    )(*tables, xs, wg, bg, wu, bu, wd, bd)


def _combine_kernel(slot_hbm, gate_hbm, ys_ref, x2_ref, gfin_ref, outp_ref, outs_ref,
                    slot_smem, gate_smem, acc_ref, isem, *, prompt_tiles):
    i = pl.program_id(0)
    tm = x2_ref.shape[0]
    slot_cp = pltpu.make_async_copy(slot_hbm.at[i], slot_smem, isem.at[0])
    gate_cp = pltpu.make_async_copy(gate_hbm.at[i], gate_smem, isem.at[1])
    slot_cp.start()
    gate_cp.start()
    x2 = x2_ref[...]
    for s in range(ROW_TILES):
        acc_ref[pl.ds(s, tm, stride=ROW_TILES), :] = x2[:, s * LANES:(s + 1) * LANES]
    slot_cp.wait()
    gate_cp.wait()

    def body(t, _):
        rows = pl.ds(pl.multiple_of(t * ROW_TILES, ROW_TILES), ROW_TILES)
        acc = acc_ref[rows, :]
        for k in range(TOP_K):
            src = pl.multiple_of(slot_smem[k, t] * ROW_TILES, ROW_TILES)
            acc = acc + gate_smem[k, t] * ys_ref[pl.ds(src, ROW_TILES), :]
        acc_ref[rows, :] = acc
        return 0

    lax.fori_loop(0, tm, body, 0, unroll=4)
    acc = jnp.concatenate([acc_ref[pl.ds(s, tm, stride=ROW_TILES), :] for s in range(ROW_TILES)],
                          axis=-1)
    y = _rms(acc) * gfin_ref[...]

    @pl.when(i < prompt_tiles)
    def _():
        outp_ref[...] = y

    @pl.when(i >= prompt_tiles)
    def _():
        outs_ref[...] = y


def _moe_combine(slots, gates, ys, x2, gfin, t_prompt):
    t, d = x2.shape
    tm = TOKEN_TILE
    ptiles = t_prompt // tm
    return pl.pallas_call(
        functools.partial(_combine_kernel, prompt_tiles=ptiles),
        out_shape=(jax.ShapeDtypeStruct((t_prompt, d), F32),
                   jax.ShapeDtypeStruct((t - t_prompt, d), F32)),
        grid=(t // tm,),
        in_specs=[pl.BlockSpec(memory_space=pl.ANY), pl.BlockSpec(memory_space=pl.ANY),
                  pl.BlockSpec((tm * TOP_K * ROW_TILES, LANES), lambda i: (i, 0)),
                  pl.BlockSpec((tm, d), lambda i: (i, 0)),
                  pl.BlockSpec((1, d), lambda i: (0, 0))],
        out_specs=(pl.BlockSpec((tm, d), lambda i: (jnp.minimum(i, ptiles - 1), 0)),
                   pl.BlockSpec((tm, d), lambda i: (jnp.maximum(i - ptiles, 0), 0))),
        scratch_shapes=[pltpu.SMEM((TOP_K, tm), I32), pltpu.SMEM((TOP_K, tm), F32),
                        pltpu.VMEM((tm * ROW_TILES, LANES), F32),
                        pltpu.SemaphoreType.DMA((2,))],
        compiler_params=_params("arbitrary"),
        name="moe_combine",
    )(slots, gates, ys, x2, gfin)


def _lru_gate_weights(w_a, b_a, w_x, b_x):
    hpb = LRU_BLOCK // LRU_HEAD_DIM
    ncb = D_LRU // LRU_BLOCK

    def blockdiag(w):
        w = w.reshape(ncb, hpb, LRU_HEAD_DIM, LRU_HEAD_DIM)
        eye = jnp.eye(hpb, dtype=w.dtype)
        full = jnp.einsum('nhij,hg->nhigj', w, eye)
        return full.reshape(ncb, LRU_BLOCK, LRU_BLOCK)

    mats = [blockdiag(w_a[0]), blockdiag(w_a[1]), blockdiag(w_x[0]), blockdiag(w_x[1])]
    w = jnp.concatenate(mats, axis=-1).astype(BF16)
    bs = [b.reshape(ncb, 1, LRU_BLOCK) for b in (b_a[0], b_a[1], b_x[0], b_x[1])]
    return w, jnp.concatenate(bs, axis=-1).astype(F32)


def _encoder_layer(xp, xs, mem, norm_mix, w_in, conv_w, conv_b, lru_w_a, lru_b_a, lru_w_x, lru_b_x,
                   lru_lambda, norm_lru_out, norm_fft_out, w_out, norm_xattn, norm_mem, w_q, w_kv,
                   w_o, norm_ffn, w_router, b_router, w_gate, b_gate, w_up, b_up, w_down, b_down,
                   norm_final):
    nbp, seq, d = xp.shape
    nb = nbp + xs.shape[0]
    t = nb * seq
    seqz = seq // DFT_RADIX * DFT_PITCH
    ncb = D_LRU // LRU_BLOCK
    row = lambda v: v.reshape(1, -1).astype(F32)

    dft_c, f1, m2 = _dft_tables(seq)
    kt, v = _kv_proj(mem, row(norm_mem), w_kv[:, :d].T.astype(BF16), w_kv[:, d:].astype(BF16))

    xrec, gate, zr, zi = _in_proj(xp.reshape(nbp * seq, d), xs.reshape(t - nbp * seq, d),
                                  row(norm_mix), w_in.astype(BF16), dft_c.astype(BF16))

    w_gates, b_gates = _lru_gate_weights(lru_w_a, lru_b_a, lru_w_x, lru_b_x)
    cw = conv_w.reshape(CONV_WIDTH, ncb, LRU_BLOCK).transpose(1, 0, 2)
    cb = conv_b.reshape(ncb, 1, LRU_BLOCK)
    lam = lru_lambda.reshape(2, ncb, LRU_BLOCK).transpose(1, 0, 2)
    ylru = _lru_mix(xrec.reshape(nb, seq, D_LRU), gate.reshape(nb, seq, D_LRU), cw, cb,
                    w_gates, b_gates, lam)

    yfft = _seq_dft(zr.reshape(nb, seqz, D_FFT), zi.reshape(nb, seqz, D_FFT),
                    f1.astype(BF16), m2.astype(BF16))

    w_r = jnp.pad(w_router.astype(F32), ((0, 0), (0, LANES - N_EXPERTS)))
    b_r = jnp.pad(b_router.astype(F32).reshape(1, -1), ((0, 0), (0, LANES - N_EXPERTS)))
    x2, xs_rows, slots, gates, tile_cnt = _attn_route(
        xp, xs, ylru, yfft, kt, v, row(norm_lru_out), row(norm_fft_out), w_out.astype(BF16),
        row(norm_xattn), w_q.astype(BF16), w_o.astype(BF16), row(norm_ffn), w_r, b_r)

    tmb = EXPERT_TILE
    nblocks = -(-(t * TOP_K) // tmb) + N_EXPERTS
    tile_cnt = tile_cnt.reshape(-1, N_EXPERTS)
    cum_end = jnp.cumsum(tile_cnt, axis=0)
    cum = cum_end - tile_cnt
    off = jnp.cumsum(tile_cnt, axis=1) - tile_cnt
    counts = cum_end[-1]
    padded = ((counts + tmb - 1) // tmb) * tmb
    pend = jnp.cumsum(padded)
    pstart = pend - padded
    blk_start = jnp.arange(nblocks, dtype=I32) * tmb
    bexp = jnp.minimum(jnp.sum((pend[None, :] <= blk_start[:, None]).astype(I32), axis=1),
                       N_EXPERTS - 1)
    ostart = blk_start - pstart[bexp]
    bvalid = jnp.clip(counts[bexp] - ostart, 0, tmb).astype(I32)
    nblk = (pend[-1:] // tmb).astype(I32)
    cum_b = jnp.take(cum, bexp, axis=1)
    end_b = jnp.take(cum_end, bexp, axis=1)
    ntiles = tile_cnt.shape[0]
    t0 = jnp.minimum(jnp.sum((end_b <= ostart[None, :]).astype(I32), axis=0), ntiles - 1)
    t1 = jnp.maximum(jnp.sum((cum_b < (ostart + tmb)[None, :]).astype(I32), axis=0) - 1, 0)
    tables = tuple(a.astype(I32) for a in (bexp, bvalid, nblk, ostart, t0, t1, cum.reshape(-1),
                                           tile_cnt.reshape(-1), off.reshape(-1)))

    ys = _moe_experts(tables, xs_rows,
                      w_gate.astype(BF16), b_gate.reshape(N_EXPERTS, 1, D_FF),
                      w_up.astype(BF16), b_up.reshape(N_EXPERTS, 1, D_FF),
                      w_down.astype(BF16), b_down.reshape(N_EXPERTS, 1, D_MODEL))
    outp, outs = _moe_combine(slots, gates, ys, x2.reshape(t, d), row(norm_final), nbp * seq)
    return outp.reshape(nbp, seq, d), outs.reshape(nb - nbp, seq, d)


def kernel(x_prompt, x_sample, mem_prompt, mem_sample, norm_mix, w_in, conv_w, conv_b, lru_w_a,
           lru_b_a, lru_w_x, lru_b_x, lru_lambda, norm_lru_out, norm_fft_out, w_out, norm_xattn,
           norm_mem, w_q, w_kv, w_o, norm_ffn, w_router, b_router, w_gate, b_gate, w_up, b_up,
           w_down, b_down, norm_final):
    assert x_prompt.shape[1:] == x_sample.shape[1:], "both groups must share (SEQ, D_MODEL)"
    assert w_in.shape[0] == 1, "single-layer block"
    mem = jnp.concatenate([mem_prompt, mem_sample], axis=0)
    return _encoder_layer(x_prompt, x_sample, mem, norm_mix[0], w_in[0], conv_w[0], conv_b[0],
                          lru_w_a[0], lru_b_a[0], lru_w_x[0], lru_b_x[0], lru_lambda[0],
                          norm_lru_out[0], norm_fft_out[0], w_out[0], norm_xattn[0], norm_mem[0],
                          w_q[0], w_kv[0], w_o[0], norm_ffn[0], w_router[0], b_router[0],
                          w_gate[0], b_gate[0], w_up[0], b_up[0], w_down[0], b_down[0], norm_final)
```

```python
import functools
import math

import numpy as np
import jax
import jax.numpy as jnp
from jax import lax
from jax.experimental import pallas as pl
from jax.experimental.pallas import tpu as pltpu

F32, BF16, I32 = jnp.float32, jnp.bfloat16, jnp.int32

D_MODEL = 1024
D_LRU = 512
LRU_HEAD_DIM = 64
CONV_WIDTH = 4
LRU_C = 8.0
D_FFT = 512
FFT_GROUP_DIM = 128
N_XATTN_HEADS = 4
XATTN_HEAD_DIM = 256
N_EXPERTS = 32
TOP_K = 4
D_FF = 1024
SWIGLU_LIMIT = 7.0
SWIGLU_ALPHA = 1.702
EPS = 1e-6

LANES = 128
SUBLANES = 8
ROW_TILES = D_MODEL // LANES
TOKEN_TILE = 512
EXPERT_TILE = 512
FF_CHUNK = 256
LRU_BLOCK = 128
SCAN_CHUNK = 128
SCAN_PITCH = SCAN_CHUNK + SUBLANES
DFT_RADIX = 64
DFT_PITCH = DFT_RADIX + SUBLANES
DFT_UNROLL = 4
VMEM_LIMIT = 56 * 1024 * 1024


def _params(*sem):
    return pltpu.CompilerParams(dimension_semantics=sem, vmem_limit_bytes=VMEM_LIMIT)


def _rms(xf):
    return xf * lax.rsqrt(jnp.mean(xf * xf, axis=-1, keepdims=True) + EPS)


def _sigmoid(x):
    return 1.0 / (1.0 + jnp.exp(-x))


def _gelu_tanh(x):
    return 0.5 * x * (1.0 + jnp.tanh(math.sqrt(2.0 / math.pi) * (x + 0.044715 * (x * x * x))))


def _kv_kernel(mem_ref, g_ref, wkt_ref, wv_ref, kt_ref, v_ref):
    mn = (_rms(mem_ref[0]) * g_ref[...]).astype(BF16)
    kt = lax.dot_general(wkt_ref[...], mn, (((1,), (1,)), ((), ())), preferred_element_type=F32)
    kt_ref[0] = kt.astype(BF16)
    v_ref[0] = jnp.dot(mn, wv_ref[...], preferred_element_type=F32).astype(BF16)


def _kv_proj(mem, g, wkt, wv):
    nb, m, d = mem.shape
    return pl.pallas_call(
        _kv_kernel,
        out_shape=(jax.ShapeDtypeStruct((nb, d, m), BF16), jax.ShapeDtypeStruct((nb, m, d), BF16)),
        grid=(nb,),
        in_specs=[pl.BlockSpec((1, m, d), lambda b: (b, 0, 0)),
                  pl.BlockSpec((1, d), lambda b: (0, 0)),
                  pl.BlockSpec((d, d), lambda b: (0, 0)),
                  pl.BlockSpec((d, d), lambda b: (0, 0))],
        out_specs=(pl.BlockSpec((1, d, m), lambda b: (b, 0, 0)),
                   pl.BlockSpec((1, m, d), lambda b: (b, 0, 0))),
        compiler_params=_params("arbitrary"),
        name="kv_proj",
    )(mem, g, wkt, wv)


def _inproj_kernel(xp_ref, xs_ref, g_ref, w_ref, dft_ref, xrec_ref, gate_ref, zr_ref, zi_ref,
                   *, prompt_tiles):
    x = jnp.where(pl.program_id(0) < prompt_tiles, xp_ref[...], xs_ref[...])
    h = (_rms(x) * g_ref[...]).astype(BF16)
    proj = jnp.dot(h, w_ref[...], preferred_element_type=F32)
    xrec_ref[...] = proj[:, :D_LRU]
    gate_ref[...] = proj[:, D_LRU:2 * D_LRU].astype(BF16)
    z = jnp.dot(proj[:, 2 * D_LRU:].astype(BF16), dft_ref[...], preferred_element_type=F32)
    pad = jnp.zeros((DFT_PITCH - DFT_RADIX, D_FFT), F32)
    for m in range(x.shape[0] // DFT_RADIX):
        rows = slice(m * DFT_RADIX, (m + 1) * DFT_RADIX)
        dst = slice(m * DFT_PITCH, m * DFT_PITCH + DFT_RADIX)
        gap = slice(m * DFT_PITCH + DFT_RADIX, (m + 1) * DFT_PITCH)
        zr_ref[dst, :] = z[rows, :D_FFT]
        zi_ref[dst, :] = z[rows, D_FFT:]
        zr_ref[gap, :] = pad
        zi_ref[gap, :] = pad


def _in_proj(xp2d, xs2d, g, w_in, dft_c):
    tm = TOKEN_TILE
    ptiles = xp2d.shape[0] // tm
    t = xp2d.shape[0] + xs2d.shape[0]
    tz = t // DFT_RADIX * DFT_PITCH
    tmz = tm // DFT_RADIX * DFT_PITCH
    row = lambda i: (i, 0)
    full = lambda i: (0, 0)
    return pl.pallas_call(
        functools.partial(_inproj_kernel, prompt_tiles=ptiles),
        out_shape=(jax.ShapeDtypeStruct((t, D_LRU), F32), jax.ShapeDtypeStruct((t, D_LRU), BF16),
                   jax.ShapeDtypeStruct((tz, D_FFT), F32), jax.ShapeDtypeStruct((tz, D_FFT), F32)),
        grid=(t // tm,),
        in_specs=[pl.BlockSpec((tm, D_MODEL), lambda i: (jnp.minimum(i, ptiles - 1), 0)),
                  pl.BlockSpec((tm, D_MODEL), lambda i: (jnp.maximum(i - ptiles, 0), 0)),
                  pl.BlockSpec((1, D_MODEL), full),
                  pl.BlockSpec(w_in.shape, full), pl.BlockSpec(dft_c.shape, full)],
        out_specs=(pl.BlockSpec((tm, D_LRU), row), pl.BlockSpec((tm, D_LRU), row),
                   pl.BlockSpec((tmz, D_FFT), row), pl.BlockSpec((tmz, D_FFT), row)),
        compiler_params=_params("arbitrary"),
        name="in_proj",
    )(xp2d, xs2d, g, w_in, dft_c)


def _lru_kernel(x_ref, gate_ref, cw_ref, cb_ref, w_ref, b_ref, lam_ref, y_ref,
                af_ref, uf_ref, ab_ref, ub_ref, cf_ref, cbk_ref, *, seq):
    nch = seq // SCAN_CHUNK
    cw = cw_ref[0]
    cb = cb_ref[0]
    lam = lam_ref[0]
    sp = jnp.maximum(-lam, 0.0) + jnp.log1p(jnp.exp(-jnp.abs(lam)))
    bias = b_ref[0]
    c = LRU_BLOCK

    def gates_body(j, _):
        r0 = pl.multiple_of(j * SCAN_CHUNK, SCAN_CHUNK)
        main = x_ref[0, pl.ds(r0, SCAN_CHUNK), :]
        prev = x_ref[0, pl.ds(jnp.maximum(r0 - SUBLANES, 0), SUBLANES), :]
        nxt = x_ref[0, pl.ds(jnp.minimum(r0 + SCAN_CHUNK, seq - SUBLANES), SUBLANES), :]
        prev = jnp.where(j > 0, prev, 0.0)
        nxt = jnp.where(j < nch - 1, nxt, 0.0)
        win = jnp.concatenate([prev, main, nxt], axis=0)
        base = SUBLANES - CONV_WIDTH // 2
        xc = cb
        for tap in range(CONV_WIDTH):
            xc = xc + win[base + tap:base + tap + SCAN_CHUNK, :] * cw[tap:tap + 1, :]
        xcb = xc.astype(BF16)
        o0 = pl.multiple_of(j * SCAN_PITCH, SUBLANES)
        for d, (a_ref, u_ref) in enumerate(((af_ref, uf_ref), (ab_ref, ub_ref))):
            ga = jnp.dot(xcb, w_ref[0, :, d * c:(d + 1) * c], preferred_element_type=F32) \
                + bias[:, d * c:(d + 1) * c]
            gx = jnp.dot(xcb, w_ref[0, :, (2 + d) * c:(3 + d) * c], preferred_element_type=F32) \
                + bias[:, (2 + d) * c:(3 + d) * c]
            log_a = (-LRU_C) * _sigmoid(ga) * sp[d:d + 1, :]
            a = jnp.exp(log_a)
            u = jnp.sqrt(-jnp.tanh(log_a) * (1.0 + a * a)) * (_sigmoid(gx) * xc)
            a_ref[pl.ds(o0, SCAN_CHUNK), :] = a
            u_ref[pl.ds(o0, SCAN_CHUNK), :] = u
        return 0

    lax.fori_loop(0, nch, gates_body, 0)

    def scan_body(t, carry):
        hf, pf, hb, pb = carry
        fwd = pl.ds(t, nch, stride=SCAN_PITCH)
        bwd = pl.ds(SCAN_CHUNK - 1 - t, nch, stride=SCAN_PITCH)
        a = af_ref[fwd, :]
        hf = a * hf + uf_ref[fwd, :]
        pf = a * pf
        uf_ref[fwd, :] = hf
        af_ref[fwd, :] = pf
        a = ab_ref[bwd, :]
        hb = a * hb + ub_ref[bwd, :]
        pb = a * pb
        ub_ref[bwd, :] = hb
        ab_ref[bwd, :] = pb
        return hf, pf, hb, pb

    zeros = jnp.zeros((nch, c), F32)
    ones = jnp.ones((nch, c), F32)
    hf, pf, hb, pb = lax.fori_loop(0, SCAN_CHUNK, scan_body, (zeros, ones, zeros, ones))

    carry = jnp.zeros((1, c), F32)
    cf_ref[0:1, :] = carry
    for j in range(1, nch):
        carry = hf[j - 1:j, :] + pf[j - 1:j, :] * carry
        cf_ref[j:j + 1, :] = carry
    carry = jnp.zeros((1, c), F32)
    cbk_ref[nch - 1:nch, :] = carry
    for j in range(nch - 2, -1, -1):
        carry = hb[j + 1:j + 2, :] + pb[j + 1:j + 2, :] * carry
        cbk_ref[j:j + 1, :] = carry

    def out_body(j, _):
        r0 = pl.multiple_of(j * SCAN_CHUNK, SCAN_CHUNK)
        o0 = pl.multiple_of(j * SCAN_PITCH, SUBLANES)
        rows = pl.ds(o0, SCAN_CHUNK)
        h = (uf_ref[rows, :] + af_ref[rows, :] * cf_ref[pl.ds(j, 1), :]
             + ub_ref[rows, :] + ab_ref[rows, :] * cbk_ref[pl.ds(j, 1), :])
        g = gate_ref[0, pl.ds(r0, SCAN_CHUNK), :].astype(F32)
        y_ref[0, pl.ds(r0, SCAN_CHUNK), :] = (h * _gelu_tanh(g)).astype(BF16)
        return 0

    lax.fori_loop(0, nch, out_body, 0)


def _lru_mix(xrec, gate, conv_w, conv_b, w_gates, b_gates, lam):
    nb, seq, _ = xrec.shape
    ncb = D_LRU // LRU_BLOCK
    c = LRU_BLOCK
    nch = seq // SCAN_CHUNK
    act = lambda b, k: (b, 0, k)
    par = lambda b, k: (k, 0, 0)
    scan_buf = pltpu.VMEM((nch * SCAN_PITCH, c), F32)
    return pl.pallas_call(
        functools.partial(_lru_kernel, seq=seq),
        out_shape=jax.ShapeDtypeStruct((nb, seq, D_LRU), BF16),
        grid=(nb, ncb),
        in_specs=[pl.BlockSpec((1, seq, c), act), pl.BlockSpec((1, seq, c), act),
                  pl.BlockSpec((1, CONV_WIDTH, c), par), pl.BlockSpec((1, 1, c), par),
                  pl.BlockSpec((1, c, 4 * c), par), pl.BlockSpec((1, 1, 4 * c), par),
                  pl.BlockSpec((1, 2, c), par)],
        out_specs=pl.BlockSpec((1, seq, c), act),
        scratch_shapes=[scan_buf, scan_buf, scan_buf, scan_buf,
                        pltpu.VMEM((nch, c), F32), pltpu.VMEM((nch, c), F32)],
        compiler_params=_params("arbitrary", "arbitrary"),
        name="lru_mix",
    )(xrec, gate, conv_w, conv_b, w_gates, b_gates, lam)


def _seqdft_kernel(zr_ref, zi_ref, f1_ref, m2_ref, y_ref, ar_ref, ai_ref):
    r, pitch, unroll = DFT_RADIX, DFT_PITCH, DFT_UNROLL
    f1 = f1_ref[...]

    def stage1(it, _):
        i2 = it * unroll
        slabs = []
        for q in range(unroll):
            rows = pl.ds(i2 + q, r, stride=pitch)
            slabs.append(jnp.concatenate([zr_ref[0, rows, :], zi_ref[0, rows, :]], axis=0))
        rhs = jnp.concatenate(slabs, axis=1).astype(BF16)
        a = jnp.dot(f1, rhs, preferred_element_type=F32)
        for q in range(unroll):
            rows = pl.ds(i2 + q, r, stride=pitch)
            ar_ref[rows, :] = a[:r, q * LANES:(q + 1) * LANES]
            ai_ref[rows, :] = a[r:, q * LANES:(q + 1) * LANES]
        return 0

    lax.fori_loop(0, r // unroll, stage1, 0)

    def stage2(it, _):
        for q in range(unroll):
            k1 = it * unroll + q
            src = pl.ds(pl.multiple_of(k1 * pitch, SUBLANES), r)
            slab = jnp.concatenate([ar_ref[src, :], ai_ref[src, :]], axis=0).astype(BF16)
            o = jnp.dot(m2_ref[k1], slab, preferred_element_type=F32)
            y_ref[0, pl.ds(k1, r, stride=pitch), :] = o
        return 0

    lax.fori_loop(0, r // unroll, stage2, 0)
    for gap in range(r, pitch):
        y_ref[0, pl.ds(gap, r, stride=pitch), :] = jnp.zeros((r, LANES), F32)


def _seq_dft(zr, zi, f1, m2):
    nb, rows, _ = zr.shape
    cw = LANES
    act = lambda b, k: (b, 0, k)
    return pl.pallas_call(
        _seqdft_kernel,
        out_shape=jax.ShapeDtypeStruct((nb, rows, D_FFT), F32),
        grid=(nb, D_FFT // cw),
        in_specs=[pl.BlockSpec((1, rows, cw), act), pl.BlockSpec((1, rows, cw), act),
                  pl.BlockSpec(f1.shape, lambda b, k: (0, 0)),
                  pl.BlockSpec(m2.shape, lambda b, k: (0, 0, 0))],
        out_specs=pl.BlockSpec((1, rows, cw), act),
        scratch_shapes=[pltpu.VMEM((rows, cw), F32), pltpu.VMEM((rows, cw), F32)],
        compiler_params=_params("arbitrary", "arbitrary"),
        name="seq_dft",
    )(zr, zi, f1, m2)


def _dft_tables(seq):
    n1 = n2 = DFT_RADIX
    assert n1 * n2 == seq, "sequence DFT is factored as DFT_RADIX x DFT_RADIX"
    g = FFT_GROUP_DIM
    ang = 2.0 * np.pi * np.outer(np.arange(g), np.arange(g)) / g
    cg, sg = np.cos(ang) / math.sqrt(g), np.sin(ang) / math.sqrt(g)
    ngroups = D_FFT // g
    dft_c = np.zeros((D_FFT, 2 * D_FFT))
    for q in range(ngroups):
        dft_c[q * g:(q + 1) * g, q * g:(q + 1) * g] = cg
        dft_c[q * g:(q + 1) * g, D_FFT + q * g:D_FFT + (q + 1) * g] = -sg
    ang1 = 2.0 * np.pi * np.outer(np.arange(n1), np.arange(n1)) / n1
    c1, s1 = np.cos(ang1) / math.sqrt(n1), np.sin(ang1) / math.sqrt(n1)
    f1 = np.block([[c1, s1], [-s1, c1]])
    k1 = np.arange(n1)[:, None, None]
    k2 = np.arange(n2)[None, :, None]
    i2 = np.arange(n2)[None, None, :]
    ang2 = 2.0 * np.pi * (i2 * k2 / n2 + i2 * k1 / seq)
    m2 = np.concatenate([np.cos(ang2), np.sin(ang2)], axis=2) / math.sqrt(n2)
    return (jnp.asarray(dft_c, F32), jnp.asarray(f1, F32), jnp.asarray(m2, F32))


def _attn_route_kernel(xp_ref, xs_ref, ylru_ref, yfft_ref, kt_ref, v_ref, gl_ref, gf_ref, wout_ref,
                       gx_ref, wq_ref, wo_ref, gffn_ref, wrh_ref, wrl_ref, br_ref,
                       x2_ref, xs_out_ref, slot_ref, gate_ref, cnt_ref, h3tl_ref, slot_smem, ssem,
                       *, prompt_batches):
    tm = xp_ref.shape[1]
    x = jnp.where(pl.program_id(0) < prompt_batches, xp_ref[0], xs_ref[0])
    m_lru = _rms(ylru_ref[0].astype(F32)) * gl_ref[...]
    yfft = jnp.concatenate([yfft_ref[0, m * DFT_PITCH:m * DFT_PITCH + DFT_RADIX, :]
                            for m in range(tm // DFT_RADIX)], axis=0)
    m_fft = _rms(yfft) * gf_ref[...]
    mixed = jnp.concatenate([m_lru, m_fft], axis=-1).astype(BF16)
    x1 = x + jnp.dot(mixed, wout_ref[...], preferred_element_type=F32)

    hq = (_rms(x1) * gx_ref[...]).astype(BF16)
    q = jnp.dot(hq, wq_ref[...], preferred_element_type=F32) * (XATTN_HEAD_DIM ** -0.5)
    qb = q.astype(BF16)
    heads = []
    for h in range(N_XATTN_HEADS):
        sl = slice(h * XATTN_HEAD_DIM, (h + 1) * XATTN_HEAD_DIM)
        s = jnp.dot(qb[:, sl], kt_ref[0, sl, :], preferred_element_type=F32)
        p = jnp.exp(s - jnp.max(s, axis=-1, keepdims=True))
        inv = 1.0 / jnp.sum(p, axis=-1, keepdims=True)
        o = jnp.dot(p.astype(BF16), v_ref[0, :, sl], preferred_element_type=F32) * inv
        heads.append(o.astype(BF16))
    att = jnp.concatenate(heads, axis=-1)
    x2 = x1 + jnp.dot(att, wo_ref[...], preferred_element_type=F32)
    x2_ref[0] = x2

    h3 = _rms(x2) * gffn_ref[...]
    for s in range(ROW_TILES):
        h3tl_ref[pl.ds(s, tm, stride=ROW_TILES), :] = h3[:, s * LANES:(s + 1) * LANES]

    h_hi = h3.astype(BF16)
    h_lo = (h3 - h_hi.astype(F32)).astype(BF16)
    logits = (jnp.dot(h_hi, wrh_ref[...], preferred_element_type=F32)
              + jnp.dot(h_lo, wrh_ref[...], preferred_element_type=F32)
              + jnp.dot(h_hi, wrl_ref[...], preferred_element_type=F32)) + br_ref[...]
    lg = logits.T[:N_EXPERTS, :]
    e_iota = lax.broadcasted_iota(I32, lg.shape, 0)
    vals, onehots = [], []
    for k in range(TOP_K):
        m = jnp.max(lg, axis=0, keepdims=True)
        idx = jnp.min(jnp.where(lg == m, e_iota, N_EXPERTS), axis=0, keepdims=True)
        oh = e_iota == idx
        vals.append(m)
        onehots.append(oh)
        lg = jnp.where(oh, -jnp.inf, lg)
    exps = [jnp.exp(v - vals[0]) for v in vals]
    inv = 1.0 / (exps[0] + exps[1] + exps[2] + exps[3])

    def lane_tiles(rows):
        return jnp.stack([jnp.concatenate([r[:, c * LANES:(c + 1) * LANES]
                                           for c in range(tm // LANES)], axis=0) for r in rows])

    gate_ref[0] = lane_tiles([e * inv for e in exps])

    member = jnp.zeros(lg.shape, F32)
    for oh in onehots:
        member = member + oh.astype(F32)
    mb = member.astype(BF16)
    r_i = lax.broadcasted_iota(I32, (tm, tm), 0)
    c_i = lax.broadcasted_iota(I32, (tm, tm), 1)
    earlier = jnp.where(r_i < c_i, 1.0, 0.0).astype(BF16)
    tok_rank = jnp.dot(mb, earlier, preferred_element_type=F32)
    er = lax.broadcasted_iota(I32, (N_EXPERTS, N_EXPERTS), 0)
    ec = lax.broadcasted_iota(I32, (N_EXPERTS, N_EXPERTS), 1)
    lower = jnp.where(ec < er, 1.0, 0.0).astype(BF16)
    exp_off = jnp.sum(jnp.dot(lower, mb, preferred_element_type=F32), axis=1, keepdims=True)
    pos = exp_off + tok_rank
    slots = [jnp.sum(jnp.where(oh, pos, 0.0), axis=0, keepdims=True) for oh in onehots]
    slot_ref[0] = lane_tiles(slots).astype(I32)
    ones = jnp.ones((SUBLANES, tm), BF16)
    cnt = lax.dot_general(ones, mb, (((1,), (1,)), ((), ())), preferred_element_type=F32)
    cnt_ref[0] = cnt[0:1, :].astype(I32)

    cp = pltpu.make_async_copy(slot_ref.at[0], slot_smem, ssem)
    cp.start()
    cp.wait()

    for c in range(tm // LANES):
        def place(j, _, c=c):
            t = c * LANES + j
            row = h3tl_ref[pl.ds(pl.multiple_of(t * ROW_TILES, ROW_TILES), ROW_TILES), :]
            for k in range(TOP_K):
                dst = pl.multiple_of(slot_smem[k, c, j] * ROW_TILES, ROW_TILES)
                xs_out_ref[pl.ds(dst, ROW_TILES), :] = row
            return 0

        lax.fori_loop(0, LANES, place, 0, unroll=8)


def _attn_route(xp, xs, ylru, yfft, kt, v, gl, gf, w_out, gx, wq, wo, gffn, wr_hi, wr_lo, br):
    nbp, seq, d = xp.shape
    nb = nbp + xs.shape[0]
    tm = TOKEN_TILE
    tmz = tm // DFT_RADIX * DFT_PITCH
    nt = seq // tm
    m = v.shape[1]
    tok = lambda b, i: (b, i, 0)
    full2 = lambda b, i: (0, 0)
    per_b = lambda b, i: (b, 0, 0)
    tiles = nb * nt
    tile_id = lambda b, i: (b * nt + i, 0, 0)
    tile_id4 = lambda b, i: (b * nt + i, 0, 0, 0)
    nw = wr_hi.shape[1]
    per_tok = (TOP_K, tm // LANES, LANES)
    return pl.pallas_call(
        functools.partial(_attn_route_kernel, prompt_batches=nbp),
        out_shape=(jax.ShapeDtypeStruct((nb, seq, d), F32),
                   jax.ShapeDtypeStruct((tiles * tm * TOP_K * ROW_TILES, LANES), F32),
                   jax.ShapeDtypeStruct((tiles,) + per_tok, I32),
                   jax.ShapeDtypeStruct((tiles,) + per_tok, F32),
                   jax.ShapeDtypeStruct((tiles, 1, N_EXPERTS), I32)),
        grid=(nb, nt),
        in_specs=[pl.BlockSpec((1, tm, d), lambda b, i: (jnp.minimum(b, nbp - 1), i, 0)),
                  pl.BlockSpec((1, tm, d), lambda b, i: (jnp.maximum(b - nbp, 0), i, 0)),
                  pl.BlockSpec((1, tm, D_LRU), tok),
                  pl.BlockSpec((1, tmz, D_FFT), tok),
                  pl.BlockSpec((1, d, m), per_b), pl.BlockSpec((1, m, d), per_b),
                  pl.BlockSpec((1, D_LRU), full2), pl.BlockSpec((1, D_FFT), full2),
                  pl.BlockSpec((d, d), full2), pl.BlockSpec((1, d), full2),
                  pl.BlockSpec((d, d), full2), pl.BlockSpec((d, d), full2),
                  pl.BlockSpec((1, d), full2), pl.BlockSpec((d, nw), full2),
                  pl.BlockSpec((d, nw), full2), pl.BlockSpec((1, nw), full2)],
        out_specs=(pl.BlockSpec((1, tm, d), tok),
                   pl.BlockSpec((tm * TOP_K * ROW_TILES, LANES), lambda b, i: (b * nt + i, 0)),
                   pl.BlockSpec((1,) + per_tok, tile_id4), pl.BlockSpec((1,) + per_tok, tile_id4),
                   pl.BlockSpec((1, 1, N_EXPERTS), tile_id)),
        scratch_shapes=[pltpu.VMEM((tm * ROW_TILES, LANES), F32),
                        pltpu.SMEM(per_tok, I32), pltpu.SemaphoreType.DMA],
        compiler_params=_params("arbitrary", "arbitrary"),
        name="attn_route",
    )(xp, xs, ylru, yfft, kt, v, gl, gf, w_out, gx, wq, wo, gffn, wr_hi, wr_lo, br)


def _for_each_run(blk, bexp_ref, ostart_ref, t0_ref, t1_ref, cum_ref, cnt_ref, off_ref, fn):
    e = bexp_ref[blk]
    o0 = ostart_ref[blk]

    def body(i, _):
        j = i * N_EXPERTS + e
        c = cum_ref[j]
        lo = jnp.maximum(o0, c)
        hi = jnp.minimum(o0 + EXPERT_TILE, c + cnt_ref[j])

        @pl.when(hi > lo)
        def _():
            fn(i * (TOKEN_TILE * TOP_K) + off_ref[j] + (lo - c), lo - o0, hi - lo)
        return 0

    lax.fori_loop(t0_ref[blk], t1_ref[blk] + 1, body, 0)


def _rows(ref, row, n_rows):
    return ref.at[pl.ds(pl.multiple_of(row * ROW_TILES, ROW_TILES), n_rows * ROW_TILES), :]


def _experts_kernel(bexp_ref, bvalid_ref, nblk_ref, ostart_ref, t0_ref, t1_ref, cum_ref, cnt_ref,
                    off_ref, xs_hbm, wg_ref, bg_ref, wu_ref, bu_ref, wd_ref, bd_ref, ys_hbm,
                    xbuf, ybuf, act_ref, gsem, osem):
    b = pl.program_id(0)
    nblk = nblk_ref[0]
    slot = lax.rem(b, 2)
    tabs = (bexp_ref, ostart_ref, t0_ref, t1_ref, cum_ref, cnt_ref, off_ref)

    def gather(blk, sl):
        _for_each_run(blk, *tabs, lambda src, dst, n: pltpu.make_async_copy(
            _rows(xs_hbm, src, n), _rows(xbuf.at[sl], dst, n), gsem.at[sl]).start())

    def wait_in(blk, sl):
        n = bvalid_ref[blk]
        pltpu.make_async_copy(_rows(xs_hbm, 0, n), _rows(xbuf.at[sl], 0, n), gsem.at[sl]).wait()

    def wait_out(blk, sl):
        n = bvalid_ref[blk]
        pltpu.make_async_copy(_rows(ybuf.at[sl], 0, n), _rows(ys_hbm, 0, n), osem.at[sl]).wait()

    @pl.when(b == 0)
    def _():
        gather(0, 0)

    @pl.when(b + 1 < nblk)
    def _():
        gather(b + 1, 1 - slot)

    @pl.when(b < nblk)
    def _():
        wait_in(b, slot)

        @pl.when(b >= 2)
        def _():
            wait_out(b - 2, slot)

        tm = EXPERT_TILE
        x = jnp.concatenate([xbuf[slot, pl.ds(s, tm, stride=ROW_TILES), :]
                             for s in range(ROW_TILES)], axis=-1)
        row = lax.broadcasted_iota(I32, (tm, 1), 0)
        xb = jnp.where(row < bvalid_ref[b], x, 0.0).astype(BF16)
        for cidx in range(D_FF // FF_CHUNK):
            sl = slice(cidx * FF_CHUNK, (cidx + 1) * FF_CHUNK)
            gt = jnp.dot(xb, wg_ref[0, :, sl], preferred_element_type=F32) + bg_ref[0, :, sl]
            up = jnp.dot(xb, wu_ref[0, :, sl], preferred_element_type=F32) + bu_ref[0, :, sl]
            gt = jnp.minimum(gt, SWIGLU_LIMIT)
            up = jnp.clip(up, -SWIGLU_LIMIT, SWIGLU_LIMIT)
            act_ref[:, sl] = (gt * _sigmoid(SWIGLU_ALPHA * gt) * (up + 1.0)).astype(BF16)
        for cidx in range(D_MODEL // FF_CHUNK):
            sl = slice(cidx * FF_CHUNK, (cidx + 1) * FF_CHUNK)
            o = jnp.dot(act_ref[...], wd_ref[0, :, sl], preferred_element_type=F32) + bd_ref[0, :, sl]
            for q in range(FF_CHUNK // LANES):
                s = cidx * (FF_CHUNK // LANES) + q
                ybuf[slot, pl.ds(s, tm, stride=ROW_TILES), :] = o[:, q * LANES:(q + 1) * LANES]
        _for_each_run(b, *tabs, lambda dst, src, n: pltpu.make_async_copy(
            _rows(ybuf.at[slot], src, n), _rows(ys_hbm, dst, n), osem.at[slot]).start())

        @pl.when(b == nblk - 1)
        def _():
            wait_out(b, slot)

            @pl.when(b >= 1)
            def _():
                wait_out(b - 1, 1 - slot)


def _moe_experts(tables, xs, wg, bg, wu, bu, wd, bd):
    bexp = tables[0]
    nblocks = bexp.shape[0]
    tm = EXPERT_TILE
    wsel = lambda i, be, bv, nb, *_: (be[jnp.minimum(i, nb[0] - 1)], 0, 0)
    wspec = pl.BlockSpec((1, D_MODEL, D_FF), wsel)
    bspec = pl.BlockSpec((1, 1, D_FF), wsel)
    hbm = pl.BlockSpec(memory_space=pl.ANY)
    return pl.pallas_call(
        _experts_kernel,
        out_shape=jax.ShapeDtypeStruct(xs.shape, F32),
        grid_spec=pltpu.PrefetchScalarGridSpec(
            num_scalar_prefetch=len(tables),
            grid=(nblocks,),
            in_specs=[hbm, wspec, bspec, wspec, bspec, wspec, bspec],
            out_specs=hbm,
            scratch_shapes=[pltpu.VMEM((2, tm * ROW_TILES, LANES), F32),
                            pltpu.VMEM((2, tm * ROW_TILES, LANES), F32),
                            pltpu.VMEM((tm, D_FF), BF16),
                            pltpu.SemaphoreType.DMA((2,)), pltpu.SemaphoreType.DMA((2,))]),
        compiler_params=_params("arbitrary"),
        name="moe_experts",
    )(*tables, xs, wg, bg, wu, bu, wd, bd)


def _combine_kernel(slot_hbm, gate_hbm, ys_ref, x2_ref, gfin_ref, outp_ref, outs_ref,
                    slot_smem, gate_smem, acc_ref, isem, *, prompt_tiles):
    i = pl.program_id(0)
    tm = x2_ref.shape[0]
    slot_cp = pltpu.make_async_copy(slot_hbm.at[i], slot_smem, isem.at[0])
    gate_cp = pltpu.make_async_copy(gate_hbm.at[i], gate_smem, isem.at[1])
    slot_cp.start()
    gate_cp.start()
    x2 = x2_ref[...]
    for s in range(ROW_TILES):
        acc_ref[pl.ds(s, tm, stride=ROW_TILES), :] = x2[:, s * LANES:(s + 1) * LANES]
    slot_cp.wait()
    gate_cp.wait()

    for c in range(tm // LANES):
        def body(j, _, c=c):
            t = c * LANES + j
            rows = pl.ds(pl.multiple_of(t * ROW_TILES, ROW_TILES), ROW_TILES)
            acc = acc_ref[rows, :]
            for k in range(TOP_K):
                src = pl.multiple_of(slot_smem[k, c, j] * ROW_TILES, ROW_TILES)
                acc = acc + gate_smem[k, c, j] * ys_ref[pl.ds(src, ROW_TILES), :]
            acc_ref[rows, :] = acc
            return 0

        lax.fori_loop(0, LANES, body, 0, unroll=8)
    acc = jnp.concatenate([acc_ref[pl.ds(s, tm, stride=ROW_TILES), :] for s in range(ROW_TILES)],
                          axis=-1)
    y = _rms(acc) * gfin_ref[...]

    @pl.when(i < prompt_tiles)
    def _():
        outp_ref[...] = y

    @pl.when(i >= prompt_tiles)
    def _():
        outs_ref[...] = y


def _moe_combine(slots, gates, ys, x2, gfin, t_prompt):
    t, d = x2.shape
    tm = TOKEN_TILE
    ptiles = t_prompt // tm
    return pl.pallas_call(
        functools.partial(_combine_kernel, prompt_tiles=ptiles),
        out_shape=(jax.ShapeDtypeStruct((t_prompt, d), F32),
                   jax.ShapeDtypeStruct((t - t_prompt, d), F32)),
        grid=(t // tm,),
        in_specs=[pl.BlockSpec(memory_space=pl.ANY), pl.BlockSpec(memory_space=pl.ANY),
                  pl.BlockSpec((tm * TOP_K * ROW_TILES, LANES), lambda i: (i, 0)),
                  pl.BlockSpec((tm, d), lambda i: (i, 0)),
                  pl.BlockSpec((1, d), lambda i: (0, 0))],
        out_specs=(pl.BlockSpec((tm, d), lambda i: (jnp.minimum(i, ptiles - 1), 0)),
                   pl.BlockSpec((tm, d), lambda i: (jnp.maximum(i - ptiles, 0), 0))),
        scratch_shapes=[pltpu.SMEM(slots.shape[1:], I32), pltpu.SMEM(gates.shape[1:], F32),
                        pltpu.VMEM((tm * ROW_TILES, LANES), F32),
                        pltpu.SemaphoreType.DMA((2,))],
        compiler_params=_params("arbitrary"),
        name="moe_combine",
    )(slots, gates, ys, x2, gfin)


def _lru_gate_weights(w_a, b_a, w_x, b_x):
    hpb = LRU_BLOCK // LRU_HEAD_DIM
    ncb = D_LRU // LRU_BLOCK

    def blockdiag(w):
        w = w.reshape(ncb, hpb, LRU_HEAD_DIM, LRU_HEAD_DIM)
        eye = jnp.eye(hpb, dtype=w.dtype)
        full = jnp.einsum('nhij,hg->nhigj', w, eye)
        return full.reshape(ncb, LRU_BLOCK, LRU_BLOCK)

    mats = [blockdiag(w_a[0]), blockdiag(w_a[1]), blockdiag(w_x[0]), blockdiag(w_x[1])]
    w = jnp.concatenate(mats, axis=-1).astype(BF16)
    bs = [b.reshape(ncb, 1, LRU_BLOCK) for b in (b_a[0], b_a[1], b_x[0], b_x[1])]
    return w, jnp.concatenate(bs, axis=-1).astype(F32)


def _encoder_layer(xp, xs, mem, norm_mix, w_in, conv_w, conv_b, lru_w_a, lru_b_a, lru_w_x, lru_b_x,
                   lru_lambda, norm_lru_out, norm_fft_out, w_out, norm_xattn, norm_mem, w_q, w_kv,
                   w_o, norm_ffn, w_router, b_router, w_gate, b_gate, w_up, b_up, w_down, b_down,
                   norm_final):
    nbp, seq, d = xp.shape
    nb = nbp + xs.shape[0]
    t = nb * seq
    seqz = seq // DFT_RADIX * DFT_PITCH
    ncb = D_LRU // LRU_BLOCK
    row = lambda v: v.reshape(1, -1).astype(F32)

    dft_c, f1, m2 = _dft_tables(seq)
    kt, v = _kv_proj(mem, row(norm_mem), w_kv[:, :d].T.astype(BF16), w_kv[:, d:].astype(BF16))

    xrec, gate, zr, zi = _in_proj(xp.reshape(nbp * seq, d), xs.reshape(t - nbp * seq, d),
                                  row(norm_mix), w_in.astype(BF16), dft_c.astype(BF16))

    w_gates, b_gates = _lru_gate_weights(lru_w_a, lru_b_a, lru_w_x, lru_b_x)
    cw = conv_w.reshape(CONV_WIDTH, ncb, LRU_BLOCK).transpose(1, 0, 2)
    cb = conv_b.reshape(ncb, 1, LRU_BLOCK)
    lam = lru_lambda.reshape(2, ncb, LRU_BLOCK).transpose(1, 0, 2)
    ylru = _lru_mix(xrec.reshape(nb, seq, D_LRU), gate.reshape(nb, seq, D_LRU), cw, cb,
                    w_gates, b_gates, lam)

    yfft = _seq_dft(zr.reshape(nb, seqz, D_FFT), zi.reshape(nb, seqz, D_FFT),
                    f1.astype(BF16), m2.astype(BF16))

    w_r = jnp.pad(w_router.astype(F32), ((0, 0), (0, LANES - N_EXPERTS)))
    w_r_hi = w_r.astype(BF16)
    w_r_lo = (w_r - w_r_hi.astype(F32)).astype(BF16)
    b_r = jnp.pad(b_router.astype(F32).reshape(1, -1), ((0, 0), (0, LANES - N_EXPERTS)))
    x2, xs_rows, slots, gates, tile_cnt = _attn_route(
        xp, xs, ylru, yfft, kt, v, row(norm_lru_out), row(norm_fft_out), w_out.astype(BF16),
        row(norm_xattn), w_q.astype(BF16), w_o.astype(BF16), row(norm_ffn), w_r_hi, w_r_lo, b_r)

    tmb = EXPERT_TILE
    nblocks = -(-(t * TOP_K) // tmb) + N_EXPERTS
    tile_cnt = tile_cnt.reshape(-1, N_EXPERTS)
    cum_end = jnp.cumsum(tile_cnt, axis=0)
    cum = cum_end - tile_cnt
    off = jnp.cumsum(tile_cnt, axis=1) - tile_cnt
    counts = cum_end[-1]
    padded = ((counts + tmb - 1) // tmb) * tmb
    pend = jnp.cumsum(padded)
    pstart = pend - padded
    blk_start = jnp.arange(nblocks, dtype=I32) * tmb
    bexp = jnp.minimum(jnp.sum((pend[None, :] <= blk_start[:, None]).astype(I32), axis=1),
                       N_EXPERTS - 1)
    ostart = blk_start - pstart[bexp]
    bvalid = jnp.clip(counts[bexp] - ostart, 0, tmb).astype(I32)
    nblk = (pend[-1:] // tmb).astype(I32)
    cum_b = jnp.take(cum, bexp, axis=1)
    end_b = jnp.take(cum_end, bexp, axis=1)
    ntiles = tile_cnt.shape[0]
    t0 = jnp.minimum(jnp.sum((end_b <= ostart[None, :]).astype(I32), axis=0), ntiles - 1)
    t1 = jnp.maximum(jnp.sum((cum_b < (ostart + tmb)[None, :]).astype(I32), axis=0) - 1, 0)
    tables = tuple(a.astype(I32) for a in (bexp, bvalid, nblk, ostart, t0, t1, cum.reshape(-1),
                                           tile_cnt.reshape(-1), off.reshape(-1)))

    ys = _moe_experts(tables, xs_rows,
                      w_gate.astype(BF16), b_gate.reshape(N_EXPERTS, 1, D_FF),
                      w_up.astype(BF16), b_up.reshape(N_EXPERTS, 1, D_FF),
                      w_down.astype(BF16), b_down.reshape(N_EXPERTS, 1, D_MODEL))
    outp, outs = _moe_combine(slots, gates, ys, x2.reshape(t, d), row(norm_final), nbp * seq)
    return outp.reshape(nbp, seq, d), outs.reshape(nb - nbp, seq, d)


def kernel(x_prompt, x_sample, mem_prompt, mem_sample, norm_mix, w_in, conv_w, conv_b, lru_w_a,
           lru_b_a, lru_w_x, lru_b_x, lru_lambda, norm_lru_out, norm_fft_out, w_out, norm_xattn,
           norm_mem, w_q, w_kv, w_o, norm_ffn, w_router, b_router, w_gate, b_gate, w_up, b_up,
           w_down, b_down, norm_final):
    assert x_prompt.shape[1:] == x_sample.shape[1:], "both groups must share (SEQ, D_MODEL)"
    assert w_in.shape[0] == 1, "single-layer block"
    mem = jnp.concatenate([mem_prompt, mem_sample], axis=0)
    return _encoder_layer(x_prompt, x_sample, mem, norm_mix[0], w_in[0], conv_w[0], conv_b[0],
                          lru_w_a[0], lru_b_a[0], lru_w_x[0], lru_b_x[0], lru_lambda[0],
                          norm_lru_out[0], norm_fft_out[0], w_out[0], norm_xattn[0], norm_mem[0],
                          w_q[0], w_kv[0], w_o[0], norm_ffn[0], w_router[0], b_router[0],
                          w_gate[0], b_gate[0], w_up[0], b_up[0], w_down[0], b_down[0], norm_final)
```

```python
import functools
import math

import numpy as np
import jax
import jax.numpy as jnp
from jax import lax
from jax.experimental import pallas as pl
from jax.experimental.pallas import tpu as pltpu

F32, BF16, I32 = jnp.float32, jnp.bfloat16, jnp.int32

D_MODEL = 1024
D_LRU = 512
LRU_HEAD_DIM = 64
CONV_WIDTH = 4
LRU_C = 8.0
D_FFT = 512
FFT_GROUP_DIM = 128
N_XATTN_HEADS = 4
XATTN_HEAD_DIM = 256
N_EXPERTS = 32
TOP_K = 4
D_FF = 1024
SWIGLU_LIMIT = 7.0
SWIGLU_ALPHA = 1.702
EPS = 1e-6

LANES = 128
SUBLANES = 8
ROW_TILES = D_MODEL // LANES
TOKEN_TILE = 512
EXPERT_TILE = 512
FF_CHUNK = 256
LRU_BLOCK = 128
SCAN_CHUNK = 128
SCAN_PITCH = SCAN_CHUNK + SUBLANES
DFT_RADIX = 64
DFT_PITCH = DFT_RADIX + SUBLANES
DFT_UNROLL = 4
VMEM_LIMIT = 56 * 1024 * 1024


def _params(*sem):
    return pltpu.CompilerParams(dimension_semantics=sem, vmem_limit_bytes=VMEM_LIMIT)


def _rms(xf):
    return xf * lax.rsqrt(jnp.mean(xf * xf, axis=-1, keepdims=True) + EPS)


def _sigmoid(x):
    return 1.0 / (1.0 + jnp.exp(-x))


def _gelu_tanh(x):
    return 0.5 * x * (1.0 + jnp.tanh(math.sqrt(2.0 / math.pi) * (x + 0.044715 * (x * x * x))))


def _kv_kernel(mem_ref, g_ref, wkt_ref, wv_ref, kt_ref, v_ref):
    mn = (_rms(mem_ref[0]) * g_ref[...]).astype(BF16)
    kt = lax.dot_general(wkt_ref[...], mn, (((1,), (1,)), ((), ())), preferred_element_type=F32)
    kt_ref[0] = kt.astype(BF16)
    v_ref[0] = jnp.dot(mn, wv_ref[...], preferred_element_type=F32).astype(BF16)


def _kv_proj(mem, g, wkt, wv):
    nb, m, d = mem.shape
    return pl.pallas_call(
        _kv_kernel,
        out_shape=(jax.ShapeDtypeStruct((nb, d, m), BF16), jax.ShapeDtypeStruct((nb, m, d), BF16)),
        grid=(nb,),
        in_specs=[pl.BlockSpec((1, m, d), lambda b: (b, 0, 0)),
                  pl.BlockSpec((1, d), lambda b: (0, 0)),
                  pl.BlockSpec((d, d), lambda b: (0, 0)),
                  pl.BlockSpec((d, d), lambda b: (0, 0))],
        out_specs=(pl.BlockSpec((1, d, m), lambda b: (b, 0, 0)),
                   pl.BlockSpec((1, m, d), lambda b: (b, 0, 0))),
        compiler_params=_params("arbitrary"),
        name="kv_proj",
    )(mem, g, wkt, wv)


def _inproj_kernel(xp_ref, xs_ref, g_ref, w_ref, dft_ref, xrec_ref, gate_ref, zr_ref, zi_ref,
                   *, prompt_tiles):
    x = jnp.where(pl.program_id(0) < prompt_tiles, xp_ref[...], xs_ref[...])
    h = (_rms(x) * g_ref[...]).astype(BF16)
    proj = jnp.dot(h, w_ref[...], preferred_element_type=F32)
    xrec_ref[...] = proj[:, :D_LRU]
    gate_ref[...] = proj[:, D_LRU:2 * D_LRU].astype(BF16)
    z = jnp.dot(proj[:, 2 * D_LRU:].astype(BF16), dft_ref[...], preferred_element_type=F32)
    pad = jnp.zeros((DFT_PITCH - DFT_RADIX, D_FFT), F32)
    for m in range(x.shape[0] // DFT_RADIX):
        rows = slice(m * DFT_RADIX, (m + 1) * DFT_RADIX)
        dst = slice(m * DFT_PITCH, m * DFT_PITCH + DFT_RADIX)
        gap = slice(m * DFT_PITCH + DFT_RADIX, (m + 1) * DFT_PITCH)
        zr_ref[dst, :] = z[rows, :D_FFT]
        zi_ref[dst, :] = z[rows, D_FFT:]
        zr_ref[gap, :] = pad
        zi_ref[gap, :] = pad


def _in_proj(xp2d, xs2d, g, w_in, dft_c):
    tm = TOKEN_TILE
    ptiles = xp2d.shape[0] // tm
    t = xp2d.shape[0] + xs2d.shape[0]
    tz = t // DFT_RADIX * DFT_PITCH
    tmz = tm // DFT_RADIX * DFT_PITCH
    row = lambda i: (i, 0)
    full = lambda i: (0, 0)
    return pl.pallas_call(
        functools.partial(_inproj_kernel, prompt_tiles=ptiles),
        out_shape=(jax.ShapeDtypeStruct((t, D_LRU), F32), jax.ShapeDtypeStruct((t, D_LRU), BF16),
                   jax.ShapeDtypeStruct((tz, D_FFT), F32), jax.ShapeDtypeStruct((tz, D_FFT), F32)),
        grid=(t // tm,),
        in_specs=[pl.BlockSpec((tm, D_MODEL), lambda i: (jnp.minimum(i, ptiles - 1), 0)),
                  pl.BlockSpec((tm, D_MODEL), lambda i: (jnp.maximum(i - ptiles, 0), 0)),
                  pl.BlockSpec((1, D_MODEL), full),
                  pl.BlockSpec(w_in.shape, full), pl.BlockSpec(dft_c.shape, full)],
        out_specs=(pl.BlockSpec((tm, D_LRU), row), pl.BlockSpec((tm, D_LRU), row),
                   pl.BlockSpec((tmz, D_FFT), row), pl.BlockSpec((tmz, D_FFT), row)),
        compiler_params=_params("arbitrary"),
        name="in_proj",
    )(xp2d, xs2d, g, w_in, dft_c)


def _lru_kernel(x_ref, gate_ref, cw_ref, cb_ref, w_ref, b_ref, lam_ref, y_ref,
                af_ref, uf_ref, ab_ref, ub_ref, cf_ref, cbk_ref, *, seq):
    nch = seq // SCAN_CHUNK
    cw = cw_ref[0]
    cb = cb_ref[0]
    lam = lam_ref[0]
    sp = jnp.maximum(-lam, 0.0) + jnp.log1p(jnp.exp(-jnp.abs(lam)))
    bias = b_ref[0]
    c = LRU_BLOCK

    def gates_body(j, _):
        r0 = pl.multiple_of(j * SCAN_CHUNK, SCAN_CHUNK)
        main = x_ref[0, pl.ds(r0, SCAN_CHUNK), :]
        prev = x_ref[0, pl.ds(jnp.maximum(r0 - SUBLANES, 0), SUBLANES), :]
        nxt = x_ref[0, pl.ds(jnp.minimum(r0 + SCAN_CHUNK, seq - SUBLANES), SUBLANES), :]
        prev = jnp.where(j > 0, prev, 0.0)
        nxt = jnp.where(j < nch - 1, nxt, 0.0)
        win = jnp.concatenate([prev, main, nxt], axis=0)
        base = SUBLANES - CONV_WIDTH // 2
        xc = cb
        for tap in range(CONV_WIDTH):
            xc = xc + win[base + tap:base + tap + SCAN_CHUNK, :] * cw[tap:tap + 1, :]
        xcb = xc.astype(BF16)
        o0 = pl.multiple_of(j * SCAN_PITCH, SUBLANES)
        for d, (a_ref, u_ref) in enumerate(((af_ref, uf_ref), (ab_ref, ub_ref))):
            ga = jnp.dot(xcb, w_ref[0, :, d * c:(d + 1) * c], preferred_element_type=F32) \
                + bias[:, d * c:(d + 1) * c]
            gx = jnp.dot(xcb, w_ref[0, :, (2 + d) * c:(3 + d) * c], preferred_element_type=F32) \
                + bias[:, (2 + d) * c:(3 + d) * c]
            log_a = (-LRU_C) * _sigmoid(ga) * sp[d:d + 1, :]
            a = jnp.exp(log_a)
            u = jnp.sqrt(-jnp.tanh(log_a) * (1.0 + a * a)) * (_sigmoid(gx) * xc)
            a_ref[pl.ds(o0, SCAN_CHUNK), :] = a
            u_ref[pl.ds(o0, SCAN_CHUNK), :] = u
        return 0

    lax.fori_loop(0, nch, gates_body, 0)

    def scan_body(t, carry):
        hf, pf, hb, pb = carry
        fwd = pl.ds(t, nch, stride=SCAN_PITCH)
        bwd = pl.ds(SCAN_CHUNK - 1 - t, nch, stride=SCAN_PITCH)
        a = af_ref[fwd, :]
        hf = a * hf + uf_ref[fwd, :]
        pf = a * pf
        uf_ref[fwd, :] = hf
        af_ref[fwd, :] = pf
        a = ab_ref[bwd, :]
        hb = a * hb + ub_ref[bwd, :]
        pb = a * pb
        ub_ref[bwd, :] = hb
        ab_ref[bwd, :] = pb
        return hf, pf, hb, pb

    zeros = jnp.zeros((nch, c), F32)
    ones = jnp.ones((nch, c), F32)
    hf, pf, hb, pb = lax.fori_loop(0, SCAN_CHUNK, scan_body, (zeros, ones, zeros, ones))

    carry = jnp.zeros((1, c), F32)
    cf_ref[0:1, :] = carry
    for j in range(1, nch):
        carry = hf[j - 1:j, :] + pf[j - 1:j, :] * carry
        cf_ref[j:j + 1, :] = carry
    carry = jnp.zeros((1, c), F32)
    cbk_ref[nch - 1:nch, :] = carry
    for j in range(nch - 2, -1, -1):
        carry = hb[j + 1:j + 2, :] + pb[j + 1:j + 2, :] * carry
        cbk_ref[j:j + 1, :] = carry

    def out_body(j, _):
        r0 = pl.multiple_of(j * SCAN_CHUNK, SCAN_CHUNK)
        o0 = pl.multiple_of(j * SCAN_PITCH, SUBLANES)
        rows = pl.ds(o0, SCAN_CHUNK)
        h = (uf_ref[rows, :] + af_ref[rows, :] * cf_ref[pl.ds(j, 1), :]
             + ub_ref[rows, :] + ab_ref[rows, :] * cbk_ref[pl.ds(j, 1), :])
        g = gate_ref[0, pl.ds(r0, SCAN_CHUNK), :].astype(F32)
        y_ref[0, pl.ds(r0, SCAN_CHUNK), :] = (h * _gelu_tanh(g)).astype(BF16)
        return 0

    lax.fori_loop(0, nch, out_body, 0)


def _lru_mix(xrec, gate, conv_w, conv_b, w_gates, b_gates, lam):
    nb, seq, _ = xrec.shape
    ncb = D_LRU // LRU_BLOCK
    c = LRU_BLOCK
    nch = seq // SCAN_CHUNK
    act = lambda b, k: (b, 0, k)
    par = lambda b, k: (k, 0, 0)
    scan_buf = pltpu.VMEM((nch * SCAN_PITCH, c), F32)
    return pl.pallas_call(
        functools.partial(_lru_kernel, seq=seq),
        out_shape=jax.ShapeDtypeStruct((nb, seq, D_LRU), BF16),
        grid=(nb, ncb),
        in_specs=[pl.BlockSpec((1, seq, c), act), pl.BlockSpec((1, seq, c), act),
                  pl.BlockSpec((1, CONV_WIDTH, c), par), pl.BlockSpec((1, 1, c), par),
                  pl.BlockSpec((1, c, 4 * c), par), pl.BlockSpec((1, 1, 4 * c), par),
                  pl.BlockSpec((1, 2, c), par)],
        out_specs=pl.BlockSpec((1, seq, c), act),
        scratch_shapes=[scan_buf, scan_buf, scan_buf, scan_buf,
                        pltpu.VMEM((nch, c), F32), pltpu.VMEM((nch, c), F32)],
        compiler_params=_params("arbitrary", "arbitrary"),
        name="lru_mix",
    )(xrec, gate, conv_w, conv_b, w_gates, b_gates, lam)


def _seqdft_kernel(zr_ref, zi_ref, f1_ref, m2_ref, y_ref, ar_ref, ai_ref):
    r, pitch, unroll = DFT_RADIX, DFT_PITCH, DFT_UNROLL
    f1 = f1_ref[...]

    def stage1(it, _):
        i2 = it * unroll
        slabs = []
        for q in range(unroll):
            rows = pl.ds(i2 + q, r, stride=pitch)
            slabs.append(jnp.concatenate([zr_ref[0, rows, :], zi_ref[0, rows, :]], axis=0))
        rhs = jnp.concatenate(slabs, axis=1).astype(BF16)
        a = jnp.dot(f1, rhs, preferred_element_type=F32)
        for q in range(unroll):
            rows = pl.ds(i2 + q, r, stride=pitch)
            ar_ref[rows, :] = a[:r, q * LANES:(q + 1) * LANES]
            ai_ref[rows, :] = a[r:, q * LANES:(q + 1) * LANES]
        return 0

    lax.fori_loop(0, r // unroll, stage1, 0)

    def stage2(it, _):
        for q in range(unroll):
            k1 = it * unroll + q
            src = pl.ds(pl.multiple_of(k1 * pitch, SUBLANES), r)
            slab = jnp.concatenate([ar_ref[src, :], ai_ref[src, :]], axis=0).astype(BF16)
            o = jnp.dot(m2_ref[k1], slab, preferred_element_type=F32)
            y_ref[0, pl.ds(k1, r, stride=pitch), :] = o
        return 0

    lax.fori_loop(0, r // unroll, stage2, 0)
    for gap in range(r, pitch):
        y_ref[0, pl.ds(gap, r, stride=pitch), :] = jnp.zeros((r, LANES), F32)


def _seq_dft(zr, zi, f1, m2):
    nb, rows, _ = zr.shape
    cw = LANES
    act = lambda b, k: (b, 0, k)
    return pl.pallas_call(
        _seqdft_kernel,
        out_shape=jax.ShapeDtypeStruct((nb, rows, D_FFT), F32),
        grid=(nb, D_FFT // cw),
        in_specs=[pl.BlockSpec((1, rows, cw), act), pl.BlockSpec((1, rows, cw), act),
                  pl.BlockSpec(f1.shape, lambda b, k: (0, 0)),
                  pl.BlockSpec(m2.shape, lambda b, k: (0, 0, 0))],
        out_specs=pl.BlockSpec((1, rows, cw), act),
        scratch_shapes=[pltpu.VMEM((rows, cw), F32), pltpu.VMEM((rows, cw), F32)],
        compiler_params=_params("arbitrary", "arbitrary"),
        name="seq_dft",
    )(zr, zi, f1, m2)


def _dft_tables(seq):
    n1 = n2 = DFT_RADIX
    assert n1 * n2 == seq, "sequence DFT is factored as DFT_RADIX x DFT_RADIX"
    g = FFT_GROUP_DIM
    ang = 2.0 * np.pi * np.outer(np.arange(g), np.arange(g)) / g
    cg, sg = np.cos(ang) / math.sqrt(g), np.sin(ang) / math.sqrt(g)
    ngroups = D_FFT // g
    dft_c = np.zeros((D_FFT, 2 * D_FFT))
    for q in range(ngroups):
        dft_c[q * g:(q + 1) * g, q * g:(q + 1) * g] = cg
        dft_c[q * g:(q + 1) * g, D_FFT + q * g:D_FFT + (q + 1) * g] = -sg
    ang1 = 2.0 * np.pi * np.outer(np.arange(n1), np.arange(n1)) / n1
    c1, s1 = np.cos(ang1) / math.sqrt(n1), np.sin(ang1) / math.sqrt(n1)
    f1 = np.block([[c1, s1], [-s1, c1]])
    k1 = np.arange(n1)[:, None, None]
    k2 = np.arange(n2)[None, :, None]
    i2 = np.arange(n2)[None, None, :]
    ang2 = 2.0 * np.pi * (i2 * k2 / n2 + i2 * k1 / seq)
    m2 = np.concatenate([np.cos(ang2), np.sin(ang2)], axis=2) / math.sqrt(n2)
    return (jnp.asarray(dft_c, F32), jnp.asarray(f1, F32), jnp.asarray(m2, F32))


def _to_flat_smem(src_ref, smem_ref, sem):
    nk, nc, nl = src_ref.shape
    copies = [pltpu.make_async_copy(src_ref.at[k, c], smem_ref.at[pl.ds((k * nc + c) * nl, nl)], sem)
              for k in range(nk) for c in range(nc)]
    for cp in copies:
        cp.start()
    for cp in copies:
        cp.wait()


def _attn_route_kernel(xp_ref, xs_ref, ylru_ref, yfft_ref, kt_ref, v_ref, gl_ref, gf_ref, wout_ref,
                       gx_ref, wq_ref, wo_ref, gffn_ref, wrh_ref, wrl_ref, br_ref,
                       x2_ref, xs_out_ref, slot_ref, gate_ref, cnt_ref, h3tl_ref, slot_smem, ssem,
                       *, prompt_batches):
    tm = xp_ref.shape[1]
    x = jnp.where(pl.program_id(0) < prompt_batches, xp_ref[0], xs_ref[0])
    m_lru = _rms(ylru_ref[0].astype(F32)) * gl_ref[...]
    yfft = jnp.concatenate([yfft_ref[0, m * DFT_PITCH:m * DFT_PITCH + DFT_RADIX, :]
                            for m in range(tm // DFT_RADIX)], axis=0)
    m_fft = _rms(yfft) * gf_ref[...]
    mixed = jnp.concatenate([m_lru, m_fft], axis=-1).astype(BF16)
    x1 = x + jnp.dot(mixed, wout_ref[...], preferred_element_type=F32)

    hq = (_rms(x1) * gx_ref[...]).astype(BF16)
    q = jnp.dot(hq, wq_ref[...], preferred_element_type=F32) * (XATTN_HEAD_DIM ** -0.5)
    qb = q.astype(BF16)
    heads = []
    for h in range(N_XATTN_HEADS):
        sl = slice(h * XATTN_HEAD_DIM, (h + 1) * XATTN_HEAD_DIM)
        s = jnp.dot(qb[:, sl], kt_ref[0, sl, :], preferred_element_type=F32)
        p = jnp.exp(s - jnp.max(s, axis=-1, keepdims=True))
        inv = 1.0 / jnp.sum(p, axis=-1, keepdims=True)
        o = jnp.dot(p.astype(BF16), v_ref[0, :, sl], preferred_element_type=F32) * inv
        heads.append(o.astype(BF16))
    att = jnp.concatenate(heads, axis=-1)
    x2 = x1 + jnp.dot(att, wo_ref[...], preferred_element_type=F32)
    x2_ref[0] = x2

    h3 = _rms(x2) * gffn_ref[...]
    for s in range(ROW_TILES):
        h3tl_ref[pl.ds(s, tm, stride=ROW_TILES), :] = h3[:, s * LANES:(s + 1) * LANES]

    h_hi = h3.astype(BF16)
    h_lo = (h3 - h_hi.astype(F32)).astype(BF16)
    logits = (jnp.dot(h_hi, wrh_ref[...], preferred_element_type=F32)
              + jnp.dot(h_lo, wrh_ref[...], preferred_element_type=F32)
              + jnp.dot(h_hi, wrl_ref[...], preferred_element_type=F32)) + br_ref[...]
    lg = logits.T[:N_EXPERTS, :]
    e_iota = lax.broadcasted_iota(I32, lg.shape, 0)
    vals, onehots = [], []
    for k in range(TOP_K):
        m = jnp.max(lg, axis=0, keepdims=True)
        idx = jnp.min(jnp.where(lg == m, e_iota, N_EXPERTS), axis=0, keepdims=True)
        oh = e_iota == idx
        vals.append(m)
        onehots.append(oh)
        lg = jnp.where(oh, -jnp.inf, lg)
    exps = [jnp.exp(v - vals[0]) for v in vals]
    inv = 1.0 / (exps[0] + exps[1] + exps[2] + exps[3])

    def lane_tiles(rows):
        return jnp.stack([jnp.concatenate([r[:, c * LANES:(c + 1) * LANES]
                                           for c in range(tm // LANES)], axis=0) for r in rows])

    gate_ref[0] = lane_tiles([e * inv for e in exps])

    member = jnp.zeros(lg.shape, F32)
    for oh in onehots:
        member = member + oh.astype(F32)
    mb = member.astype(BF16)
    r_i = lax.broadcasted_iota(I32, (tm, tm), 0)
    c_i = lax.broadcasted_iota(I32, (tm, tm), 1)
    earlier = jnp.where(r_i < c_i, 1.0, 0.0).astype(BF16)
    tok_rank = jnp.dot(mb, earlier, preferred_element_type=F32)
    er = lax.broadcasted_iota(I32, (N_EXPERTS, N_EXPERTS), 0)
    ec = lax.broadcasted_iota(I32, (N_EXPERTS, N_EXPERTS), 1)
    lower = jnp.where(ec < er, 1.0, 0.0).astype(BF16)
    exp_off = jnp.sum(jnp.dot(lower, mb, preferred_element_type=F32), axis=1, keepdims=True)
    pos = exp_off + tok_rank
    slots = [jnp.sum(jnp.where(oh, pos, 0.0), axis=0, keepdims=True) for oh in onehots]
    slot_ref[0] = lane_tiles(slots).astype(I32) * ROW_TILES
    ones = jnp.ones((SUBLANES, tm), BF16)
    cnt = lax.dot_general(ones, mb, (((1,), (1,)), ((), ())), preferred_element_type=F32)
    cnt_ref[0] = cnt[0:1, :].astype(I32)

    _to_flat_smem(slot_ref.at[0], slot_smem, ssem)

    def place(t, _):
        row = h3tl_ref[pl.ds(pl.multiple_of(t * ROW_TILES, ROW_TILES), ROW_TILES), :]
        for k in range(TOP_K):
            dst = pl.multiple_of(slot_smem[k * tm + t], ROW_TILES)
            xs_out_ref[pl.ds(dst, ROW_TILES), :] = row
        return 0

    lax.fori_loop(0, tm, place, 0, unroll=8)


def _attn_route(xp, xs, ylru, yfft, kt, v, gl, gf, w_out, gx, wq, wo, gffn, wr_hi, wr_lo, br):
    nbp, seq, d = xp.shape
    nb = nbp + xs.shape[0]
    tm = TOKEN_TILE
    tmz = tm // DFT_RADIX * DFT_PITCH
    nt = seq // tm
    m = v.shape[1]
    tok = lambda b, i: (b, i, 0)
    full2 = lambda b, i: (0, 0)
    per_b = lambda b, i: (b, 0, 0)
    tiles = nb * nt
    tile_id = lambda b, i: (b * nt + i, 0, 0)
    tile_id4 = lambda b, i: (b * nt + i, 0, 0, 0)
    nw = wr_hi.shape[1]
    per_tok = (TOP_K, tm // LANES, LANES)
    return pl.pallas_call(
        functools.partial(_attn_route_kernel, prompt_batches=nbp),
        out_shape=(jax.ShapeDtypeStruct((nb, seq, d), F32),
                   jax.ShapeDtypeStruct((tiles * tm * TOP_K * ROW_TILES, LANES), F32),
                   jax.ShapeDtypeStruct((tiles,) + per_tok, I32),
                   jax.ShapeDtypeStruct((tiles,) + per_tok, F32),
                   jax.ShapeDtypeStruct((tiles, 1, N_EXPERTS), I32)),
        grid=(nb, nt),
        in_specs=[pl.BlockSpec((1, tm, d), lambda b, i: (jnp.minimum(b, nbp - 1), i, 0)),
                  pl.BlockSpec((1, tm, d), lambda b, i: (jnp.maximum(b - nbp, 0), i, 0)),
                  pl.BlockSpec((1, tm, D_LRU), tok),
                  pl.BlockSpec((1, tmz, D_FFT), tok),
                  pl.BlockSpec((1, d, m), per_b), pl.BlockSpec((1, m, d), per_b),
                  pl.BlockSpec((1, D_LRU), full2), pl.BlockSpec((1, D_FFT), full2),
                  pl.BlockSpec((d, d), full2), pl.BlockSpec((1, d), full2),
                  pl.BlockSpec((d, d), full2), pl.BlockSpec((d, d), full2),
                  pl.BlockSpec((1, d), full2), pl.BlockSpec((d, nw), full2),
                  pl.BlockSpec((d, nw), full2), pl.BlockSpec((1, nw), full2)],
        out_specs=(pl.BlockSpec((1, tm, d), tok),
                   pl.BlockSpec((tm * TOP_K * ROW_TILES, LANES), lambda b, i: (b * nt + i, 0)),
                   pl.BlockSpec((1,) + per_tok, tile_id4), pl.BlockSpec((1,) + per_tok, tile_id4),
                   pl.BlockSpec((1, 1, N_EXPERTS), tile_id)),
        scratch_shapes=[pltpu.VMEM((tm * ROW_TILES, LANES), F32),
                        pltpu.SMEM((TOP_K * tm,), I32), pltpu.SemaphoreType.DMA],
        compiler_params=_params("arbitrary", "arbitrary"),
        name="attn_route",
    )(xp, xs, ylru, yfft, kt, v, gl, gf, w_out, gx, wq, wo, gffn, wr_hi, wr_lo, br)


def _for_each_run(blk, bexp_ref, ostart_ref, t0_ref, t1_ref, cum_ref, cnt_ref, off_ref, fn):
    e = bexp_ref[blk]
    o0 = ostart_ref[blk]

    def body(i, _):
        j = i * N_EXPERTS + e
        c = cum_ref[j]
        lo = jnp.maximum(o0, c)
        hi = jnp.minimum(o0 + EXPERT_TILE, c + cnt_ref[j])

        @pl.when(hi > lo)
        def _():
            fn(i * (TOKEN_TILE * TOP_K) + off_ref[j] + (lo - c), lo - o0, hi - lo)
        return 0

    lax.fori_loop(t0_ref[blk], t1_ref[blk] + 1, body, 0)


def _rows(ref, row, n_rows):
    return ref.at[pl.ds(pl.multiple_of(row * ROW_TILES, ROW_TILES), n_rows * ROW_TILES), :]


def _experts_kernel(bexp_ref, bvalid_ref, nblk_ref, ostart_ref, t0_ref, t1_ref, cum_ref, cnt_ref,
                    off_ref, xs_hbm, wg_ref, bg_ref, wu_ref, bu_ref, wd_ref, bd_ref, ys_hbm,
                    xbuf, ybuf, act_ref, gsem, osem):
    b = pl.program_id(0)
    nblk = nblk_ref[0]
    slot = lax.rem(b, 2)
    tabs = (bexp_ref, ostart_ref, t0_ref, t1_ref, cum_ref, cnt_ref, off_ref)

    def gather(blk, sl):
        _for_each_run(blk, *tabs, lambda src, dst, n: pltpu.make_async_copy(
            _rows(xs_hbm, src, n), _rows(xbuf.at[sl], dst, n), gsem.at[sl]).start())

    def wait_in(blk, sl):
        n = bvalid_ref[blk]
        pltpu.make_async_copy(_rows(xs_hbm, 0, n), _rows(xbuf.at[sl], 0, n), gsem.at[sl]).wait()

    def wait_out(blk, sl):
        n = bvalid_ref[blk]
        pltpu.make_async_copy(_rows(ybuf.at[sl], 0, n), _rows(ys_hbm, 0, n), osem.at[sl]).wait()

    @pl.when(b == 0)
    def _():
        gather(0, 0)

    @pl.when(b + 1 < nblk)
    def _():
        gather(b + 1, 1 - slot)

    @pl.when(b < nblk)
    def _():
        wait_in(b, slot)

        @pl.when(b >= 2)
        def _():
            wait_out(b - 2, slot)

        tm = EXPERT_TILE
        x = jnp.concatenate([xbuf[slot, pl.ds(s, tm, stride=ROW_TILES), :]
                             for s in range(ROW_TILES)], axis=-1)
        row = lax.broadcasted_iota(I32, (tm, 1), 0)
        xb = jnp.where(row < bvalid_ref[b], x, 0.0).astype(BF16)
        for cidx in range(D_FF // FF_CHUNK):
            sl = slice(cidx * FF_CHUNK, (cidx + 1) * FF_CHUNK)
            gt = jnp.dot(xb, wg_ref[0, :, sl], preferred_element_type=F32) + bg_ref[0, :, sl]
            up = jnp.dot(xb, wu_ref[0, :, sl], preferred_element_type=F32) + bu_ref[0, :, sl]
            gt = jnp.minimum(gt, SWIGLU_LIMIT)
            up = jnp.clip(up, -SWIGLU_LIMIT, SWIGLU_LIMIT)
            act_ref[:, sl] = (gt * _sigmoid(SWIGLU_ALPHA * gt) * (up + 1.0)).astype(BF16)
        for cidx in range(D_MODEL // FF_CHUNK):
            sl = slice(cidx * FF_CHUNK, (cidx + 1) * FF_CHUNK)
            o = jnp.dot(act_ref[...], wd_ref[0, :, sl], preferred_element_type=F32) + bd_ref[0, :, sl]
            for q in range(FF_CHUNK // LANES):
                s = cidx * (FF_CHUNK // LANES) + q
                ybuf[slot, pl.ds(s, tm, stride=ROW_TILES), :] = o[:, q * LANES:(q + 1) * LANES]
        _for_each_run(b, *tabs, lambda dst, src, n: pltpu.make_async_copy(
            _rows(ybuf.at[slot], src, n), _rows(ys_hbm, dst, n), osem.at[slot]).start())

        @pl.when(b == nblk - 1)
        def _():
            wait_out(b, slot)

            @pl.when(b >= 1)
            def _():
                wait_out(b - 1, 1 - slot)


def _moe_experts(tables, xs, wg, bg, wu, bu, wd, bd):
    bexp = tables[0]
    nblocks = bexp.shape[0]
    tm = EXPERT_TILE
    wsel = lambda i, be, bv, nb, *_: (be[jnp.minimum(i, nb[0] - 1)], 0, 0)
    wspec = pl.BlockSpec((1, D_MODEL, D_FF), wsel)
    bspec = pl.BlockSpec((1, 1, D_FF), wsel)
    hbm = pl.BlockSpec(memory_space=pl.ANY)
    return pl.pallas_call(
        _experts_kernel,
        out_shape=jax.ShapeDtypeStruct(xs.shape, F32),
        grid_spec=pltpu.PrefetchScalarGridSpec(
            num_scalar_prefetch=len(tables),
            grid=(nblocks,),
            in_specs=[hbm, wspec, bspec, wspec, bspec, wspec, bspec],
            out_specs=hbm,
            scratch_shapes=[pltpu.VMEM((2, tm * ROW_TILES, LANES), F32),
                            pltpu.VMEM((2, tm * ROW_TILES, LANES), F32),
                            pltpu.VMEM((tm, D_FF), BF16),
                            pltpu.SemaphoreType.DMA((2,)), pltpu.SemaphoreType.DMA((2,))]),
        compiler_params=_params("arbitrary"),
        name="moe_experts",
    )(*tables, xs, wg, bg, wu, bu, wd, bd)


def _combine_kernel(slot_hbm, gate_hbm, ys_ref, x2_ref, gfin_ref, outp_ref, outs_ref,
                    slot_smem, gate_smem, acc_ref, isem, *, prompt_tiles):
    i = pl.program_id(0)
    tm = x2_ref.shape[0]
    x2 = x2_ref[...]
    for s in range(ROW_TILES):
        acc_ref[pl.ds(s, tm, stride=ROW_TILES), :] = x2[:, s * LANES:(s + 1) * LANES]
    _to_flat_smem(slot_hbm.at[i], slot_smem, isem.at[0])
    _to_flat_smem(gate_hbm.at[i], gate_smem, isem.at[1])

    def body(t, _):
        rows = pl.ds(pl.multiple_of(t * ROW_TILES, ROW_TILES), ROW_TILES)
        acc = acc_ref[rows, :]
        for k in range(TOP_K):
            src = pl.multiple_of(slot_smem[k * tm + t], ROW_TILES)
            acc = acc + gate_smem[k * tm + t] * ys_ref[pl.ds(src, ROW_TILES), :]
        acc_ref[rows, :] = acc
        return 0

    lax.fori_loop(0, tm, body, 0, unroll=8)
    acc = jnp.concatenate([acc_ref[pl.ds(s, tm, stride=ROW_TILES), :] for s in range(ROW_TILES)],
                          axis=-1)
    y = _rms(acc) * gfin_ref[...]

    @pl.when(i < prompt_tiles)
    def _():
        outp_ref[...] = y

    @pl.when(i >= prompt_tiles)
    def _():
        outs_ref[...] = y


def _moe_combine(slots, gates, ys, x2, gfin, t_prompt):
    t, d = x2.shape
    tm = TOKEN_TILE
    ptiles = t_prompt // tm
    return pl.pallas_call(
        functools.partial(_combine_kernel, prompt_tiles=ptiles),
        out_shape=(jax.ShapeDtypeStruct((t_prompt, d), F32),
                   jax.ShapeDtypeStruct((t - t_prompt, d), F32)),
        grid=(t // tm,),
        in_specs=[pl.BlockSpec(memory_space=pl.ANY), pl.BlockSpec(memory_space=pl.ANY),
                  pl.BlockSpec((tm * TOP_K * ROW_TILES, LANES), lambda i: (i, 0)),
                  pl.BlockSpec((tm, d), lambda i: (i, 0)),
                  pl.BlockSpec((1, d), lambda i: (0, 0))],
        out_specs=(pl.BlockSpec((tm, d), lambda i: (jnp.minimum(i, ptiles - 1), 0)),
                   pl.BlockSpec((tm, d), lambda i: (jnp.maximum(i - ptiles, 0), 0))),
        scratch_shapes=[pltpu.SMEM((TOP_K * tm,), I32), pltpu.SMEM((TOP_K * tm,), F32),
                        pltpu.VMEM((tm * ROW_TILES, LANES), F32),
                        pltpu.SemaphoreType.DMA((2,))],
        compiler_params=_params("arbitrary"),
        name="moe_combine",
    )(slots, gates, ys, x2, gfin)


def _lru_gate_weights(w_a, b_a, w_x, b_x):
    hpb = LRU_BLOCK // LRU_HEAD_DIM
    ncb = D_LRU // LRU_BLOCK

    def blockdiag(w):
        w = w.reshape(ncb, hpb, LRU_HEAD_DIM, LRU_HEAD_DIM)
        eye = jnp.eye(hpb, dtype=w.dtype)
        full = jnp.einsum('nhij,hg->nhigj', w, eye)
        return full.reshape(ncb, LRU_BLOCK, LRU_BLOCK)

    mats = [blockdiag(w_a[0]), blockdiag(w_a[1]), blockdiag(w_x[0]), blockdiag(w_x[1])]
    w = jnp.concatenate(mats, axis=-1).astype(BF16)
    bs = [b.reshape(ncb, 1, LRU_BLOCK) for b in (b_a[0], b_a[1], b_x[0], b_x[1])]
    return w, jnp.concatenate(bs, axis=-1).astype(F32)


def _encoder_layer(xp, xs, mem, norm_mix, w_in, conv_w, conv_b, lru_w_a, lru_b_a, lru_w_x, lru_b_x,
                   lru_lambda, norm_lru_out, norm_fft_out, w_out, norm_xattn, norm_mem, w_q, w_kv,
                   w_o, norm_ffn, w_router, b_router, w_gate, b_gate, w_up, b_up, w_down, b_down,
                   norm_final):
    nbp, seq, d = xp.shape
    nb = nbp + xs.shape[0]
    t = nb * seq
    seqz = seq // DFT_RADIX * DFT_PITCH
    ncb = D_LRU // LRU_BLOCK
    row = lambda v: v.reshape(1, -1).astype(F32)

    dft_c, f1, m2 = _dft_tables(seq)
    kt, v = _kv_proj(mem, row(norm_mem), w_kv[:, :d].T.astype(BF16), w_kv[:, d:].astype(BF16))

    xrec, gate, zr, zi = _in_proj(xp.reshape(nbp * seq, d), xs.reshape(t - nbp * seq, d),
                                  row(norm_mix), w_in.astype(BF16), dft_c.astype(BF16))

    w_gates, b_gates = _lru_gate_weights(lru_w_a, lru_b_a, lru_w_x, lru_b_x)
    cw = conv_w.reshape(CONV_WIDTH, ncb, LRU_BLOCK).transpose(1, 0, 2)
    cb = conv_b.reshape(ncb, 1, LRU_BLOCK)
    lam = lru_lambda.reshape(2, ncb, LRU_BLOCK).transpose(1, 0, 2)
    ylru = _lru_mix(xrec.reshape(nb, seq, D_LRU), gate.reshape(nb, seq, D_LRU), cw, cb,
                    w_gates, b_gates, lam)

    yfft = _seq_dft(zr.reshape(nb, seqz, D_FFT), zi.reshape(nb, seqz, D_FFT),
                    f1.astype(BF16), m2.astype(BF16))

    w_r = jnp.pad(w_router.astype(F32), ((0, 0), (0, LANES - N_EXPERTS)))
    w_r_hi = w_r.astype(BF16)
    w_r_lo = (w_r - w_r_hi.astype(F32)).astype(BF16)
    b_r = jnp.pad(b_router.astype(F32).reshape(1, -1), ((0, 0), (0, LANES - N_EXPERTS)))
    x2, xs_rows, slots, gates, tile_cnt = _attn_route(
        xp, xs, ylru, yfft, kt, v, row(norm_lru_out), row(norm_fft_out), w_out.astype(BF16),
        row(norm_xattn), w_q.astype(BF16), w_o.astype(BF16), row(norm_ffn), w_r_hi, w_r_lo, b_r)

    tmb = EXPERT_TILE
    nblocks = -(-(t * TOP_K) // tmb) + N_EXPERTS
    tile_cnt = tile_cnt.reshape(-1, N_EXPERTS)
    cum_end = jnp.cumsum(tile_cnt, axis=0)
    cum = cum_end - tile_cnt
    off = jnp.cumsum(tile_cnt, axis=1) - tile_cnt
    counts = cum_end[-1]
    padded = ((counts + tmb - 1) // tmb) * tmb
    pend = jnp.cumsum(padded)
    pstart = pend - padded
    blk_start = jnp.arange(nblocks, dtype=I32) * tmb
    bexp = jnp.minimum(jnp.sum((pend[None, :] <= blk_start[:, None]).astype(I32), axis=1),
                       N_EXPERTS - 1)
    ostart = blk_start - pstart[bexp]
    bvalid = jnp.clip(counts[bexp] - ostart, 0, tmb).astype(I32)
    nblk = (pend[-1:] // tmb).astype(I32)
    cum_b = jnp.take(cum, bexp, axis=1)
    end_b = jnp.take(cum_end, bexp, axis=1)
    ntiles = tile_cnt.shape[0]
    t0 = jnp.minimum(jnp.sum((end_b <= ostart[None, :]).astype(I32), axis=0), ntiles - 1)
    t1 = jnp.maximum(jnp.sum((cum_b < (ostart + tmb)[None, :]).astype(I32), axis=0) - 1, 0)
    tables = tuple(a.astype(I32) for a in (bexp, bvalid, nblk, ostart, t0, t1, cum.reshape(-1),
                                           tile_cnt.reshape(-1), off.reshape(-1)))

    ys = _moe_experts(tables, xs_rows,
                      w_gate.astype(BF16), b_gate.reshape(N_EXPERTS, 1, D_FF),
                      w_up.astype(BF16), b_up.reshape(N_EXPERTS, 1, D_FF),
                      w_down.astype(BF16), b_down.reshape(N_EXPERTS, 1, D_MODEL))
    outp, outs = _moe_combine(slots, gates, ys, x2.reshape(t, d), row(norm_final), nbp * seq)
    return outp.reshape(nbp, seq, d), outs.reshape(nb - nbp, seq, d)


def kernel(x_prompt, x_sample, mem_prompt, mem_sample, norm_mix, w_in, conv_w, conv_b, lru_w_a,
           lru_b_a, lru_w_x, lru_b_x, lru_lambda, norm_lru_out, norm_fft_out, w_out, norm_xattn,
           norm_mem, w_q, w_kv, w_o, norm_ffn, w_router, b_router, w_gate, b_gate, w_up, b_up,
           w_down, b_down, norm_final):
    assert x_prompt.shape[1:] == x_sample.shape[1:], "both groups must share (SEQ, D_MODEL)"
    assert w_in.shape[0] == 1, "single-layer block"
    mem = jnp.concatenate([mem_prompt, mem_sample], axis=0)
    return _encoder_layer(x_prompt, x_sample, mem, norm_mix[0], w_in[0], conv_w[0], conv_b[0],
                          lru_w_a[0], lru_b_a[0], lru_w_x[0], lru_b_x[0], lru_lambda[0],
                          norm_lru_out[0], norm_fft_out[0], w_out[0], norm_xattn[0], norm_mem[0],
                          w_q[0], w_kv[0], w_o[0], norm_ffn[0], w_router[0], b_router[0],
                          w_gate[0], b_gate[0], w_up[0], b_up[0], w_down[0], b_down[0], norm_final)
```

```python
import functools
import math

import numpy as np
import jax
import jax.numpy as jnp
from jax import lax
from jax.experimental import pallas as pl
from jax.experimental.pallas import tpu as pltpu

F32, BF16, I32 = jnp.float32, jnp.bfloat16, jnp.int32

D_MODEL = 1024
D_LRU = 512
LRU_HEAD_DIM = 64
CONV_WIDTH = 4
LRU_C = 8.0
D_FFT = 512
FFT_GROUP_DIM = 128
N_XATTN_HEADS = 4
XATTN_HEAD_DIM = 256
N_EXPERTS = 32
TOP_K = 4
D_FF = 1024
SWIGLU_LIMIT = 7.0
SWIGLU_ALPHA = 1.702
EPS = 1e-6

LANES = 128
SUBLANES = 8
ROW_TILES = D_MODEL // LANES
TOKEN_TILE = 512
EXPERT_TILE = 512
FF_CHUNK = 256
LRU_BLOCK = 128
SCAN_CHUNK = 128
SCAN_PITCH = SCAN_CHUNK + SUBLANES
DFT_RADIX = 64
DFT_PITCH = DFT_RADIX + SUBLANES
DFT_UNROLL = 4
VMEM_LIMIT = 56 * 1024 * 1024


def _params(*sem):
    return pltpu.CompilerParams(dimension_semantics=sem, vmem_limit_bytes=VMEM_LIMIT)


def _rms(xf):
    return xf * lax.rsqrt(jnp.mean(xf * xf, axis=-1, keepdims=True) + EPS)


def _gelu_tanh(x):
    return 0.5 * x * (1.0 + jnp.tanh(math.sqrt(2.0 / math.pi) * (x + 0.044715 * (x * x * x))))


def _kv_kernel(mem_ref, g_ref, wkt_ref, wv_ref, kt_ref, v_ref):
    mn = (_rms(mem_ref[0]) * g_ref[...]).astype(BF16)
    kt = lax.dot_general(wkt_ref[...], mn, (((1,), (1,)), ((), ())), preferred_element_type=F32)
    kt_ref[0] = kt.astype(BF16)
    v_ref[0] = jnp.dot(mn, wv_ref[...], preferred_element_type=F32).astype(BF16)


def _kv_proj(mem, g, wkt, wv):
    nb, m, d = mem.shape
    return pl.pallas_call(
        _kv_kernel,
        out_shape=(jax.ShapeDtypeStruct((nb, d, m), BF16), jax.ShapeDtypeStruct((nb, m, d), BF16)),
        grid=(nb,),
        in_specs=[pl.BlockSpec((1, m, d), lambda b: (b, 0, 0)),
                  pl.BlockSpec((1, d), lambda b: (0, 0)),
                  pl.BlockSpec((d, d), lambda b: (0, 0)),
                  pl.BlockSpec((d, d), lambda b: (0, 0))],
        out_specs=(pl.BlockSpec((1, d, m), lambda b: (b, 0, 0)),
                   pl.BlockSpec((1, m, d), lambda b: (b, 0, 0))),
        compiler_params=_params("arbitrary"),
        name="kv_proj",
    )(mem, g, wkt, wv)


def _inproj_kernel(xp_ref, xs_ref, g_ref, w_ref, dft_ref, xrec_ref, gate_ref, zr_ref, zi_ref,
                   *, prompt_tiles):
    x = jnp.where(pl.program_id(0) < prompt_tiles, xp_ref[...], xs_ref[...])
    h = (_rms(x) * g_ref[...]).astype(BF16)
    proj = jnp.dot(h, w_ref[...], preferred_element_type=F32)
    xrec_ref[...] = proj[:, :D_LRU]
    gate_ref[...] = proj[:, D_LRU:2 * D_LRU].astype(BF16)
    z = jnp.dot(proj[:, 2 * D_LRU:].astype(BF16), dft_ref[...], preferred_element_type=F32)
    pad = jnp.zeros((DFT_PITCH - DFT_RADIX, D_FFT), F32)
    for m in range(x.shape[0] // DFT_RADIX):
        rows = slice(m * DFT_RADIX, (m + 1) * DFT_RADIX)
        dst = slice(m * DFT_PITCH, m * DFT_PITCH + DFT_RADIX)
        gap = slice(m * DFT_PITCH + DFT_RADIX, (m + 1) * DFT_PITCH)
        zr_ref[dst, :] = z[rows, :D_FFT]
        zi_ref[dst, :] = z[rows, D_FFT:]
        zr_ref[gap, :] = pad
        zi_ref[gap, :] = pad


def _in_proj(xp2d, xs2d, g, w_in, dft_c):
    tm = TOKEN_TILE
    ptiles = xp2d.shape[0] // tm
    t = xp2d.shape[0] + xs2d.shape[0]
    tz = t // DFT_RADIX * DFT_PITCH
    tmz = tm // DFT_RADIX * DFT_PITCH
    row = lambda i: (i, 0)
    full = lambda i: (0, 0)
    return pl.pallas_call(
        functools.partial(_inproj_kernel, prompt_tiles=ptiles),
        out_shape=(jax.ShapeDtypeStruct((t, D_LRU), F32), jax.ShapeDtypeStruct((t, D_LRU), BF16),
                   jax.ShapeDtypeStruct((tz, D_FFT), F32), jax.ShapeDtypeStruct((tz, D_FFT), F32)),
        grid=(t // tm,),
        in_specs=[pl.BlockSpec((tm, D_MODEL), lambda i: (jnp.minimum(i, ptiles - 1), 0)),
                  pl.BlockSpec((tm, D_MODEL), lambda i: (jnp.maximum(i - ptiles, 0), 0)),
                  pl.BlockSpec((1, D_MODEL), full),
                  pl.BlockSpec(w_in.shape, full), pl.BlockSpec(dft_c.shape, full)],
        out_specs=(pl.BlockSpec((tm, D_LRU), row), pl.BlockSpec((tm, D_LRU), row),
                   pl.BlockSpec((tmz, D_FFT), row), pl.BlockSpec((tmz, D_FFT), row)),
        compiler_params=_params("arbitrary"),
        name="in_proj",
    )(xp2d, xs2d, g, w_in, dft_c)


def _lru_kernel(x_ref, gate_ref, cw_ref, cb_ref, w_ref, b_ref, lam_ref, y_ref,
                af_ref, uf_ref, ab_ref, ub_ref, cf_ref, cbk_ref, *, seq):
    nch = seq // SCAN_CHUNK
    cw = cw_ref[0]
    cb = cb_ref[0]
    lam = lam_ref[0]
    sp = jnp.maximum(-lam, 0.0) + jnp.log1p(jnp.exp(-jnp.abs(lam)))
    neg_half_c_sp = (-0.5 * LRU_C) * sp
    bias = b_ref[0]
    c = LRU_BLOCK

    def gates_body(j, _):
        r0 = pl.multiple_of(j * SCAN_CHUNK, SCAN_CHUNK)
        main = x_ref[0, pl.ds(r0, SCAN_CHUNK), :]
        prev = x_ref[0, pl.ds(jnp.maximum(r0 - SUBLANES, 0), SUBLANES), :]
        nxt = x_ref[0, pl.ds(jnp.minimum(r0 + SCAN_CHUNK, seq - SUBLANES), SUBLANES), :]
        prev = jnp.where(j > 0, prev, 0.0)
        nxt = jnp.where(j < nch - 1, nxt, 0.0)
        win = jnp.concatenate([prev, main, nxt], axis=0)
        base = SUBLANES - CONV_WIDTH // 2
        xc = cb
        for tap in range(CONV_WIDTH):
            xc = xc + win[base + tap:base + tap + SCAN_CHUNK, :] * cw[tap:tap + 1, :]
        xcb = xc.astype(BF16)
        half_xc = 0.5 * xc
        o0 = pl.multiple_of(j * SCAN_PITCH, SUBLANES)
        for d, (a_ref, u_ref) in enumerate(((af_ref, uf_ref), (ab_ref, ub_ref))):
            ga = jnp.dot(xcb, w_ref[0, :, d * c:(d + 1) * c], preferred_element_type=F32) \
                + bias[:, d * c:(d + 1) * c]
            gx = jnp.dot(xcb, w_ref[0, :, (2 + d) * c:(3 + d) * c], preferred_element_type=F32) \
                + bias[:, (2 + d) * c:(3 + d) * c]
            log_a = neg_half_c_sp[d:d + 1, :] * (1.0 + jnp.tanh(0.5 * ga))
            ix = half_xc + half_xc * jnp.tanh(0.5 * gx)
            a = jnp.exp(log_a)
            u = jnp.sqrt(jnp.tanh(log_a) * (-1.0 - a * a)) * ix
            a_ref[pl.ds(o0, SCAN_CHUNK), :] = a
            u_ref[pl.ds(o0, SCAN_CHUNK), :] = u
        return 0

    lax.fori_loop(0, nch, gates_body, 0, unroll=2)

    def scan_body(t, carry):
        hf, pf, hb, pb = carry
        fwd = pl.ds(t, nch, stride=SCAN_PITCH)
        bwd = pl.ds(SCAN_CHUNK - 1 - t, nch, stride=SCAN_PITCH)
        a = af_ref[fwd, :]
        hf = a * hf + uf_ref[fwd, :]
        pf = a * pf
        uf_ref[fwd, :] = hf
        af_ref[fwd, :] = pf
        a = ab_ref[bwd, :]
        hb = a * hb + ub_ref[bwd, :]
        pb = a * pb
        ub_ref[bwd, :] = hb
        ab_ref[bwd, :] = pb
        return hf, pf, hb, pb

    zeros = jnp.zeros((nch, c), F32)
    ones = jnp.ones((nch, c), F32)
    hf, pf, hb, pb = lax.fori_loop(0, SCAN_CHUNK, scan_body, (zeros, ones, zeros, ones))

    carry = jnp.zeros((1, c), F32)
    cf_ref[0:1, :] = carry
    for j in range(1, nch):
        carry = hf[j - 1:j, :] + pf[j - 1:j, :] * carry
        cf_ref[j:j + 1, :] = carry
    carry = jnp.zeros((1, c), F32)
    cbk_ref[nch - 1:nch, :] = carry
    for j in range(nch - 2, -1, -1):
        carry = hb[j + 1:j + 2, :] + pb[j + 1:j + 2, :] * carry
        cbk_ref[j:j + 1, :] = carry

    def out_body(j, _):
        r0 = pl.multiple_of(j * SCAN_CHUNK, SCAN_CHUNK)
        o0 = pl.multiple_of(j * SCAN_PITCH, SUBLANES)
        rows = pl.ds(o0, SCAN_CHUNK)
        h = (uf_ref[rows, :] + af_ref[rows, :] * cf_ref[pl.ds(j, 1), :]
             + ub_ref[rows, :] + ab_ref[rows, :] * cbk_ref[pl.ds(j, 1), :])
        g = gate_ref[0, pl.ds(r0, SCAN_CHUNK), :].astype(F32)
        y_ref[0, pl.ds(r0, SCAN_CHUNK), :] = (h * _gelu_tanh(g)).astype(BF16)
        return 0

    lax.fori_loop(0, nch, out_body, 0)


def _lru_mix(xrec, gate, conv_w, conv_b, w_gates, b_gates, lam):
    nb, seq, _ = xrec.shape
    ncb = D_LRU // LRU_BLOCK
    c = LRU_BLOCK
    nch = seq // SCAN_CHUNK
    act = lambda b, k: (b, 0, k)
    par = lambda b, k: (k, 0, 0)
    scan_buf = pltpu.VMEM((nch * SCAN_PITCH, c), F32)
    return pl.pallas_call(
        functools.partial(_lru_kernel, seq=seq),
        out_shape=jax.ShapeDtypeStruct((nb, seq, D_LRU), BF16),
        grid=(nb, ncb),
        in_specs=[pl.BlockSpec((1, seq, c), act), pl.BlockSpec((1, seq, c), act),
                  pl.BlockSpec((1, CONV_WIDTH, c), par), pl.BlockSpec((1, 1, c), par),
                  pl.BlockSpec((1, c, 4 * c), par), pl.BlockSpec((1, 1, 4 * c), par),
                  pl.BlockSpec((1, 2, c), par)],
        out_specs=pl.BlockSpec((1, seq, c), act),
        scratch_shapes=[scan_buf, scan_buf, scan_buf, scan_buf,
                        pltpu.VMEM((nch, c), F32), pltpu.VMEM((nch, c), F32)],
        compiler_params=_params("arbitrary", "arbitrary"),
        name="lru_mix",
    )(xrec, gate, conv_w, conv_b, w_gates, b_gates, lam)


def _seqdft_kernel(zr_ref, zi_ref, f1_ref, m2_ref, y_ref, ar_ref, ai_ref):
    r, pitch, unroll = DFT_RADIX, DFT_PITCH, DFT_UNROLL
    f1 = f1_ref[...]

    def stage1(it, _):
        i2 = it * unroll
        slabs = []
        for q in range(unroll):
            rows = pl.ds(i2 + q, r, stride=pitch)
            slabs.append(jnp.concatenate([zr_ref[0, rows, :], zi_ref[0, rows, :]], axis=0))
        rhs = jnp.concatenate(slabs, axis=1).astype(BF16)
        a = jnp.dot(f1, rhs, preferred_element_type=F32)
        for q in range(unroll):
            rows = pl.ds(i2 + q, r, stride=pitch)
            ar_ref[rows, :] = a[:r, q * LANES:(q + 1) * LANES]
            ai_ref[rows, :] = a[r:, q * LANES:(q + 1) * LANES]
        return 0

    lax.fori_loop(0, r // unroll, stage1, 0)

    def stage2(it, _):
        for q in range(unroll):
            k1 = it * unroll + q
            src = pl.ds(pl.multiple_of(k1 * pitch, SUBLANES), r)
            slab = jnp.concatenate([ar_ref[src, :], ai_ref[src, :]], axis=0).astype(BF16)
            o = jnp.dot(m2_ref[k1], slab, preferred_element_type=F32)
            y_ref[0, pl.ds(k1, r, stride=pitch), :] = o
        return 0

    lax.fori_loop(0, r // unroll, stage2, 0)
    for gap in range(r, pitch):
        y_ref[0, pl.ds(gap, r, stride=pitch), :] = jnp.zeros((r, LANES), F32)


def _seq_dft(zr, zi, f1, m2):
    nb, rows, _ = zr.shape
    cw = LANES
    act = lambda b, k: (b, 0, k)
    return pl.pallas_call(
        _seqdft_kernel,
        out_shape=jax.ShapeDtypeStruct((nb, rows, D_FFT), F32),
        grid=(nb, D_FFT // cw),
        in_specs=[pl.BlockSpec((1, rows, cw), act), pl.BlockSpec((1, rows, cw), act),
                  pl.BlockSpec(f1.shape, lambda b, k: (0, 0)),
                  pl.BlockSpec(m2.shape, lambda b, k: (0, 0, 0))],
        out_specs=pl.BlockSpec((1, rows, cw), act),
        scratch_shapes=[pltpu.VMEM((rows, cw), F32), pltpu.VMEM((rows, cw), F32)],
        compiler_params=_params("arbitrary", "arbitrary"),
        name="seq_dft",
    )(zr, zi, f1, m2)


def _dft_tables(seq):
    n1 = n2 = DFT_RADIX
    assert n1 * n2 == seq, "sequence DFT is factored as DFT_RADIX x DFT_RADIX"
    g = FFT_GROUP_DIM
    ang = 2.0 * np.pi * np.outer(np.arange(g), np.arange(g)) / g
    cg, sg = np.cos(ang) / math.sqrt(g), np.sin(ang) / math.sqrt(g)
    ngroups = D_FFT // g
    dft_c = np.zeros((D_FFT, 2 * D_FFT))
    for q in range(ngroups):
        dft_c[q * g:(q + 1) * g, q * g:(q + 1) * g] = cg
        dft_c[q * g:(q + 1) * g, D_FFT + q * g:D_FFT + (q + 1) * g] = -sg
    ang1 = 2.0 * np.pi * np.outer(np.arange(n1), np.arange(n1)) / n1
    c1, s1 = np.cos(ang1) / math.sqrt(n1), np.sin(ang1) / math.sqrt(n1)
    f1 = np.block([[c1, s1], [-s1, c1]])
    k1 = np.arange(n1)[:, None, None]
    k2 = np.arange(n2)[None, :, None]
    i2 = np.arange(n2)[None, None, :]
    ang2 = 2.0 * np.pi * (i2 * k2 / n2 + i2 * k1 / seq)
    m2 = np.concatenate([np.cos(ang2), np.sin(ang2)], axis=2) / math.sqrt(n2)
    return (jnp.asarray(dft_c, F32), jnp.asarray(f1, F32), jnp.asarray(m2, F32))


def _to_flat_smem(src_ref, smem_ref, sem):
    nk, nc, nl = src_ref.shape
    copies = [pltpu.make_async_copy(src_ref.at[k, c], smem_ref.at[pl.ds((k * nc + c) * nl, nl)], sem)
              for k in range(nk) for c in range(nc)]
    for cp in copies:
        cp.start()
    for cp in copies:
        cp.wait()


def _attn_route_kernel(xp_ref, xs_ref, ylru_ref, yfft_ref, kt_ref, v_ref, gl_ref, gf_ref, wout_ref,
                       gx_ref, wq_ref, wo_ref, gffn_ref, wrh_ref, wrl_ref, br_ref,
                       x2_ref, xs_out_ref, slot_ref, gate_ref, cnt_ref, h3tl_ref, slot_smem, ssem,
                       *, prompt_batches):
    tm = xp_ref.shape[1]
    x = jnp.where(pl.program_id(0) < prompt_batches, xp_ref[0], xs_ref[0])
    m_lru = _rms(ylru_ref[0].astype(F32)) * gl_ref[...]
    yfft = jnp.concatenate([yfft_ref[0, m * DFT_PITCH:m * DFT_PITCH + DFT_RADIX, :]
                            for m in range(tm // DFT_RADIX)], axis=0)
    m_fft = _rms(yfft) * gf_ref[...]
    mixed = jnp.concatenate([m_lru, m_fft], axis=-1).astype(BF16)
    x1 = x + jnp.dot(mixed, wout_ref[...], preferred_element_type=F32)

    hq = (_rms(x1) * gx_ref[...]).astype(BF16)
    q = jnp.dot(hq, wq_ref[...], preferred_element_type=F32) * (XATTN_HEAD_DIM ** -0.5)
    qb = q.astype(BF16)
    heads = []
    for h in range(N_XATTN_HEADS):
        sl = slice(h * XATTN_HEAD_DIM, (h + 1) * XATTN_HEAD_DIM)
        s = jnp.dot(qb[:, sl], kt_ref[0, sl, :], preferred_element_type=F32)
        p = jnp.exp(s - jnp.max(s, axis=-1, keepdims=True))
        inv = 1.0 / jnp.sum(p, axis=-1, keepdims=True)
        o = jnp.dot(p.astype(BF16), v_ref[0, :, sl], preferred_element_type=F32) * inv
        heads.append(o.astype(BF16))
    att = jnp.concatenate(heads, axis=-1)
    x2 = x1 + jnp.dot(att, wo_ref[...], preferred_element_type=F32)
    x2_ref[0] = x2

    h3 = _rms(x2) * gffn_ref[...]
    for s in range(ROW_TILES):
        h3tl_ref[pl.ds(s, tm, stride=ROW_TILES), :] = h3[:, s * LANES:(s + 1) * LANES]

    h_hi = h3.astype(BF16)
    h_lo = (h3 - h_hi.astype(F32)).astype(BF16)
    logits = (jnp.dot(h_hi, wrh_ref[...], preferred_element_type=F32)
              + jnp.dot(h_lo, wrh_ref[...], preferred_element_type=F32)
              + jnp.dot(h_hi, wrl_ref[...], preferred_element_type=F32)) + br_ref[...]
    lg = logits.T[:N_EXPERTS, :]
    e_iota = lax.broadcasted_iota(I32, lg.shape, 0)
    vals, onehots = [], []
    for k in range(TOP_K):
        m = jnp.max(lg, axis=0, keepdims=True)
        idx = jnp.min(jnp.where(lg == m, e_iota, N_EXPERTS), axis=0, keepdims=True)
        oh = e_iota == idx
        vals.append(m)
        onehots.append(oh)
        lg = jnp.where(oh, -jnp.inf, lg)
    exps = [jnp.exp(v - vals[0]) for v in vals]
    inv = 1.0 / (exps[0] + exps[1] + exps[2] + exps[3])

    def lane_tiles(rows):
        return jnp.stack([jnp.concatenate([r[:, c * LANES:(c + 1) * LANES]
                                           for c in range(tm // LANES)], axis=0) for r in rows])

    gate_ref[0] = lane_tiles([e * inv for e in exps])

    member = jnp.zeros(lg.shape, F32)
    for oh in onehots:
        member = member + oh.astype(F32)
    mb = member.astype(BF16)
    r_i = lax.broadcasted_iota(I32, (tm, tm), 0)
    c_i = lax.broadcasted_iota(I32, (tm, tm), 1)
    earlier = jnp.where(r_i < c_i, 1.0, 0.0).astype(BF16)
    tok_rank = jnp.dot(mb, earlier, preferred_element_type=F32)
    er = lax.broadcasted_iota(I32, (N_EXPERTS, N_EXPERTS), 0)
    ec = lax.broadcasted_iota(I32, (N_EXPERTS, N_EXPERTS), 1)
    lower = jnp.where(ec < er, 1.0, 0.0).astype(BF16)
    exp_off = jnp.sum(jnp.dot(lower, mb, preferred_element_type=F32), axis=1, keepdims=True)
    pos = exp_off + tok_rank
    slots = [jnp.sum(jnp.where(oh, pos, 0.0), axis=0, keepdims=True) for oh in onehots]
    slot_ref[0] = lane_tiles(slots).astype(I32) * ROW_TILES
    ones = jnp.ones((SUBLANES, tm), BF16)
    cnt = lax.dot_general(ones, mb, (((1,), (1,)), ((), ())), preferred_element_type=F32)
    cnt_ref[0] = cnt[0:1, :].astype(I32)

    _to_flat_smem(slot_ref.at[0], slot_smem, ssem)

    def place(t, _):
        row = h3tl_ref[pl.ds(pl.multiple_of(t * ROW_TILES, ROW_TILES), ROW_TILES), :]
        for k in range(TOP_K):
            dst = pl.multiple_of(slot_smem[k * tm + t], ROW_TILES)
            xs_out_ref[pl.ds(dst, ROW_TILES), :] = row
        return 0

    lax.fori_loop(0, tm, place, 0, unroll=8)


def _attn_route(xp, xs, ylru, yfft, kt, v, gl, gf, w_out, gx, wq, wo, gffn, wr_hi, wr_lo, br):
    nbp, seq, d = xp.shape
    nb = nbp + xs.shape[0]
    tm = TOKEN_TILE
    tmz = tm // DFT_RADIX * DFT_PITCH
    nt = seq // tm
    m = v.shape[1]
    tok = lambda b, i: (b, i, 0)
    full2 = lambda b, i: (0, 0)
    per_b = lambda b, i: (b, 0, 0)
    tiles = nb * nt
    tile_id = lambda b, i: (b * nt + i, 0, 0)
    tile_id4 = lambda b, i: (b * nt + i, 0, 0, 0)
    nw = wr_hi.shape[1]
    per_tok = (TOP_K, tm // LANES, LANES)
    return pl.pallas_call(
        functools.partial(_attn_route_kernel, prompt_batches=nbp),
        out_shape=(jax.ShapeDtypeStruct((nb, seq, d), F32),
                   jax.ShapeDtypeStruct((tiles * tm * TOP_K * ROW_TILES, LANES), F32),
                   jax.ShapeDtypeStruct((tiles,) + per_tok, I32),
                   jax.ShapeDtypeStruct((tiles,) + per_tok, F32),
                   jax.ShapeDtypeStruct((tiles, 1, N_EXPERTS), I32)),
        grid=(nb, nt),
        in_specs=[pl.BlockSpec((1, tm, d), lambda b, i: (jnp.minimum(b, nbp - 1), i, 0)),
                  pl.BlockSpec((1, tm, d), lambda b, i: (jnp.maximum(b - nbp, 0), i, 0)),
                  pl.BlockSpec((1, tm, D_LRU), tok),
                  pl.BlockSpec((1, tmz, D_FFT), tok),
                  pl.BlockSpec((1, d, m), per_b), pl.BlockSpec((1, m, d), per_b),
                  pl.BlockSpec((1, D_LRU), full2), pl.BlockSpec((1, D_FFT), full2),
                  pl.BlockSpec((d, d), full2), pl.BlockSpec((1, d), full2),
                  pl.BlockSpec((d, d), full2), pl.BlockSpec((d, d), full2),
                  pl.BlockSpec((1, d), full2), pl.BlockSpec((d, nw), full2),
                  pl.BlockSpec((d, nw), full2), pl.BlockSpec((1, nw), full2)],
        out_specs=(pl.BlockSpec((1, tm, d), tok),
                   pl.BlockSpec((tm * TOP_K * ROW_TILES, LANES), lambda b, i: (b * nt + i, 0)),
                   pl.BlockSpec((1,) + per_tok, tile_id4), pl.BlockSpec((1,) + per_tok, tile_id4),
                   pl.BlockSpec((1, 1, N_EXPERTS), tile_id)),
        scratch_shapes=[pltpu.VMEM((tm * ROW_TILES, LANES), F32),
                        pltpu.SMEM((TOP_K * tm,), I32), pltpu.SemaphoreType.DMA],
        compiler_params=_params("arbitrary", "arbitrary"),
        name="attn_route",
    )(xp, xs, ylru, yfft, kt, v, gl, gf, w_out, gx, wq, wo, gffn, wr_hi, wr_lo, br)


def _for_each_run(blk, bexp_ref, ostart_ref, t0_ref, t1_ref, cum_ref, cnt_ref, off_ref, fn):
    e = bexp_ref[blk]
    o0 = ostart_ref[blk]

    def body(i, _):
        j = i * N_EXPERTS + e
        c = cum_ref[j]
        lo = jnp.maximum(o0, c)
        hi = jnp.minimum(o0 + EXPERT_TILE, c + cnt_ref[j])

        @pl.when(hi > lo)
        def _():
            fn(i * (TOKEN_TILE * TOP_K) + off_ref[j] + (lo - c), lo - o0, hi - lo)
        return 0

    lax.fori_loop(t0_ref[blk], t1_ref[blk] + 1, body, 0)


def _rows(ref, row, n_rows):
    return ref.at[pl.ds(pl.multiple_of(row * ROW_TILES, ROW_TILES), n_rows * ROW_TILES), :]


def _experts_kernel(bexp_ref, bvalid_ref, nblk_ref, ostart_ref, t0_ref, t1_ref, cum_ref, cnt_ref,
                    off_ref, xs_hbm, wg_ref, bg_ref, wu_ref, bu_ref, wd_ref, bd_ref, ys_hbm,
                    xbuf, ybuf, act_ref, wgb_ref, wub_ref, wdb_ref, gsem, osem):
    b = pl.program_id(0)
    nblk = nblk_ref[0]
    slot = lax.rem(b, 2)
    tabs = (bexp_ref, ostart_ref, t0_ref, t1_ref, cum_ref, cnt_ref, off_ref)

    def gather(blk, sl):
        _for_each_run(blk, *tabs, lambda src, dst, n: pltpu.make_async_copy(
            _rows(xs_hbm, src, n), _rows(xbuf.at[sl], dst, n), gsem.at[sl]).start())

    def wait_in(blk, sl):
        n = bvalid_ref[blk]
        pltpu.make_async_copy(_rows(xs_hbm, 0, n), _rows(xbuf.at[sl], 0, n), gsem.at[sl]).wait()

    def wait_out(blk, sl):
        n = bvalid_ref[blk]
        pltpu.make_async_copy(_rows(ybuf.at[sl], 0, n), _rows(ys_hbm, 0, n), osem.at[sl]).wait()

    @pl.when(b == 0)
    def _():
        gather(0, 0)

    @pl.when(b + 1 < nblk)
    def _():
        gather(b + 1, 1 - slot)

    @pl.when(b < nblk)
    def _():
        wait_in(b, slot)

        @pl.when(b >= 2)
        def _():
            wait_out(b - 2, slot)

        @pl.when(jnp.logical_or(b == 0, bexp_ref[b] != bexp_ref[jnp.maximum(b - 1, 0)]))
        def _():
            def cast(r, _):
                rows = pl.ds(pl.multiple_of(r * LANES, LANES), LANES)
                wgb_ref[rows, :] = wg_ref[0, rows, :].astype(BF16)
                wub_ref[rows, :] = wu_ref[0, rows, :].astype(BF16)
                wdb_ref[rows, :] = wd_ref[0, rows, :].astype(BF16)
                return 0

            lax.fori_loop(0, D_MODEL // LANES, cast, 0)

        tm = EXPERT_TILE
        x = jnp.concatenate([xbuf[slot, pl.ds(s, tm, stride=ROW_TILES), :]
                             for s in range(ROW_TILES)], axis=-1)
        row = lax.broadcasted_iota(I32, (tm, 1), 0)
        xb = jnp.where(row < bvalid_ref[b], x, 0.0).astype(BF16)
        for cidx in range(D_FF // FF_CHUNK):
            sl = slice(cidx * FF_CHUNK, (cidx + 1) * FF_CHUNK)
            gt = jnp.dot(xb, wgb_ref[:, sl], preferred_element_type=F32) + bg_ref[0, :, sl]
            up = jnp.dot(xb, wub_ref[:, sl], preferred_element_type=F32) + bu_ref[0, :, sl]
            gt = jnp.minimum(gt, SWIGLU_LIMIT)
            up = jnp.clip(up, -SWIGLU_LIMIT, SWIGLU_LIMIT)
            hg = 0.5 * gt
            act_ref[:, sl] = ((hg + hg * jnp.tanh(SWIGLU_ALPHA * hg)) * (up + 1.0)).astype(BF16)
        for cidx in range(D_MODEL // FF_CHUNK):
            sl = slice(cidx * FF_CHUNK, (cidx + 1) * FF_CHUNK)
            o = jnp.dot(act_ref[...], wdb_ref[:, sl], preferred_element_type=F32) + bd_ref[0, :, sl]
            for q in range(FF_CHUNK // LANES):
                s = cidx * (FF_CHUNK // LANES) + q
                ybuf[slot, pl.ds(s, tm, stride=ROW_TILES), :] = o[:, q * LANES:(q + 1) * LANES]
        _for_each_run(b, *tabs, lambda dst, src, n: pltpu.make_async_copy(
            _rows(ybuf.at[slot], src, n), _rows(ys_hbm, dst, n), osem.at[slot]).start())

        @pl.when(b == nblk - 1)
        def _():
            wait_out(b, slot)

            @pl.when(b >= 1)
            def _():
                wait_out(b - 1, 1 - slot)


def _moe_experts(tables, xs, wg, bg, wu, bu, wd, bd):
    bexp = tables[0]
    nblocks = bexp.shape[0]
    tm = EXPERT_TILE
    wsel = lambda i, be, bv, nb, *_: (be[jnp.minimum(i, nb[0] - 1)], 0, 0)
    wspec = pl.BlockSpec((1, D_MODEL, D_FF), wsel)
    bspec = pl.BlockSpec((1, 1, D_FF), wsel)
    hbm = pl.BlockSpec(memory_space=pl.ANY)
    return pl.pallas_call(
        _experts_kernel,
        out_shape=jax.ShapeDtypeStruct(xs.shape, F32),
        grid_spec=pltpu.PrefetchScalarGridSpec(
            num_scalar_prefetch=len(tables),
            grid=(nblocks,),
            in_specs=[hbm, wspec, bspec, wspec, bspec, wspec, bspec],
            out_specs=hbm,
            scratch_shapes=[pltpu.VMEM((2, tm * ROW_TILES, LANES), F32),
                            pltpu.VMEM((2, tm * ROW_TILES, LANES), F32),
                            pltpu.VMEM((tm, D_FF), BF16),
                            pltpu.VMEM((D_MODEL, D_FF), BF16), pltpu.VMEM((D_MODEL, D_FF), BF16),
                            pltpu.VMEM((D_FF, D_MODEL), BF16),
                            pltpu.SemaphoreType.DMA((2,)), pltpu.SemaphoreType.DMA((2,))]),
        compiler_params=_params("arbitrary"),
        name="moe_experts",
    )(*tables, xs, wg, bg, wu, bu, wd, bd)


def _combine_kernel(slot_hbm, gate_hbm, ys_ref, x2_ref, gfin_ref, outp_ref, outs_ref,
                    slot_smem, gate_smem, acc_ref, isem, *, prompt_tiles):
    i = pl.program_id(0)
    tm = x2_ref.shape[0]
    x2 = x2_ref[...]
    for s in range(ROW_TILES):
        acc_ref[pl.ds(s, tm, stride=ROW_TILES), :] = x2[:, s * LANES:(s + 1) * LANES]
    _to_flat_smem(slot_hbm.at[i], slot_smem, isem.at[0])
    _to_flat_smem(gate_hbm.at[i], gate_smem, isem.at[1])

    def body(t, _):
        rows = pl.ds(pl.multiple_of(t * ROW_TILES, ROW_TILES), ROW_TILES)
        acc = acc_ref[rows, :]
        for k in range(TOP_K):
            src = pl.multiple_of(slot_smem[k * tm + t], ROW_TILES)
            acc = acc + gate_smem[k * tm + t] * ys_ref[pl.ds(src, ROW_TILES), :]
        acc_ref[rows, :] = acc
        return 0

    lax.fori_loop(0, tm, body, 0, unroll=8)
    acc = jnp.concatenate([acc_ref[pl.ds(s, tm, stride=ROW_TILES), :] for s in range(ROW_TILES)],
                          axis=-1)
    y = _rms(acc) * gfin_ref[...]

    @pl.when(i < prompt_tiles)
    def _():
        outp_ref[...] = y

    @pl.when(i >= prompt_tiles)
    def _():
        outs_ref[...] = y


def _moe_combine(slots, gates, ys, x2, gfin, t_prompt):
    t, d = x2.shape
    tm = TOKEN_TILE
    ptiles = t_prompt // tm
    return pl.pallas_call(
        functools.partial(_combine_kernel, prompt_tiles=ptiles),
        out_shape=(jax.ShapeDtypeStruct((t_prompt, d), F32),
                   jax.ShapeDtypeStruct((t - t_prompt, d), F32)),
        grid=(t // tm,),
        in_specs=[pl.BlockSpec(memory_space=pl.ANY), pl.BlockSpec(memory_space=pl.ANY),
                  pl.BlockSpec((tm * TOP_K * ROW_TILES, LANES), lambda i: (i, 0)),
                  pl.BlockSpec((tm, d), lambda i: (i, 0)),
                  pl.BlockSpec((1, d), lambda i: (0, 0))],
        out_specs=(pl.BlockSpec((tm, d), lambda i: (jnp.minimum(i, ptiles - 1), 0)),
                   pl.BlockSpec((tm, d), lambda i: (jnp.maximum(i - ptiles, 0), 0))),
        scratch_shapes=[pltpu.SMEM((TOP_K * tm,), I32), pltpu.SMEM((TOP_K * tm,), F32),
                        pltpu.VMEM((tm * ROW_TILES, LANES), F32),
                        pltpu.SemaphoreType.DMA((2,))],
        compiler_params=_params("arbitrary"),
        name="moe_combine",
    )(slots, gates, ys, x2, gfin)


def _lru_gate_weights(w_a, b_a, w_x, b_x):
    hpb = LRU_BLOCK // LRU_HEAD_DIM
    ncb = D_LRU // LRU_BLOCK

    def blockdiag(w):
        w = w.reshape(ncb, hpb, LRU_HEAD_DIM, LRU_HEAD_DIM)
        eye = jnp.eye(hpb, dtype=w.dtype)
        full = jnp.einsum('nhij,hg->nhigj', w, eye)
        return full.reshape(ncb, LRU_BLOCK, LRU_BLOCK)

    mats = [blockdiag(w_a[0]), blockdiag(w_a[1]), blockdiag(w_x[0]), blockdiag(w_x[1])]
    w = jnp.concatenate(mats, axis=-1).astype(BF16)
    bs = [b.reshape(ncb, 1, LRU_BLOCK) for b in (b_a[0], b_a[1], b_x[0], b_x[1])]
    return w, jnp.concatenate(bs, axis=-1).astype(F32)


def _encoder_layer(xp, xs, mem, norm_mix, w_in, conv_w, conv_b, lru_w_a, lru_b_a, lru_w_x, lru_b_x,
                   lru_lambda, norm_lru_out, norm_fft_out, w_out, norm_xattn, norm_mem, w_q, w_kv,
                   w_o, norm_ffn, w_router, b_router, w_gate, b_gate, w_up, b_up, w_down, b_down,
                   norm_final):
    nbp, seq, d = xp.shape
    nb = nbp + xs.shape[0]
    t = nb * seq
    seqz = seq // DFT_RADIX * DFT_PITCH
    ncb = D_LRU // LRU_BLOCK
    row = lambda v: v.reshape(1, -1).astype(F32)

    dft_c, f1, m2 = _dft_tables(seq)
    kt, v = _kv_proj(mem, row(norm_mem), w_kv[:, :d].T.astype(BF16), w_kv[:, d:].astype(BF16))

    xrec, gate, zr, zi = _in_proj(xp.reshape(nbp * seq, d), xs.reshape(t - nbp * seq, d),
                                  row(norm_mix), w_in.astype(BF16), dft_c.astype(BF16))

    w_gates, b_gates = _lru_gate_weights(lru_w_a, lru_b_a, lru_w_x, lru_b_x)
    cw = conv_w.reshape(CONV_WIDTH, ncb, LRU_BLOCK).transpose(1, 0, 2)
    cb = conv_b.reshape(ncb, 1, LRU_BLOCK)
    lam = lru_lambda.reshape(2, ncb, LRU_BLOCK).transpose(1, 0, 2)
    ylru = _lru_mix(xrec.reshape(nb, seq, D_LRU), gate.reshape(nb, seq, D_LRU), cw, cb,
                    w_gates, b_gates, lam)

    yfft = _seq_dft(zr.reshape(nb, seqz, D_FFT), zi.reshape(nb, seqz, D_FFT),
                    f1.astype(BF16), m2.astype(BF16))

    w_r = jnp.pad(w_router.astype(F32), ((0, 0), (0, LANES - N_EXPERTS)))
    w_r_hi = w_r.astype(BF16)
    w_r_lo = (w_r - w_r_hi.astype(F32)).astype(BF16)
    b_r = jnp.pad(b_router.astype(F32).reshape(1, -1), ((0, 0), (0, LANES - N_EXPERTS)))
    x2, xs_rows, slots, gates, tile_cnt = _attn_route(
        xp, xs, ylru, yfft, kt, v, row(norm_lru_out), row(norm_fft_out), w_out.astype(BF16),
        row(norm_xattn), w_q.astype(BF16), w_o.astype(BF16), row(norm_ffn), w_r_hi, w_r_lo, b_r)

    tmb = EXPERT_TILE
    nblocks = -(-(t * TOP_K) // tmb) + N_EXPERTS
    tile_cnt = tile_cnt.reshape(-1, N_EXPERTS)
    cum_end = jnp.cumsum(tile_cnt, axis=0)
    cum = cum_end - tile_cnt
    off = jnp.cumsum(tile_cnt, axis=1) - tile_cnt
    counts = cum_end[-1]
    padded = ((counts + tmb - 1) // tmb) * tmb
    pend = jnp.cumsum(padded)
    pstart = pend - padded
    blk_start = jnp.arange(nblocks, dtype=I32) * tmb
    bexp = jnp.minimum(jnp.sum((pend[None, :] <= blk_start[:, None]).astype(I32), axis=1),
                       N_EXPERTS - 1)
    ostart = blk_start - pstart[bexp]
    bvalid = jnp.clip(counts[bexp] - ostart, 0, tmb).astype(I32)
    nblk = (pend[-1:] // tmb).astype(I32)
    cum_b = jnp.take(cum, bexp, axis=1)
    end_b = jnp.take(cum_end, bexp, axis=1)
    ntiles = tile_cnt.shape[0]
    t0 = jnp.minimum(jnp.sum((end_b <= ostart[None, :]).astype(I32), axis=0), ntiles - 1)
    t1 = jnp.maximum(jnp.sum((cum_b < (ostart + tmb)[None, :]).astype(I32), axis=0) - 1, 0)
    tables = tuple(a.astype(I32) for a in (bexp, bvalid, nblk, ostart, t0, t1, cum.reshape(-1),
                                           tile_cnt.reshape(-1), off.reshape(-1)))

    ys = _moe_experts(tables, xs_rows,
                      w_gate, b_gate.reshape(N_EXPERTS, 1, D_FF),
                      w_up, b_up.reshape(N_EXPERTS, 1, D_FF),
                      w_down, b_down.reshape(N_EXPERTS, 1, D_MODEL))
    outp, outs = _moe_combine(slots, gates, ys, x2.reshape(t, d), row(norm_final), nbp * seq)
    return outp.reshape(nbp, seq, d), outs.reshape(nb - nbp, seq, d)


def kernel(x_prompt, x_sample, mem_prompt, mem_sample, norm_mix, w_in, conv_w, conv_b, lru_w_a,
           lru_b_a, lru_w_x, lru_b_x, lru_lambda, norm_lru_out, norm_fft_out, w_out, norm_xattn,
           norm_mem, w_q, w_kv, w_o, norm_ffn, w_router, b_router, w_gate, b_gate, w_up, b_up,
           w_down, b_down, norm_final):
    assert x_prompt.shape[1:] == x_sample.shape[1:], "both groups must share (SEQ, D_MODEL)"
    assert w_in.shape[0] == 1, "single-layer block"
    mem = jnp.concatenate([mem_prompt, mem_sample], axis=0)
    return _encoder_layer(x_prompt, x_sample, mem, norm_mix[0], w_in[0], conv_w[0], conv_b[0],
                          lru_w_a[0], lru_b_a[0], lru_w_x[0], lru_b_x[0], lru_lambda[0],
                          norm_lru_out[0], norm_fft_out[0], w_out[0], norm_xattn[0], norm_mem[0],
                          w_q[0], w_kv[0], w_o[0], norm_ffn[0], w_router[0], b_router[0],
                          w_gate[0], b_gate[0], w_up[0], b_up[0], w_down[0], b_down[0], norm_final)
```

```python
import functools
import math

import numpy as np
import jax
import jax.numpy as jnp
from jax import lax
from jax.experimental import pallas as pl
from jax.experimental.pallas import tpu as pltpu

F32, BF16, I32 = jnp.float32, jnp.bfloat16, jnp.int32

D_MODEL = 1024
D_LRU = 512
LRU_HEAD_DIM = 64
CONV_WIDTH = 4
LRU_C = 8.0
D_FFT = 512
FFT_GROUP_DIM = 128
N_XATTN_HEADS = 4
XATTN_HEAD_DIM = 256
N_EXPERTS = 32
TOP_K = 4
D_FF = 1024
SWIGLU_LIMIT = 7.0
SWIGLU_ALPHA = 1.702
EPS = 1e-6

LANES = 128
SUBLANES = 8
ROW_TILES = D_MODEL // LANES
TOKEN_TILE = 512
EXPERT_TILE = 512
FF_CHUNK = 256
LRU_BLOCK = 128
SCAN_CHUNK = 128
SCAN_PITCH = SCAN_CHUNK + SUBLANES
DFT_RADIX = 64
DFT_PITCH = DFT_RADIX + SUBLANES
DFT_UNROLL = 4
VMEM_LIMIT = 56 * 1024 * 1024


def _params(*sem):
    return pltpu.CompilerParams(dimension_semantics=sem, vmem_limit_bytes=VMEM_LIMIT)


def _rms(xf):
    return xf * lax.rsqrt(jnp.mean(xf * xf, axis=-1, keepdims=True) + EPS)


def _gelu_tanh(x):
    return 0.5 * x * (1.0 + jnp.tanh(math.sqrt(2.0 / math.pi) * (x + 0.044715 * (x * x * x))))


def _kv_kernel(mem_ref, g_ref, wkt_ref, wv_ref, kt_ref, v_ref):
    mn = (_rms(mem_ref[0]) * g_ref[...]).astype(BF16)
    kt = lax.dot_general(wkt_ref[...], mn, (((1,), (1,)), ((), ())), preferred_element_type=F32)
    kt_ref[0] = kt.astype(BF16)
    v_ref[0] = jnp.dot(mn, wv_ref[...], preferred_element_type=F32).astype(BF16)


def _kv_proj(mem, g, wkt, wv):
    nb, m, d = mem.shape
    return pl.pallas_call(
        _kv_kernel,
        out_shape=(jax.ShapeDtypeStruct((nb, d, m), BF16), jax.ShapeDtypeStruct((nb, m, d), BF16)),
        grid=(nb,),
        in_specs=[pl.BlockSpec((1, m, d), lambda b: (b, 0, 0)),
                  pl.BlockSpec((1, d), lambda b: (0, 0)),
                  pl.BlockSpec((d, d), lambda b: (0, 0)),
                  pl.BlockSpec((d, d), lambda b: (0, 0))],
        out_specs=(pl.BlockSpec((1, d, m), lambda b: (b, 0, 0)),
                   pl.BlockSpec((1, m, d), lambda b: (b, 0, 0))),
        compiler_params=_params("arbitrary"),
        name="kv_proj",
    )(mem, g, wkt, wv)


def _inproj_kernel(xp_ref, xs_ref, g_ref, w_ref, dft_ref, xrec_ref, gate_ref, zr_ref, zi_ref,
                   *, prompt_tiles):
    x = jnp.where(pl.program_id(0) < prompt_tiles, xp_ref[...], xs_ref[...])
    h = (_rms(x) * g_ref[...]).astype(BF16)
    proj = jnp.dot(h, w_ref[...], preferred_element_type=F32)
    xrec_ref[...] = proj[:, :D_LRU]
    gate_ref[...] = proj[:, D_LRU:2 * D_LRU].astype(BF16)
    z = jnp.dot(proj[:, 2 * D_LRU:].astype(BF16), dft_ref[...], preferred_element_type=F32)
    pad = jnp.zeros((DFT_PITCH - DFT_RADIX, D_FFT), F32)
    for m in range(x.shape[0] // DFT_RADIX):
        rows = slice(m * DFT_RADIX, (m + 1) * DFT_RADIX)
        dst = slice(m * DFT_PITCH, m * DFT_PITCH + DFT_RADIX)
        gap = slice(m * DFT_PITCH + DFT_RADIX, (m + 1) * DFT_PITCH)
        zr_ref[dst, :] = z[rows, :D_FFT]
        zi_ref[dst, :] = z[rows, D_FFT:]
        zr_ref[gap, :] = pad
        zi_ref[gap, :] = pad


def _in_proj(xp2d, xs2d, g, w_in, dft_c):
    tm = TOKEN_TILE
    ptiles = xp2d.shape[0] // tm
    t = xp2d.shape[0] + xs2d.shape[0]
    tz = t // DFT_RADIX * DFT_PITCH
    tmz = tm // DFT_RADIX * DFT_PITCH
    row = lambda i: (i, 0)
    full = lambda i: (0, 0)
    return pl.pallas_call(
        functools.partial(_inproj_kernel, prompt_tiles=ptiles),
        out_shape=(jax.ShapeDtypeStruct((t, D_LRU), F32), jax.ShapeDtypeStruct((t, D_LRU), BF16),
                   jax.ShapeDtypeStruct((tz, D_FFT), F32), jax.ShapeDtypeStruct((tz, D_FFT), F32)),
        grid=(t // tm,),
        in_specs=[pl.BlockSpec((tm, D_MODEL), lambda i: (jnp.minimum(i, ptiles - 1), 0)),
                  pl.BlockSpec((tm, D_MODEL), lambda i: (jnp.maximum(i - ptiles, 0), 0)),
                  pl.BlockSpec((1, D_MODEL), full),
                  pl.BlockSpec(w_in.shape, full), pl.BlockSpec(dft_c.shape, full)],
        out_specs=(pl.BlockSpec((tm, D_LRU), row), pl.BlockSpec((tm, D_LRU), row),
                   pl.BlockSpec((tmz, D_FFT), row), pl.BlockSpec((tmz, D_FFT), row)),
        compiler_params=_params("arbitrary"),
        name="in_proj",
    )(xp2d, xs2d, g, w_in, dft_c)


def _lru_kernel(x_ref, gate_ref, cw_ref, cb_ref, w_ref, b_ref, lam_ref, y_ref,
                af_ref, uf_ref, ab_ref, ub_ref, cf_ref, cbk_ref, *, seq):
    nch = seq // SCAN_CHUNK
    cw = cw_ref[0]
    cb = cb_ref[0]
    lam = lam_ref[0]
    sp = jnp.maximum(-lam, 0.0) + jnp.log1p(jnp.exp(-jnp.abs(lam)))
    neg_half_c_sp = (-0.5 * LRU_C) * sp
    bias = b_ref[0]
    c = LRU_BLOCK

    def gates_body(j, _):
        r0 = pl.multiple_of(j * SCAN_CHUNK, SCAN_CHUNK)
        main = x_ref[0, pl.ds(r0, SCAN_CHUNK), :]
        prev = x_ref[0, pl.ds(jnp.maximum(r0 - SUBLANES, 0), SUBLANES), :]
        nxt = x_ref[0, pl.ds(jnp.minimum(r0 + SCAN_CHUNK, seq - SUBLANES), SUBLANES), :]
        prev = jnp.where(j > 0, prev, 0.0)
        nxt = jnp.where(j < nch - 1, nxt, 0.0)
        win = jnp.concatenate([prev, main, nxt], axis=0)
        base = SUBLANES - CONV_WIDTH // 2
        xc = cb
        for tap in range(CONV_WIDTH):
            xc = xc + win[base + tap:base + tap + SCAN_CHUNK, :] * cw[tap:tap + 1, :]
        xcb = xc.astype(BF16)
        half_xc = 0.5 * xc
        o0 = pl.multiple_of(j * SCAN_PITCH, SUBLANES)
        for d, (a_ref, u_ref) in enumerate(((af_ref, uf_ref), (ab_ref, ub_ref))):
            ga = jnp.dot(xcb, w_ref[0, :, d * c:(d + 1) * c], preferred_element_type=F32) \
                + bias[:, d * c:(d + 1) * c]
            gx = jnp.dot(xcb, w_ref[0, :, (2 + d) * c:(3 + d) * c], preferred_element_type=F32) \
                + bias[:, (2 + d) * c:(3 + d) * c]
            log_a = neg_half_c_sp[d:d + 1, :] * (1.0 + jnp.tanh(0.5 * ga))
            ix = half_xc + half_xc * jnp.tanh(0.5 * gx)
            a = jnp.exp(log_a)
            u = jnp.sqrt(jnp.tanh(log_a) * (-1.0 - a * a)) * ix
            a_ref[pl.ds(o0, SCAN_CHUNK), :] = a
            u_ref[pl.ds(o0, SCAN_CHUNK), :] = u
        return 0

    lax.fori_loop(0, nch, gates_body, 0, unroll=2)

    def scan_body(t, carry):
        hf, pf, hb, pb = carry
        fwd = pl.ds(t, nch, stride=SCAN_PITCH)
        bwd = pl.ds(SCAN_CHUNK - 1 - t, nch, stride=SCAN_PITCH)
        a = af_ref[fwd, :]
        hf = a * hf + uf_ref[fwd, :]
        pf = a * pf
        uf_ref[fwd, :] = hf
        af_ref[fwd, :] = pf
        a = ab_ref[bwd, :]
        hb = a * hb + ub_ref[bwd, :]
        pb = a * pb
        ub_ref[bwd, :] = hb
        ab_ref[bwd, :] = pb
        return hf, pf, hb, pb

    zeros = jnp.zeros((nch, c), F32)
    ones = jnp.ones((nch, c), F32)
    hf, pf, hb, pb = lax.fori_loop(0, SCAN_CHUNK, scan_body, (zeros, ones, zeros, ones), unroll=2)

    carry = jnp.zeros((1, c), F32)
    cf_ref[0:1, :] = carry
    for j in range(1, nch):
        carry = hf[j - 1:j, :] + pf[j - 1:j, :] * carry
        cf_ref[j:j + 1, :] = carry
    carry = jnp.zeros((1, c), F32)
    cbk_ref[nch - 1:nch, :] = carry
    for j in range(nch - 2, -1, -1):
        carry = hb[j + 1:j + 2, :] + pb[j + 1:j + 2, :] * carry
        cbk_ref[j:j + 1, :] = carry

    def out_body(j, _):
        r0 = pl.multiple_of(j * SCAN_CHUNK, SCAN_CHUNK)
        o0 = pl.multiple_of(j * SCAN_PITCH, SUBLANES)
        rows = pl.ds(o0, SCAN_CHUNK)
        h = (uf_ref[rows, :] + af_ref[rows, :] * cf_ref[pl.ds(j, 1), :]
             + ub_ref[rows, :] + ab_ref[rows, :] * cbk_ref[pl.ds(j, 1), :])
        g = gate_ref[0, pl.ds(r0, SCAN_CHUNK), :].astype(F32)
        y_ref[0, pl.ds(r0, SCAN_CHUNK), :] = (h * _gelu_tanh(g)).astype(BF16)
        return 0

    lax.fori_loop(0, nch, out_body, 0, unroll=2)


def _lru_mix(xrec, gate, conv_w, conv_b, w_gates, b_gates, lam):
    nb, seq, _ = xrec.shape
    ncb = D_LRU // LRU_BLOCK
    c = LRU_BLOCK
    nch = seq // SCAN_CHUNK
    act = lambda b, k: (b, 0, k)
    par = lambda b, k: (k, 0, 0)
    scan_buf = pltpu.VMEM((nch * SCAN_PITCH, c), F32)
    return pl.pallas_call(
        functools.partial(_lru_kernel, seq=seq),
        out_shape=jax.ShapeDtypeStruct((nb, seq, D_LRU), BF16),
        grid=(nb, ncb),
        in_specs=[pl.BlockSpec((1, seq, c), act), pl.BlockSpec((1, seq, c), act),
                  pl.BlockSpec((1, CONV_WIDTH, c), par), pl.BlockSpec((1, 1, c), par),
                  pl.BlockSpec((1, c, 4 * c), par), pl.BlockSpec((1, 1, 4 * c), par),
                  pl.BlockSpec((1, 2, c), par)],
        out_specs=pl.BlockSpec((1, seq, c), act),
        scratch_shapes=[scan_buf, scan_buf, scan_buf, scan_buf,
                        pltpu.VMEM((nch, c), F32), pltpu.VMEM((nch, c), F32)],
        compiler_params=_params("arbitrary", "arbitrary"),
        name="lru_mix",
    )(xrec, gate, conv_w, conv_b, w_gates, b_gates, lam)


def _seqdft_kernel(zr_ref, zi_ref, f1_ref, m2_ref, y_ref, ar_ref, ai_ref):
    r, pitch, unroll = DFT_RADIX, DFT_PITCH, DFT_UNROLL
    f1 = f1_ref[...]

    def stage1(it, _):
        i2 = it * unroll
        slabs = []
        for q in range(unroll):
            rows = pl.ds(i2 + q, r, stride=pitch)
            slabs.append(jnp.concatenate([zr_ref[0, rows, :], zi_ref[0, rows, :]], axis=0))
        rhs = jnp.concatenate(slabs, axis=1).astype(BF16)
        a = jnp.dot(f1, rhs, preferred_element_type=F32)
        for q in range(unroll):
            rows = pl.ds(i2 + q, r, stride=pitch)
            ar_ref[rows, :] = a[:r, q * LANES:(q + 1) * LANES]
            ai_ref[rows, :] = a[r:, q * LANES:(q + 1) * LANES]
        return 0

    lax.fori_loop(0, r // unroll, stage1, 0, unroll=2)

    def stage2(it, _):
        for q in range(unroll):
            k1 = it * unroll + q
            src = pl.ds(pl.multiple_of(k1 * pitch, SUBLANES), r)
            slab = jnp.concatenate([ar_ref[src, :], ai_ref[src, :]], axis=0).astype(BF16)
            o = jnp.dot(m2_ref[k1], slab, preferred_element_type=F32)
            y_ref[0, pl.ds(k1, r, stride=pitch), :] = o
        return 0

    lax.fori_loop(0, r // unroll, stage2, 0, unroll=2)
    for gap in range(r, pitch):
        y_ref[0, pl.ds(gap, r, stride=pitch), :] = jnp.zeros((r, LANES), F32)


def _seq_dft(zr, zi, f1, m2):
    nb, rows, _ = zr.shape
    cw = LANES
    act = lambda b, k: (b, 0, k)
    return pl.pallas_call(
        _seqdft_kernel,
        out_shape=jax.ShapeDtypeStruct((nb, rows, D_FFT), F32),
        grid=(nb, D_FFT // cw),
        in_specs=[pl.BlockSpec((1, rows, cw), act), pl.BlockSpec((1, rows, cw), act),
                  pl.BlockSpec(f1.shape, lambda b, k: (0, 0)),
                  pl.BlockSpec(m2.shape, lambda b, k: (0, 0, 0))],
        out_specs=pl.BlockSpec((1, rows, cw), act),
        scratch_shapes=[pltpu.VMEM((rows, cw), F32), pltpu.VMEM((rows, cw), F32)],
        compiler_params=_params("arbitrary", "arbitrary"),
        name="seq_dft",
    )(zr, zi, f1, m2)


def _dft_tables(seq):
    n1 = n2 = DFT_RADIX
    assert n1 * n2 == seq, "sequence DFT is factored as DFT_RADIX x DFT_RADIX"
    g = FFT_GROUP_DIM
    ang = 2.0 * np.pi * np.outer(np.arange(g), np.arange(g)) / g
    cg, sg = np.cos(ang) / math.sqrt(g), np.sin(ang) / math.sqrt(g)
    ngroups = D_FFT // g
    dft_c = np.zeros((D_FFT, 2 * D_FFT))
    for q in range(ngroups):
        dft_c[q * g:(q + 1) * g, q * g:(q + 1) * g] = cg
        dft_c[q * g:(q + 1) * g, D_FFT + q * g:D_FFT + (q + 1) * g] = -sg
    ang1 = 2.0 * np.pi * np.outer(np.arange(n1), np.arange(n1)) / n1
    c1, s1 = np.cos(ang1) / math.sqrt(n1), np.sin(ang1) / math.sqrt(n1)
    f1 = np.block([[c1, s1], [-s1, c1]])
    k1 = np.arange(n1)[:, None, None]
    k2 = np.arange(n2)[None, :, None]
    i2 = np.arange(n2)[None, None, :]
    ang2 = 2.0 * np.pi * (i2 * k2 / n2 + i2 * k1 / seq)
    m2 = np.concatenate([np.cos(ang2), np.sin(ang2)], axis=2) / math.sqrt(n2)
    return (jnp.asarray(dft_c, F32), jnp.asarray(f1, F32), jnp.asarray(m2, F32))


def _flat_smem_copies(src_ref, smem_ref, base, sem):
    nk, nc, nl = src_ref.shape
    return [pltpu.make_async_copy(src_ref.at[k, c],
                                  smem_ref.at[pl.ds(base + (k * nc + c) * nl, nl)], sem)
            for k in range(nk) for c in range(nc)]


def _to_flat_smem(src_ref, smem_ref, sem):
    copies = _flat_smem_copies(src_ref, smem_ref, 0, sem)
    for cp in copies:
        cp.start()
    for cp in copies:
        cp.wait()


def _attn_route_kernel(xp_ref, xs_ref, ylru_ref, yfft_ref, kt_ref, v_ref, gl_ref, gf_ref, wout_ref,
                       gx_ref, wq_ref, wo_ref, gffn_ref, wrh_ref, wrl_ref, br_ref,
                       x2_ref, xs_out_ref, slot_ref, gate_ref, cnt_ref, h3tl_ref, slot_smem, ssem,
                       *, prompt_batches):
    tm = xp_ref.shape[1]
    x = jnp.where(pl.program_id(0) < prompt_batches, xp_ref[0], xs_ref[0])
    m_lru = _rms(ylru_ref[0].astype(F32)) * gl_ref[...]
    yfft = jnp.concatenate([yfft_ref[0, m * DFT_PITCH:m * DFT_PITCH + DFT_RADIX, :]
                            for m in range(tm // DFT_RADIX)], axis=0)
    m_fft = _rms(yfft) * gf_ref[...]
    mixed = jnp.concatenate([m_lru, m_fft], axis=-1).astype(BF16)
    x1 = x + jnp.dot(mixed, wout_ref[...], preferred_element_type=F32)

    hq = (_rms(x1) * gx_ref[...]).astype(BF16)
    q = jnp.dot(hq, wq_ref[...], preferred_element_type=F32) * (XATTN_HEAD_DIM ** -0.5)
    qb = q.astype(BF16)
    heads = []
    for h in range(N_XATTN_HEADS):
        sl = slice(h * XATTN_HEAD_DIM, (h + 1) * XATTN_HEAD_DIM)
        s = jnp.dot(qb[:, sl], kt_ref[0, sl, :], preferred_element_type=F32)
        p = jnp.exp(s - jnp.max(s, axis=-1, keepdims=True))
        inv = 1.0 / jnp.sum(p, axis=-1, keepdims=True)
        o = jnp.dot(p.astype(BF16), v_ref[0, :, sl], preferred_element_type=F32) * inv
        heads.append(o.astype(BF16))
    att = jnp.concatenate(heads, axis=-1)
    x2 = x1 + jnp.dot(att, wo_ref[...], preferred_element_type=F32)
    x2_ref[0] = x2

    h3 = _rms(x2) * gffn_ref[...]
    for s in range(ROW_TILES):
        h3tl_ref[pl.ds(s, tm, stride=ROW_TILES), :] = h3[:, s * LANES:(s + 1) * LANES]

    h_hi = h3.astype(BF16)
    h_lo = (h3 - h_hi.astype(F32)).astype(BF16)
    logits = (jnp.dot(h_hi, wrh_ref[...], preferred_element_type=F32)
              + jnp.dot(h_lo, wrh_ref[...], preferred_element_type=F32)
              + jnp.dot(h_hi, wrl_ref[...], preferred_element_type=F32)) + br_ref[...]
    lg = logits.T[:N_EXPERTS, :]
    e_iota = lax.broadcasted_iota(I32, lg.shape, 0)
    vals, onehots = [], []
    for k in range(TOP_K):
        m = jnp.max(lg, axis=0, keepdims=True)
        idx = jnp.min(jnp.where(lg == m, e_iota, N_EXPERTS), axis=0, keepdims=True)
        oh = e_iota == idx
        vals.append(m)
        onehots.append(oh)
        lg = jnp.where(oh, -jnp.inf, lg)
    exps = [jnp.exp(v - vals[0]) for v in vals]
    inv = 1.0 / (exps[0] + exps[1] + exps[2] + exps[3])

    def lane_tiles(rows):
        return jnp.stack([jnp.concatenate([r[:, c * LANES:(c + 1) * LANES]
                                           for c in range(tm // LANES)], axis=0) for r in rows])

    gate_ref[0] = lane_tiles([e * inv for e in exps])

    member = jnp.zeros(lg.shape, F32)
    for oh in onehots:
        member = member + oh.astype(F32)
    mb = member.astype(BF16)
    r_i = lax.broadcasted_iota(I32, (tm, tm), 0)
    c_i = lax.broadcasted_iota(I32, (tm, tm), 1)
    earlier = jnp.where(r_i < c_i, 1.0, 0.0).astype(BF16)
    tok_rank = jnp.dot(mb, earlier, preferred_element_type=F32)
    er = lax.broadcasted_iota(I32, (N_EXPERTS, N_EXPERTS), 0)
    ec = lax.broadcasted_iota(I32, (N_EXPERTS, N_EXPERTS), 1)
    lower = jnp.where(ec < er, 1.0, 0.0).astype(BF16)
    exp_off = jnp.sum(jnp.dot(lower, mb, preferred_element_type=F32), axis=1, keepdims=True)
    pos = exp_off + tok_rank
    slots = [jnp.sum(jnp.where(oh, pos, 0.0), axis=0, keepdims=True) for oh in onehots]
    slot_ref[0] = lane_tiles(slots).astype(I32) * ROW_TILES
    ones = jnp.ones((SUBLANES, tm), BF16)
    cnt = lax.dot_general(ones, mb, (((1,), (1,)), ((), ())), preferred_element_type=F32)
    cnt_ref[0] = cnt[0:1, :].astype(I32)

    _to_flat_smem(slot_ref.at[0], slot_smem, ssem)

    def place(t, _):
        row = h3tl_ref[pl.ds(pl.multiple_of(t * ROW_TILES, ROW_TILES), ROW_TILES), :]
        for k in range(TOP_K):
            dst = pl.multiple_of(slot_smem[k * tm + t], ROW_TILES)
            xs_out_ref[pl.ds(dst, ROW_TILES), :] = row
        return 0

    lax.fori_loop(0, tm, place, 0, unroll=8)


def _attn_route(xp, xs, ylru, yfft, kt, v, gl, gf, w_out, gx, wq, wo, gffn, wr_hi, wr_lo, br):
    nbp, seq, d = xp.shape
    nb = nbp + xs.shape[0]
    tm = TOKEN_TILE
    tmz = tm // DFT_RADIX * DFT_PITCH
    nt = seq // tm
    m = v.shape[1]
    tok = lambda b, i: (b, i, 0)
    full2 = lambda b, i: (0, 0)
    per_b = lambda b, i: (b, 0, 0)
    tiles = nb * nt
    tile_id = lambda b, i: (b * nt + i, 0, 0)
    tile_id4 = lambda b, i: (b * nt + i, 0, 0, 0)
    nw = wr_hi.shape[1]
    per_tok = (TOP_K, tm // LANES, LANES)
    return pl.pallas_call(
        functools.partial(_attn_route_kernel, prompt_batches=nbp),
        out_shape=(jax.ShapeDtypeStruct((nb, seq, d), F32),
                   jax.ShapeDtypeStruct((tiles * tm * TOP_K * ROW_TILES, LANES), F32),
                   jax.ShapeDtypeStruct((tiles,) + per_tok, I32),
                   jax.ShapeDtypeStruct((tiles,) + per_tok, F32),
                   jax.ShapeDtypeStruct((tiles, 1, N_EXPERTS), I32)),
        grid=(nb, nt),
        in_specs=[pl.BlockSpec((1, tm, d), lambda b, i: (jnp.minimum(b, nbp - 1), i, 0)),
                  pl.BlockSpec((1, tm, d), lambda b, i: (jnp.maximum(b - nbp, 0), i, 0)),
                  pl.BlockSpec((1, tm, D_LRU), tok),
                  pl.BlockSpec((1, tmz, D_FFT), tok),
                  pl.BlockSpec((1, d, m), per_b), pl.BlockSpec((1, m, d), per_b),
                  pl.BlockSpec((1, D_LRU), full2), pl.BlockSpec((1, D_FFT), full2),
                  pl.BlockSpec((d, d), full2), pl.BlockSpec((1, d), full2),
                  pl.BlockSpec((d, d), full2), pl.BlockSpec((d, d), full2),
                  pl.BlockSpec((1, d), full2), pl.BlockSpec((d, nw), full2),
                  pl.BlockSpec((d, nw), full2), pl.BlockSpec((1, nw), full2)],
        out_specs=(pl.BlockSpec((1, tm, d), tok),
                   pl.BlockSpec((tm * TOP_K * ROW_TILES, LANES), lambda b, i: (b * nt + i, 0)),
                   pl.BlockSpec((1,) + per_tok, tile_id4), pl.BlockSpec((1,) + per_tok, tile_id4),
                   pl.BlockSpec((1, 1, N_EXPERTS), tile_id)),
        scratch_shapes=[pltpu.VMEM((tm * ROW_TILES, LANES), F32),
                        pltpu.SMEM((TOP_K * tm,), I32), pltpu.SemaphoreType.DMA],
        compiler_params=_params("arbitrary", "arbitrary"),
        name="attn_route",
    )(xp, xs, ylru, yfft, kt, v, gl, gf, w_out, gx, wq, wo, gffn, wr_hi, wr_lo, br)


def _for_each_run(blk, bexp_ref, ostart_ref, t0_ref, t1_ref, cum_ref, cnt_ref, off_ref, fn):
    e = bexp_ref[blk]
    o0 = ostart_ref[blk]

    def body(i, _):
        j = i * N_EXPERTS + e
        c = cum_ref[j]
        lo = jnp.maximum(o0, c)
        hi = jnp.minimum(o0 + EXPERT_TILE, c + cnt_ref[j])

        @pl.when(hi > lo)
        def _():
            fn(i * (TOKEN_TILE * TOP_K) + off_ref[j] + (lo - c), lo - o0, hi - lo)
        return 0

    lax.fori_loop(t0_ref[blk], t1_ref[blk] + 1, body, 0)


def _rows(ref, row, n_rows):
    return ref.at[pl.ds(pl.multiple_of(row * ROW_TILES, ROW_TILES), n_rows * ROW_TILES), :]


def _experts_kernel(bexp_ref, bvalid_ref, nblk_ref, ostart_ref, t0_ref, t1_ref, cum_ref, cnt_ref,
                    off_ref, xs_hbm, wg_ref, bg_ref, wu_ref, bu_ref, wd_ref, bd_ref, ys_hbm,
                    xbuf, ybuf, act_ref, wgb_ref, wub_ref, wdb_ref, gsem, osem):
    b = pl.program_id(0)
    nblk = nblk_ref[0]
    slot = lax.rem(b, 2)
    tabs = (bexp_ref, ostart_ref, t0_ref, t1_ref, cum_ref, cnt_ref, off_ref)

    def gather(blk, sl):
        _for_each_run(blk, *tabs, lambda src, dst, n: pltpu.make_async_copy(
            _rows(xs_hbm, src, n), _rows(xbuf.at[sl], dst, n), gsem.at[sl]).start())

    def wait_in(blk, sl):
        n = bvalid_ref[blk]
        pltpu.make_async_copy(_rows(xs_hbm, 0, n), _rows(xbuf.at[sl], 0, n), gsem.at[sl]).wait()

    def wait_out(blk, sl):
        n = bvalid_ref[blk]
        pltpu.make_async_copy(_rows(ybuf.at[sl], 0, n), _rows(ys_hbm, 0, n), osem.at[sl]).wait()

    @pl.when(b == 0)
    def _():
        gather(0, 0)

    @pl.when(b + 1 < nblk)
    def _():
        gather(b + 1, 1 - slot)

    @pl.when(b < nblk)
    def _():
        wait_in(b, slot)

        @pl.when(b >= 2)
        def _():
            wait_out(b - 2, slot)

        @pl.when(jnp.logical_or(b == 0, bexp_ref[b] != bexp_ref[jnp.maximum(b - 1, 0)]))
        def _():
            def cast(r, _):
                rows = pl.ds(pl.multiple_of(r * LANES, LANES), LANES)
                wgb_ref[rows, :] = wg_ref[0, rows, :].astype(BF16)
                wub_ref[rows, :] = wu_ref[0, rows, :].astype(BF16)
                wdb_ref[rows, :] = wd_ref[0, rows, :].astype(BF16)
                return 0

            lax.fori_loop(0, D_MODEL // LANES, cast, 0)

        tm = EXPERT_TILE
        x = jnp.concatenate([xbuf[slot, pl.ds(s, tm, stride=ROW_TILES), :]
                             for s in range(ROW_TILES)], axis=-1)
        row = lax.broadcasted_iota(I32, (tm, 1), 0)
        xb = jnp.where(row < bvalid_ref[b], x, 0.0).astype(BF16)
        for cidx in range(D_FF // FF_CHUNK):
            sl = slice(cidx * FF_CHUNK, (cidx + 1) * FF_CHUNK)
            gt = jnp.dot(xb, wgb_ref[:, sl], preferred_element_type=F32) + bg_ref[0, :, sl]
            up = jnp.dot(xb, wub_ref[:, sl], preferred_element_type=F32) + bu_ref[0, :, sl]
            gt = jnp.minimum(gt, SWIGLU_LIMIT)
            up = jnp.clip(up, -SWIGLU_LIMIT, SWIGLU_LIMIT)
            hg = 0.5 * gt
            act_ref[:, sl] = ((hg + hg * jnp.tanh(SWIGLU_ALPHA * hg)) * (up + 1.0)).astype(BF16)
        for cidx in range(D_MODEL // FF_CHUNK):
            sl = slice(cidx * FF_CHUNK, (cidx + 1) * FF_CHUNK)
            o = jnp.dot(act_ref[...], wdb_ref[:, sl], preferred_element_type=F32) + bd_ref[0, :, sl]
            for q in range(FF_CHUNK // LANES):
                s = cidx * (FF_CHUNK // LANES) + q
                ybuf[slot, pl.ds(s, tm, stride=ROW_TILES), :] = o[:, q * LANES:(q + 1) * LANES]
        _for_each_run(b, *tabs, lambda dst, src, n: pltpu.make_async_copy(
            _rows(ybuf.at[slot], src, n), _rows(ys_hbm, dst, n), osem.at[slot]).start())

        @pl.when(b == nblk - 1)
        def _():
            wait_out(b, slot)

            @pl.when(b >= 1)
            def _():
                wait_out(b - 1, 1 - slot)


def _moe_experts(tables, xs, wg, bg, wu, bu, wd, bd):
    bexp = tables[0]
    nblocks = bexp.shape[0]
    tm = EXPERT_TILE
    wsel = lambda i, be, bv, nb, *_: (be[jnp.minimum(i, nb[0] - 1)], 0, 0)
    wspec = pl.BlockSpec((1, D_MODEL, D_FF), wsel)
    bspec = pl.BlockSpec((1, 1, D_FF), wsel)
    hbm = pl.BlockSpec(memory_space=pl.ANY)
    return pl.pallas_call(
        _experts_kernel,
        out_shape=jax.ShapeDtypeStruct(xs.shape, F32),
        grid_spec=pltpu.PrefetchScalarGridSpec(
            num_scalar_prefetch=len(tables),
            grid=(nblocks,),
            in_specs=[hbm, wspec, bspec, wspec, bspec, wspec, bspec],
            out_specs=hbm,
            scratch_shapes=[pltpu.VMEM((2, tm * ROW_TILES, LANES), F32),
                            pltpu.VMEM((2, tm * ROW_TILES, LANES), F32),
                            pltpu.VMEM((tm, D_FF), BF16),
                            pltpu.VMEM((D_MODEL, D_FF), BF16), pltpu.VMEM((D_MODEL, D_FF), BF16),
                            pltpu.VMEM((D_FF, D_MODEL), BF16),
                            pltpu.SemaphoreType.DMA((2,)), pltpu.SemaphoreType.DMA((2,))]),
        compiler_params=_params("arbitrary"),
        name="moe_experts",
    )(*tables, xs, wg, bg, wu, bu, wd, bd)


def _combine_kernel(slot_hbm, gate_hbm, ys_ref, x2_ref, gfin_ref, outp_ref, outs_ref,
                    slot_smem, gate_smem, acc_ref, isem, *, prompt_tiles):
    i = pl.program_id(0)
    tm = x2_ref.shape[0]
    n = TOP_K * tm
    half = lax.rem(i, 2)
    base = half * n

    def table_copies(tile, hf):
        return (_flat_smem_copies(slot_hbm.at[tile], slot_smem, hf * n, isem.at[hf])
                + _flat_smem_copies(gate_hbm.at[tile], gate_smem, hf * n, isem.at[hf]))

    @pl.when(i == 0)
    def _():
        for cp in table_copies(0, 0):
            cp.start()

    @pl.when(i + 1 < pl.num_programs(0))
    def _():
        for cp in table_copies(i + 1, 1 - half):
            cp.start()

    x2 = x2_ref[...]
    for s in range(ROW_TILES):
        acc_ref[pl.ds(s, tm, stride=ROW_TILES), :] = x2[:, s * LANES:(s + 1) * LANES]
    for cp in table_copies(i, half):
        cp.wait()

    def body(t, _):
        rows = pl.ds(pl.multiple_of(t * ROW_TILES, ROW_TILES), ROW_TILES)
        acc = acc_ref[rows, :]
        for k in range(TOP_K):
            src = pl.multiple_of(slot_smem[base + k * tm + t], ROW_TILES)
            acc = acc + gate_smem[base + k * tm + t] * ys_ref[pl.ds(src, ROW_TILES), :]
        acc_ref[rows, :] = acc
        return 0

    lax.fori_loop(0, tm, body, 0, unroll=8)
    acc = jnp.concatenate([acc_ref[pl.ds(s, tm, stride=ROW_TILES), :] for s in range(ROW_TILES)],
                          axis=-1)
    y = _rms(acc) * gfin_ref[...]

    @pl.when(i < prompt_tiles)
    def _():
        outp_ref[...] = y

    @pl.when(i >= prompt_tiles)
    def _():
        outs_ref[...] = y


def _moe_combine(slots, gates, ys, x2, gfin, t_prompt):
    t, d = x2.shape
    tm = TOKEN_TILE
    ptiles = t_prompt // tm
    return pl.pallas_call(
        functools.partial(_combine_kernel, prompt_tiles=ptiles),
        out_shape=(jax.ShapeDtypeStruct((t_prompt, d), F32),
                   jax.ShapeDtypeStruct((t - t_prompt, d), F32)),
        grid=(t // tm,),
        in_specs=[pl.BlockSpec(memory_space=pl.ANY), pl.BlockSpec(memory_space=pl.ANY),
                  pl.BlockSpec((tm * TOP_K * ROW_TILES, LANES), lambda i: (i, 0)),
                  pl.BlockSpec((tm, d), lambda i: (i, 0)),
                  pl.BlockSpec((1, d), lambda i: (0, 0))],
        out_specs=(pl.BlockSpec((tm, d), lambda i: (jnp.minimum(i, ptiles - 1), 0)),
                   pl.BlockSpec((tm, d), lambda i: (jnp.maximum(i - ptiles, 0), 0))),
        scratch_shapes=[pltpu.SMEM((2 * TOP_K * tm,), I32), pltpu.SMEM((2 * TOP_K * tm,), F32),
                        pltpu.VMEM((tm * ROW_TILES, LANES), F32),
                        pltpu.SemaphoreType.DMA((2,))],
        compiler_params=_params("arbitrary"),
        name="moe_combine",
    )(slots, gates, ys, x2, gfin)


def _lru_gate_weights(w_a, b_a, w_x, b_x):
    hpb = LRU_BLOCK // LRU_HEAD_DIM
    ncb = D_LRU // LRU_BLOCK

    def blockdiag(w):
        w = w.reshape(ncb, hpb, LRU_HEAD_DIM, LRU_HEAD_DIM)
        eye = jnp.eye(hpb, dtype=w.dtype)
        full = jnp.einsum('nhij,hg->nhigj', w, eye)
        return full.reshape(ncb, LRU_BLOCK, LRU_BLOCK)

    mats = [blockdiag(w_a[0]), blockdiag(w_a[1]), blockdiag(w_x[0]), blockdiag(w_x[1])]
    w = jnp.concatenate(mats, axis=-1).astype(BF16)
    bs = [b.reshape(ncb, 1, LRU_BLOCK) for b in (b_a[0], b_a[1], b_x[0], b_x[1])]
    return w, jnp.concatenate(bs, axis=-1).astype(F32)


def _encoder_layer(xp, xs, mem, norm_mix, w_in, conv_w, conv_b, lru_w_a, lru_b_a, lru_w_x, lru_b_x,
                   lru_lambda, norm_lru_out, norm_fft_out, w_out, norm_xattn, norm_mem, w_q, w_kv,
                   w_o, norm_ffn, w_router, b_router, w_gate, b_gate, w_up, b_up, w_down, b_down,
                   norm_final):
    nbp, seq, d = xp.shape
    nb = nbp + xs.shape[0]
    t = nb * seq
    seqz = seq // DFT_RADIX * DFT_PITCH
    ncb = D_LRU // LRU_BLOCK
    row = lambda v: v.reshape(1, -1).astype(F32)

    dft_c, f1, m2 = _dft_tables(seq)
    kt, v = _kv_proj(mem, row(norm_mem), w_kv[:, :d].T.astype(BF16), w_kv[:, d:].astype(BF16))

    xrec, gate, zr, zi = _in_proj(xp.reshape(nbp * seq, d), xs.reshape(t - nbp * seq, d),
                                  row(norm_mix), w_in.astype(BF16), dft_c.astype(BF16))

    w_gates, b_gates = _lru_gate_weights(lru_w_a, lru_b_a, lru_w_x, lru_b_x)
    cw = conv_w.reshape(CONV_WIDTH, ncb, LRU_BLOCK).transpose(1, 0, 2)
    cb = conv_b.reshape(ncb, 1, LRU_BLOCK)
    lam = lru_lambda.reshape(2, ncb, LRU_BLOCK).transpose(1, 0, 2)
    ylru = _lru_mix(xrec.reshape(nb, seq, D_LRU), gate.reshape(nb, seq, D_LRU), cw, cb,
                    w_gates, b_gates, lam)

    yfft = _seq_dft(zr.reshape(nb, seqz, D_FFT), zi.reshape(nb, seqz, D_FFT),
                    f1.astype(BF16), m2.astype(BF16))

    w_r = jnp.pad(w_router.astype(F32), ((0, 0), (0, LANES - N_EXPERTS)))
    w_r_hi = w_r.astype(BF16)
    w_r_lo = (w_r - w_r_hi.astype(F32)).astype(BF16)
    b_r = jnp.pad(b_router.astype(F32).reshape(1, -1), ((0, 0), (0, LANES - N_EXPERTS)))
    x2, xs_rows, slots, gates, tile_cnt = _attn_route(
        xp, xs, ylru, yfft, kt, v, row(norm_lru_out), row(norm_fft_out), w_out.astype(BF16),
        row(norm_xattn), w_q.astype(BF16), w_o.astype(BF16), row(norm_ffn), w_r_hi, w_r_lo, b_r)

    tmb = EXPERT_TILE
    nblocks = -(-(t * TOP_K) // tmb) + N_EXPERTS
    tile_cnt = tile_cnt.reshape(-1, N_EXPERTS)
    cum_end = jnp.cumsum(tile_cnt, axis=0)
    cum = cum_end - tile_cnt
    off = jnp.cumsum(tile_cnt, axis=1) - tile_cnt
    counts = cum_end[-1]
    padded = ((counts + tmb - 1) // tmb) * tmb
    pend = jnp.cumsum(padded)
    pstart = pend - padded
    blk_start = jnp.arange(nblocks, dtype=I32) * tmb
    bexp = jnp.minimum(jnp.sum((pend[None, :] <= blk_start[:, None]).astype(I32), axis=1),
                       N_EXPERTS - 1)
    ostart = blk_start - pstart[bexp]
    bvalid = jnp.clip(counts[bexp] - ostart, 0, tmb).astype(I32)
    nblk = (pend[-1:] // tmb).astype(I32)
    cum_b = jnp.take(cum, bexp, axis=1)
    end_b = jnp.take(cum_end, bexp, axis=1)
    ntiles = tile_cnt.shape[0]
    t0 = jnp.minimum(jnp.sum((end_b <= ostart[None, :]).astype(I32), axis=0), ntiles - 1)
    t1 = jnp.maximum(jnp.sum((cum_b < (ostart + tmb)[None, :]).astype(I32), axis=0) - 1, 0)
    tables = tuple(a.astype(I32) for a in (bexp, bvalid, nblk, ostart, t0, t1, cum.reshape(-1),
                                           tile_cnt.reshape(-1), off.reshape(-1)))

    ys = _moe_experts(tables, xs_rows,
                      w_gate, b_gate.reshape(N_EXPERTS, 1, D_FF),
                      w_up, b_up.reshape(N_EXPERTS, 1, D_FF),
                      w_down, b_down.reshape(N_EXPERTS, 1, D_MODEL))
    outp, outs = _moe_combine(slots, gates, ys, x2.reshape(t, d), row(norm_final), nbp * seq)
    return outp.reshape(nbp, seq, d), outs.reshape(nb - nbp, seq, d)


def kernel(x_prompt, x_sample, mem_prompt, mem_sample, norm_mix, w_in, conv_w, conv_b, lru_w_a,
           lru_b_a, lru_w_x, lru_b_x, lru_lambda, norm_lru_out, norm_fft_out, w_out, norm_xattn,
           norm_mem, w_q, w_kv, w_o, norm_ffn, w_router, b_router, w_gate, b_gate, w_up, b_up,
           w_down, b_down, norm_final):
    assert x_prompt.shape[1:] == x_sample.shape[1:], "both groups must share (SEQ, D_MODEL)"
    assert w_in.shape[0] == 1, "single-layer block"
    mem = jnp.concatenate([mem_prompt, mem_sample], axis=0)
    return _encoder_layer(x_prompt, x_sample, mem, norm_mix[0], w_in[0], conv_w[0], conv_b[0],
                          lru_w_a[0], lru_b_a[0], lru_w_x[0], lru_b_x[0], lru_lambda[0],
                          norm_lru_out[0], norm_fft_out[0], w_out[0], norm_xattn[0], norm_mem[0],
                          w_q[0], w_kv[0], w_o[0], norm_ffn[0], w_router[0], b_router[0],
                          w_gate[0], b_gate[0], w_up[0], b_up[0], w_down[0], b_down[0], norm_final)
```

```python
import functools
import math

import numpy as np
import jax
import jax.numpy as jnp
from jax import lax
from jax.experimental import pallas as pl
from jax.experimental.pallas import tpu as pltpu

F32, BF16, I32 = jnp.float32, jnp.bfloat16, jnp.int32

D_MODEL = 1024
D_LRU = 512
LRU_HEAD_DIM = 64
CONV_WIDTH = 4
LRU_C = 8.0
D_FFT = 512
FFT_GROUP_DIM = 128
N_XATTN_HEADS = 4
XATTN_HEAD_DIM = 256
N_EXPERTS = 32
TOP_K = 4
D_FF = 1024
SWIGLU_LIMIT = 7.0
SWIGLU_ALPHA = 1.702
EPS = 1e-6

LANES = 128
SUBLANES = 8
ROW_TILES = D_MODEL // LANES
TOKEN_TILE = 512
EXPERT_TILE = 512
FF_CHUNK = 256
LRU_BLOCK = 128
SCAN_CHUNK = 128
SCAN_PITCH = SCAN_CHUNK + SUBLANES
DFT_RADIX = 64
DFT_PITCH = DFT_RADIX + SUBLANES
DFT_UNROLL = 4
VMEM_LIMIT = 56 * 1024 * 1024


def _params(*sem):
    return pltpu.CompilerParams(dimension_semantics=sem, vmem_limit_bytes=VMEM_LIMIT)


def _rms(xf):
    return xf * lax.rsqrt(jnp.mean(xf * xf, axis=-1, keepdims=True) + EPS)


def _gelu_tanh(x):
    return 0.5 * x * (1.0 + jnp.tanh(math.sqrt(2.0 / math.pi) * (x + 0.044715 * (x * x * x))))


def _kv_kernel(mem_ref, g_ref, wkt_ref, wv_ref, kt_ref, v_ref):
    mn = (_rms(mem_ref[0]) * g_ref[...]).astype(BF16)
    kt = lax.dot_general(wkt_ref[...], mn, (((1,), (1,)), ((), ())), preferred_element_type=F32)
    kt_ref[0] = kt.astype(BF16)
    v_ref[0] = jnp.dot(mn, wv_ref[...], preferred_element_type=F32).astype(BF16)


def _kv_proj(mem, g, wkt, wv):
    nb, m, d = mem.shape
    return pl.pallas_call(
        _kv_kernel,
        out_shape=(jax.ShapeDtypeStruct((nb, d, m), BF16), jax.ShapeDtypeStruct((nb, m, d), BF16)),
        grid=(nb,),
        in_specs=[pl.BlockSpec((1, m, d), lambda b: (b, 0, 0)),
                  pl.BlockSpec((1, d), lambda b: (0, 0)),
                  pl.BlockSpec((d, d), lambda b: (0, 0)),
                  pl.BlockSpec((d, d), lambda b: (0, 0))],
        out_specs=(pl.BlockSpec((1, d, m), lambda b: (b, 0, 0)),
                   pl.BlockSpec((1, m, d), lambda b: (b, 0, 0))),
        compiler_params=_params("arbitrary"),
        name="kv_proj",
    )(mem, g, wkt, wv)


def _inproj_kernel(xp_ref, xs_ref, g_ref, w_ref, dft_ref, xrec_ref, gate_ref, zr_ref, zi_ref,
                   *, prompt_tiles):
    x = jnp.where(pl.program_id(0) < prompt_tiles, xp_ref[...], xs_ref[...])
    h = (_rms(x) * g_ref[...]).astype(BF16)
    proj = jnp.dot(h, w_ref[...], preferred_element_type=F32)
    xrec_ref[...] = proj[:, :D_LRU]
    gate_ref[...] = proj[:, D_LRU:2 * D_LRU].astype(BF16)
    z = jnp.dot(proj[:, 2 * D_LRU:].astype(BF16), dft_ref[...], preferred_element_type=F32)
    pad = jnp.zeros((DFT_PITCH - DFT_RADIX, D_FFT), F32)
    for m in range(x.shape[0] // DFT_RADIX):
        rows = slice(m * DFT_RADIX, (m + 1) * DFT_RADIX)
        dst = slice(m * DFT_PITCH, m * DFT_PITCH + DFT_RADIX)
        gap = slice(m * DFT_PITCH + DFT_RADIX, (m + 1) * DFT_PITCH)
        zr_ref[dst, :] = z[rows, :D_FFT]
        zi_ref[dst, :] = z[rows, D_FFT:]
        zr_ref[gap, :] = pad
        zi_ref[gap, :] = pad


def _in_proj(xp2d, xs2d, g, w_in, dft_c):
    tm = TOKEN_TILE
    ptiles = xp2d.shape[0] // tm
    t = xp2d.shape[0] + xs2d.shape[0]
    tz = t // DFT_RADIX * DFT_PITCH
    tmz = tm // DFT_RADIX * DFT_PITCH
    row = lambda i: (i, 0)
    full = lambda i: (0, 0)
    return pl.pallas_call(
        functools.partial(_inproj_kernel, prompt_tiles=ptiles),
        out_shape=(jax.ShapeDtypeStruct((t, D_LRU), F32), jax.ShapeDtypeStruct((t, D_LRU), BF16),
                   jax.ShapeDtypeStruct((tz, D_FFT), F32), jax.ShapeDtypeStruct((tz, D_FFT), F32)),
        grid=(t // tm,),
        in_specs=[pl.BlockSpec((tm, D_MODEL), lambda i: (jnp.minimum(i, ptiles - 1), 0)),
                  pl.BlockSpec((tm, D_MODEL), lambda i: (jnp.maximum(i - ptiles, 0), 0)),
                  pl.BlockSpec((1, D_MODEL), full),
                  pl.BlockSpec(w_in.shape, full), pl.BlockSpec(dft_c.shape, full)],
        out_specs=(pl.BlockSpec((tm, D_LRU), row), pl.BlockSpec((tm, D_LRU), row),
                   pl.BlockSpec((tmz, D_FFT), row), pl.BlockSpec((tmz, D_FFT), row)),
        compiler_params=_params("arbitrary"),
        name="in_proj",
    )(xp2d, xs2d, g, w_in, dft_c)


def _lru_kernel(x_ref, gate_ref, cw_ref, cb_ref, w_ref, b_ref, lam_ref, y_ref,
                af_ref, uf_ref, ab_ref, ub_ref, cf_ref, cbk_ref, *, seq):
    nch = seq // SCAN_CHUNK
    cw = cw_ref[0]
    cb = cb_ref[0]
    lam = lam_ref[0]
    sp = jnp.maximum(-lam, 0.0) + jnp.log1p(jnp.exp(-jnp.abs(lam)))
    neg_half_c_sp = (-0.5 * LRU_C) * sp
    bias = b_ref[0]
    c = LRU_BLOCK

    def gates_body(j, _):
        r0 = pl.multiple_of(j * SCAN_CHUNK, SCAN_CHUNK)
        main = x_ref[0, pl.ds(r0, SCAN_CHUNK), :]
        prev = x_ref[0, pl.ds(jnp.maximum(r0 - SUBLANES, 0), SUBLANES), :]
        nxt = x_ref[0, pl.ds(jnp.minimum(r0 + SCAN_CHUNK, seq - SUBLANES), SUBLANES), :]
        prev = jnp.where(j > 0, prev, 0.0)
        nxt = jnp.where(j < nch - 1, nxt, 0.0)
        win = jnp.concatenate([prev, main, nxt], axis=0)
        base = SUBLANES - CONV_WIDTH // 2
        xc = cb
        for tap in range(CONV_WIDTH):
            xc = xc + win[base + tap:base + tap + SCAN_CHUNK, :] * cw[tap:tap + 1, :]
        xcb = xc.astype(BF16)
        half_xc = 0.5 * xc
        o0 = pl.multiple_of(j * SCAN_PITCH, SUBLANES)
        for d, (a_ref, u_ref) in enumerate(((af_ref, uf_ref), (ab_ref, ub_ref))):
            ga = jnp.dot(xcb, w_ref[0, :, d * c:(d + 1) * c], preferred_element_type=F32) \
                + bias[:, d * c:(d + 1) * c]
            gx = jnp.dot(xcb, w_ref[0, :, (2 + d) * c:(3 + d) * c], preferred_element_type=F32) \
                + bias[:, (2 + d) * c:(3 + d) * c]
            log_a = neg_half_c_sp[d:d + 1, :] * (1.0 + jnp.tanh(0.5 * ga))
            ix = half_xc + half_xc * jnp.tanh(0.5 * gx)
            a = jnp.exp(log_a)
            u = jnp.sqrt(jnp.tanh(log_a) * (-1.0 - a * a)) * ix
            a_ref[pl.ds(o0, SCAN_CHUNK), :] = a
            u_ref[pl.ds(o0, SCAN_CHUNK), :] = u
        return 0

    lax.fori_loop(0, nch, gates_body, 0, unroll=2)

    def scan_body(t, carry):
        hf, pf, hb, pb = carry
        fwd = pl.ds(t, nch, stride=SCAN_PITCH)
        bwd = pl.ds(SCAN_CHUNK - 1 - t, nch, stride=SCAN_PITCH)
        a = af_ref[fwd, :]
        hf = a * hf + uf_ref[fwd, :]
        pf = a * pf
        uf_ref[fwd, :] = hf
        af_ref[fwd, :] = pf
        a = ab_ref[bwd, :]
        hb = a * hb + ub_ref[bwd, :]
        pb = a * pb
        ub_ref[bwd, :] = hb
        ab_ref[bwd, :] = pb
        return hf, pf, hb, pb

    zeros = jnp.zeros((nch, c), F32)
    ones = jnp.ones((nch, c), F32)
    hf, pf, hb, pb = lax.fori_loop(0, SCAN_CHUNK, scan_body, (zeros, ones, zeros, ones), unroll=2)

    carry = jnp.zeros((1, c), F32)
    cf_ref[0:1, :] = carry
    for j in range(1, nch):
        carry = hf[j - 1:j, :] + pf[j - 1:j, :] * carry
        cf_ref[j:j + 1, :] = carry
    carry = jnp.zeros((1, c), F32)
    cbk_ref[nch - 1:nch, :] = carry
    for j in range(nch - 2, -1, -1):
        carry = hb[j + 1:j + 2, :] + pb[j + 1:j + 2, :] * carry
        cbk_ref[j:j + 1, :] = carry

    def out_body(j, _):
        r0 = pl.multiple_of(j * SCAN_CHUNK, SCAN_CHUNK)
        o0 = pl.multiple_of(j * SCAN_PITCH, SUBLANES)
        rows = pl.ds(o0, SCAN_CHUNK)
        h = (uf_ref[rows, :] + af_ref[rows, :] * cf_ref[pl.ds(j, 1), :]
             + ub_ref[rows, :] + ab_ref[rows, :] * cbk_ref[pl.ds(j, 1), :])
        g = gate_ref[0, pl.ds(r0, SCAN_CHUNK), :].astype(F32)
        y_ref[0, pl.ds(r0, SCAN_CHUNK), :] = (h * _gelu_tanh(g)).astype(BF16)
        return 0

    lax.fori_loop(0, nch, out_body, 0, unroll=2)


def _lru_mix(xrec, gate, conv_w, conv_b, w_gates, b_gates, lam):
    nb, seq, _ = xrec.shape
    ncb = D_LRU // LRU_BLOCK
    c = LRU_BLOCK
    nch = seq // SCAN_CHUNK
    act = lambda b, k: (b, 0, k)
    par = lambda b, k: (k, 0, 0)
    scan_buf = pltpu.VMEM((nch * SCAN_PITCH, c), F32)
    return pl.pallas_call(
        functools.partial(_lru_kernel, seq=seq),
        out_shape=jax.ShapeDtypeStruct((nb, seq, D_LRU), BF16),
        grid=(nb, ncb),
        in_specs=[pl.BlockSpec((1, seq, c), act), pl.BlockSpec((1, seq, c), act),
                  pl.BlockSpec((1, CONV_WIDTH, c), par), pl.BlockSpec((1, 1, c), par),
                  pl.BlockSpec((1, c, 4 * c), par), pl.BlockSpec((1, 1, 4 * c), par),
                  pl.BlockSpec((1, 2, c), par)],
        out_specs=pl.BlockSpec((1, seq, c), act),
        scratch_shapes=[scan_buf, scan_buf, scan_buf, scan_buf,
                        pltpu.VMEM((nch, c), F32), pltpu.VMEM((nch, c), F32)],
        compiler_params=_params("arbitrary", "arbitrary"),
        name="lru_mix",
    )(xrec, gate, conv_w, conv_b, w_gates, b_gates, lam)


def _seqdft_kernel(zr_ref, zi_ref, f1_ref, m2_ref, y_ref, ar_ref, ai_ref):
    r, pitch, unroll = DFT_RADIX, DFT_PITCH, DFT_UNROLL
    f1 = f1_ref[...]

    def stage1(it, _):
        i2 = it * unroll
        slabs = []
        for q in range(unroll):
            rows = pl.ds(i2 + q, r, stride=pitch)
            slabs.append(jnp.concatenate([zr_ref[0, rows, :], zi_ref[0, rows, :]], axis=0))
        rhs = jnp.concatenate(slabs, axis=1).astype(BF16)
        a = jnp.dot(f1, rhs, preferred_element_type=F32)
        for q in range(unroll):
            rows = pl.ds(i2 + q, r, stride=pitch)
            ar_ref[rows, :] = a[:r, q * LANES:(q + 1) * LANES]
            ai_ref[rows, :] = a[r:, q * LANES:(q + 1) * LANES]
        return 0

    lax.fori_loop(0, r // unroll, stage1, 0, unroll=2)

    def stage2(it, _):
        for q in range(unroll):
            k1 = it * unroll + q
            src = pl.ds(pl.multiple_of(k1 * pitch, SUBLANES), r)
            slab = jnp.concatenate([ar_ref[src, :], ai_ref[src, :]], axis=0).astype(BF16)
            o = jnp.dot(m2_ref[k1], slab, preferred_element_type=F32)
            y_ref[0, pl.ds(k1, r, stride=pitch), :] = o
        return 0

    lax.fori_loop(0, r // unroll, stage2, 0, unroll=2)
    for gap in range(r, pitch):
        y_ref[0, pl.ds(gap, r, stride=pitch), :] = jnp.zeros((r, LANES), F32)


def _seq_dft(zr, zi, f1, m2):
    nb, rows, _ = zr.shape
    cw = LANES
    act = lambda b, k: (b, 0, k)
    return pl.pallas_call(
        _seqdft_kernel,
        out_shape=jax.ShapeDtypeStruct((nb, rows, D_FFT), F32),
        grid=(nb, D_FFT // cw),
        in_specs=[pl.BlockSpec((1, rows, cw), act), pl.BlockSpec((1, rows, cw), act),
                  pl.BlockSpec(f1.shape, lambda b, k: (0, 0)),
                  pl.BlockSpec(m2.shape, lambda b, k: (0, 0, 0))],
        out_specs=pl.BlockSpec((1, rows, cw), act),
        scratch_shapes=[pltpu.VMEM((rows, cw), F32), pltpu.VMEM((rows, cw), F32)],
        compiler_params=_params("arbitrary", "arbitrary"),
        name="seq_dft",
    )(zr, zi, f1, m2)


def _dft_tables(seq):
    n1 = n2 = DFT_RADIX
    assert n1 * n2 == seq, "sequence DFT is factored as DFT_RADIX x DFT_RADIX"
    g = FFT_GROUP_DIM
    ang = 2.0 * np.pi * np.outer(np.arange(g), np.arange(g)) / g
    cg, sg = np.cos(ang) / math.sqrt(g), np.sin(ang) / math.sqrt(g)
    ngroups = D_FFT // g
    dft_c = np.zeros((D_FFT, 2 * D_FFT))
    for q in range(ngroups):
        dft_c[q * g:(q + 1) * g, q * g:(q + 1) * g] = cg
        dft_c[q * g:(q + 1) * g, D_FFT + q * g:D_FFT + (q + 1) * g] = -sg
    ang1 = 2.0 * np.pi * np.outer(np.arange(n1), np.arange(n1)) / n1
    c1, s1 = np.cos(ang1) / math.sqrt(n1), np.sin(ang1) / math.sqrt(n1)
    f1 = np.block([[c1, s1], [-s1, c1]])
    k1 = np.arange(n1)[:, None, None]
    k2 = np.arange(n2)[None, :, None]
    i2 = np.arange(n2)[None, None, :]
    ang2 = 2.0 * np.pi * (i2 * k2 / n2 + i2 * k1 / seq)
    m2 = np.concatenate([np.cos(ang2), np.sin(ang2)], axis=2) / math.sqrt(n2)
    return (jnp.asarray(dft_c, F32), jnp.asarray(f1, F32), jnp.asarray(m2, F32))


def _flat_smem_copies(src_ref, smem_ref, base, sem):
    nk, nc, nl = src_ref.shape
    return [pltpu.make_async_copy(src_ref.at[k, c],
                                  smem_ref.at[pl.ds(base + (k * nc + c) * nl, nl)], sem)
            for k in range(nk) for c in range(nc)]


def _attn_route_kernel(xp_ref, xs_ref, ylru_ref, yfft_ref, kt_ref, v_ref, gl_ref, gf_ref, wout_ref,
                       gx_ref, wq_ref, wo_ref, gffn_ref, wrh_ref, wrl_ref, br_ref,
                       x2_ref, xs_out_ref, slot_ref, gate_ref, cnt_ref, h3tl_ref, slot_smem, ssem,
                       *, prompt_tiles, tiles):
    g = pl.program_id(0)
    tm = xp_ref.shape[1]
    n = TOP_K * tm
    cur = lax.rem(g, 2)
    prev = 1 - cur

    @pl.when(g == 0)
    def _():
        h3tl_ref[1] = jnp.zeros(h3tl_ref.shape[1:], F32)

        def clear(j, _):
            slot_smem[n + j] = 0
            return 0

        lax.fori_loop(0, n, clear, 0)

    @pl.when(g >= 1)
    def _():
        for cp in _flat_smem_copies(slot_ref.at[0], slot_smem, prev * n, ssem.at[prev]):
            cp.wait()

    for t in range(tm):
        row = h3tl_ref[prev, t * ROW_TILES:(t + 1) * ROW_TILES, :]
        for k in range(TOP_K):
            dst = pl.multiple_of(slot_smem[prev * n + k * tm + t], ROW_TILES)
            xs_out_ref[pl.ds(dst, ROW_TILES), :] = row

    x = jnp.where(g < prompt_tiles, xp_ref[0], xs_ref[0])
    m_lru = _rms(ylru_ref[0].astype(F32)) * gl_ref[...]
    yfft = jnp.concatenate([yfft_ref[0, m * DFT_PITCH:m * DFT_PITCH + DFT_RADIX, :]
                            for m in range(tm // DFT_RADIX)], axis=0)
    m_fft = _rms(yfft) * gf_ref[...]
    mixed = jnp.concatenate([m_lru, m_fft], axis=-1).astype(BF16)
    x1 = x + jnp.dot(mixed, wout_ref[...], preferred_element_type=F32)

    hq = (_rms(x1) * gx_ref[...]).astype(BF16)
    q = jnp.dot(hq, wq_ref[...], preferred_element_type=F32) * (XATTN_HEAD_DIM ** -0.5)
    qb = q.astype(BF16)
    heads = []
    for h in range(N_XATTN_HEADS):
        sl = slice(h * XATTN_HEAD_DIM, (h + 1) * XATTN_HEAD_DIM)
        s = jnp.dot(qb[:, sl], kt_ref[0, sl, :], preferred_element_type=F32)
        p = jnp.exp(s - jnp.max(s, axis=-1, keepdims=True))
        inv = 1.0 / jnp.sum(p, axis=-1, keepdims=True)
        o = jnp.dot(p.astype(BF16), v_ref[0, :, sl], preferred_element_type=F32) * inv
        heads.append(o.astype(BF16))
    att = jnp.concatenate(heads, axis=-1)
    x2 = x1 + jnp.dot(att, wo_ref[...], preferred_element_type=F32)
    x2_ref[0] = x2

    h3 = _rms(x2) * gffn_ref[...]
    for s in range(ROW_TILES):
        h3tl_ref[cur, pl.ds(s, tm, stride=ROW_TILES), :] = h3[:, s * LANES:(s + 1) * LANES]

    h_hi = h3.astype(BF16)
    h_lo = (h3 - h_hi.astype(F32)).astype(BF16)
    logits = (jnp.dot(h_hi, wrh_ref[...], preferred_element_type=F32)
              + jnp.dot(h_lo, wrh_ref[...], preferred_element_type=F32)
              + jnp.dot(h_hi, wrl_ref[...], preferred_element_type=F32)) + br_ref[...]
    lg = logits.T[:N_EXPERTS, :]
    e_iota = lax.broadcasted_iota(I32, lg.shape, 0)
    vals, onehots = [], []
    for k in range(TOP_K):
        m = jnp.max(lg, axis=0, keepdims=True)
        idx = jnp.min(jnp.where(lg == m, e_iota, N_EXPERTS), axis=0, keepdims=True)
        oh = e_iota == idx
        vals.append(m)
        onehots.append(oh)
        lg = jnp.where(oh, -jnp.inf, lg)
    exps = [jnp.exp(v - vals[0]) for v in vals]
    inv = 1.0 / (exps[0] + exps[1] + exps[2] + exps[3])

    def lane_tiles(rows):
        return jnp.stack([jnp.concatenate([r[:, c * LANES:(c + 1) * LANES]
                                           for c in range(tm // LANES)], axis=0) for r in rows])

    gate_ref[0] = lane_tiles([e * inv for e in exps])

    member = jnp.zeros(lg.shape, F32)
    for oh in onehots:
        member = member + oh.astype(F32)
    mb = member.astype(BF16)
    r_i = lax.broadcasted_iota(I32, (tm, tm), 0)
    c_i = lax.broadcasted_iota(I32, (tm, tm), 1)
    earlier = jnp.where(r_i < c_i, 1.0, 0.0).astype(BF16)
    tok_rank = jnp.dot(mb, earlier, preferred_element_type=F32)
    er = lax.broadcasted_iota(I32, (N_EXPERTS, N_EXPERTS), 0)
    ec = lax.broadcasted_iota(I32, (N_EXPERTS, N_EXPERTS), 1)
    lower = jnp.where(ec < er, 1.0, 0.0).astype(BF16)
    exp_off = jnp.sum(jnp.dot(lower, mb, preferred_element_type=F32), axis=1, keepdims=True)
    pos = exp_off + tok_rank
    slots = [jnp.sum(jnp.where(oh, pos, 0.0), axis=0, keepdims=True) for oh in onehots]
    slot_ref[0] = lane_tiles(slots).astype(I32) * ROW_TILES
    ones = jnp.ones((SUBLANES, tm), BF16)
    cnt = lax.dot_general(ones, mb, (((1,), (1,)), ((), ())), preferred_element_type=F32)
    cnt_ref[0] = cnt[0:1, :].astype(I32)

    @pl.when(g < tiles)
    def _():
        for cp in _flat_smem_copies(slot_ref.at[0], slot_smem, cur * n, ssem.at[cur]):
            cp.start()


def _attn_route(xp, xs, ylru, yfft, kt, v, gl, gf, w_out, gx, wq, wo, gffn, wr_hi, wr_lo, br):
    nbp, seq, d = xp.shape
    nb = nbp + xs.shape[0]
    tm = TOKEN_TILE
    tmz = tm // DFT_RADIX * DFT_PITCH
    nt = seq // tm
    m = v.shape[1]
    tiles = nb * nt
    ptiles = nbp * nt
    tile = lambda g: jnp.minimum(g, tiles - 1)
    tok = lambda g: (tile(g) // nt, tile(g) % nt, 0)
    full2 = lambda g: (0, 0)
    per_b = lambda g: (tile(g) // nt, 0, 0)
    tile_id = lambda g: (tile(g), 0, 0)
    tile_id4 = lambda g: (tile(g), 0, 0, 0)
    xp_idx = lambda g: (jnp.minimum(g, ptiles - 1) // nt, jnp.minimum(g, ptiles - 1) % nt, 0)
    xs_idx = lambda g: (jnp.maximum(tile(g) - ptiles, 0) // nt, jnp.maximum(tile(g) - ptiles, 0) % nt, 0)
    nw = wr_hi.shape[1]
    per_tok = (TOP_K, tm // LANES, LANES)
    return pl.pallas_call(
        functools.partial(_attn_route_kernel, prompt_tiles=ptiles, tiles=tiles),
        out_shape=(jax.ShapeDtypeStruct((nb, seq, d), F32),
                   jax.ShapeDtypeStruct((tiles * tm * TOP_K * ROW_TILES, LANES), F32),
                   jax.ShapeDtypeStruct((tiles,) + per_tok, I32),
                   jax.ShapeDtypeStruct((tiles,) + per_tok, F32),
                   jax.ShapeDtypeStruct((tiles, 1, N_EXPERTS), I32)),
        grid=(tiles + 1,),
        in_specs=[pl.BlockSpec((1, tm, d), xp_idx),
                  pl.BlockSpec((1, tm, d), xs_idx),
                  pl.BlockSpec((1, tm, D_LRU), tok),
                  pl.BlockSpec((1, tmz, D_FFT), tok),
                  pl.BlockSpec((1, d, m), per_b), pl.BlockSpec((1, m, d), per_b),
                  pl.BlockSpec((1, D_LRU), full2), pl.BlockSpec((1, D_FFT), full2),
                  pl.BlockSpec((d, d), full2), pl.BlockSpec((1, d), full2),
                  pl.BlockSpec((d, d), full2), pl.BlockSpec((d, d), full2),
                  pl.BlockSpec((1, d), full2), pl.BlockSpec((d, nw), full2),
                  pl.BlockSpec((d, nw), full2), pl.BlockSpec((1, nw), full2)],
        out_specs=(pl.BlockSpec((1, tm, d), tok),
                   pl.BlockSpec((tm * TOP_K * ROW_TILES, LANES), lambda g: (jnp.maximum(g - 1, 0), 0)),
                   pl.BlockSpec((1,) + per_tok, tile_id4), pl.BlockSpec((1,) + per_tok, tile_id4),
                   pl.BlockSpec((1, 1, N_EXPERTS), tile_id)),
        scratch_shapes=[pltpu.VMEM((2, tm * ROW_TILES, LANES), F32),
                        pltpu.SMEM((2 * TOP_K * tm,), I32), pltpu.SemaphoreType.DMA((2,))],
        compiler_params=_params("arbitrary"),
        name="attn_route",
    )(xp, xs, ylru, yfft, kt, v, gl, gf, w_out, gx, wq, wo, gffn, wr_hi, wr_lo, br)


def _for_each_run(blk, bexp_ref, ostart_ref, t0_ref, t1_ref, cum_ref, cnt_ref, off_ref, fn):
    e = bexp_ref[blk]
    o0 = ostart_ref[blk]

    def body(i, _):
        j = i * N_EXPERTS + e
        c = cum_ref[j]
        lo = jnp.maximum(o0, c)
        hi = jnp.minimum(o0 + EXPERT_TILE, c + cnt_ref[j])

        @pl.when(hi > lo)
        def _():
            fn(i * (TOKEN_TILE * TOP_K) + off_ref[j] + (lo - c), lo - o0, hi - lo)
        return 0

    lax.fori_loop(t0_ref[blk], t1_ref[blk] + 1, body, 0)


def _rows(ref, row, n_rows):
    return ref.at[pl.ds(pl.multiple_of(row * ROW_TILES, ROW_TILES), n_rows * ROW_TILES), :]


def _experts_kernel(bexp_ref, bvalid_ref, nblk_ref, ostart_ref, t0_ref, t1_ref, cum_ref, cnt_ref,
                    off_ref, xs_hbm, wg_ref, bg_ref, wu_ref, bu_ref, wd_ref, bd_ref, ys_hbm,
                    xbuf, ybuf, act_ref, wgb_ref, wub_ref, wdb_ref, gsem, osem):
    b = pl.program_id(0)
    nblk = nblk_ref[0]
    slot = lax.rem(b, 2)
    tabs = (bexp_ref, ostart_ref, t0_ref, t1_ref, cum_ref, cnt_ref, off_ref)

    def gather(blk, sl):
        _for_each_run(blk, *tabs, lambda src, dst, n: pltpu.make_async_copy(
            _rows(xs_hbm, src, n), _rows(xbuf.at[sl], dst, n), gsem.at[sl]).start())

    def wait_in(blk, sl):
        n = bvalid_ref[blk]
        pltpu.make_async_copy(_rows(xs_hbm, 0, n), _rows(xbuf.at[sl], 0, n), gsem.at[sl]).wait()

    def wait_out(blk, sl):
        n = bvalid_ref[blk]
        pltpu.make_async_copy(_rows(ybuf.at[sl], 0, n), _rows(ys_hbm, 0, n), osem.at[sl]).wait()

    @pl.when(b == 0)
    def _():
        gather(0, 0)

    @pl.when(b + 1 < nblk)
    def _():
        gather(b + 1, 1 - slot)

    @pl.when(b < nblk)
    def _():
        wait_in(b, slot)

        @pl.when(b >= 2)
        def _():
            wait_out(b - 2, slot)

        @pl.when(jnp.logical_or(b == 0, bexp_ref[b] != bexp_ref[jnp.maximum(b - 1, 0)]))
        def _():
            def cast(r, _):
                rows = pl.ds(pl.multiple_of(r * LANES, LANES), LANES)
                wgb_ref[rows, :] = wg_ref[0, rows, :].astype(BF16)
                wub_ref[rows, :] = wu_ref[0, rows, :].astype(BF16)
                wdb_ref[rows, :] = wd_ref[0, rows, :].astype(BF16)
                return 0

            lax.fori_loop(0, D_MODEL // LANES, cast, 0)

        tm = EXPERT_TILE
        x = jnp.concatenate([xbuf[slot, pl.ds(s, tm, stride=ROW_TILES), :]
                             for s in range(ROW_TILES)], axis=-1)
        row = lax.broadcasted_iota(I32, (tm, 1), 0)
        xb = jnp.where(row < bvalid_ref[b], x, 0.0).astype(BF16)
        for cidx in range(D_FF // FF_CHUNK):
            sl = slice(cidx * FF_CHUNK, (cidx + 1) * FF_CHUNK)
            gt = jnp.dot(xb, wgb_ref[:, sl], preferred_element_type=F32) + bg_ref[0, :, sl]
            up = jnp.dot(xb, wub_ref[:, sl], preferred_element_type=F32) + bu_ref[0, :, sl]
            gt = jnp.minimum(gt, SWIGLU_LIMIT)
            up = jnp.clip(up, -SWIGLU_LIMIT, SWIGLU_LIMIT)
            hg = 0.5 * gt
            act_ref[:, sl] = ((hg + hg * jnp.tanh(SWIGLU_ALPHA * hg)) * (up + 1.0)).astype(BF16)
        for cidx in range(D_MODEL // FF_CHUNK):
            sl = slice(cidx * FF_CHUNK, (cidx + 1) * FF_CHUNK)
            o = jnp.dot(act_ref[...], wdb_ref[:, sl], preferred_element_type=F32) + bd_ref[0, :, sl]
            for q in range(FF_CHUNK // LANES):
                s = cidx * (FF_CHUNK // LANES) + q
                ybuf[slot, pl.ds(s, tm, stride=ROW_TILES), :] = o[:, q * LANES:(q + 1) * LANES]
        _for_each_run(b, *tabs, lambda dst, src, n: pltpu.make_async_copy(
            _rows(ybuf.at[slot], src, n), _rows(ys_hbm, dst, n), osem.at[slot]).start())

        @pl.when(b == nblk - 1)
        def _():
            wait_out(b, slot)

            @pl.when(b >= 1)
            def _():
                wait_out(b - 1, 1 - slot)


def _moe_experts(tables, xs, wg, bg, wu, bu, wd, bd):
    bexp = tables[0]
    nblocks = bexp.shape[0]
    tm = EXPERT_TILE
    wsel = lambda i, be, bv, nb, *_: (be[jnp.minimum(i, nb[0] - 1)], 0, 0)
    wspec = pl.BlockSpec((1, D_MODEL, D_FF), wsel)
    bspec = pl.BlockSpec((1, 1, D_FF), wsel)
    hbm = pl.BlockSpec(memory_space=pl.ANY)
    return pl.pallas_call(
        _experts_kernel,
        out_shape=jax.ShapeDtypeStruct(xs.shape, F32),
        grid_spec=pltpu.PrefetchScalarGridSpec(
            num_scalar_prefetch=len(tables),
            grid=(nblocks,),
            in_specs=[hbm, wspec, bspec, wspec, bspec, wspec, bspec],
            out_specs=hbm,
            scratch_shapes=[pltpu.VMEM((2, tm * ROW_TILES, LANES), F32),
                            pltpu.VMEM((2, tm * ROW_TILES, LANES), F32),
                            pltpu.VMEM((tm, D_FF), BF16),
                            pltpu.VMEM((D_MODEL, D_FF), BF16), pltpu.VMEM((D_MODEL, D_FF), BF16),
                            pltpu.VMEM((D_FF, D_MODEL), BF16),
                            pltpu.SemaphoreType.DMA((2,)), pltpu.SemaphoreType.DMA((2,))]),
        compiler_params=_params("arbitrary"),
        name="moe_experts",
    )(*tables, xs, wg, bg, wu, bu, wd, bd)


def _combine_kernel(slot_hbm, gate_hbm, ys_ref, x2_ref, gfin_ref, outp_ref, outs_ref,
                    slot_smem, gate_smem, acc_ref, isem, *, prompt_tiles):
    i = pl.program_id(0)
    tm = x2_ref.shape[0]
    n = TOP_K * tm
    half = lax.rem(i, 2)
    base = half * n

    def table_copies(tile, hf):
        return (_flat_smem_copies(slot_hbm.at[tile], slot_smem, hf * n, isem.at[hf])
                + _flat_smem_copies(gate_hbm.at[tile], gate_smem, hf * n, isem.at[hf]))

    @pl.when(i == 0)
    def _():
        for cp in table_copies(0, 0):
            cp.start()

    @pl.when(i + 1 < pl.num_programs(0))
    def _():
        for cp in table_copies(i + 1, 1 - half):
            cp.start()

    x2 = x2_ref[...]
    for s in range(ROW_TILES):
        acc_ref[pl.ds(s, tm, stride=ROW_TILES), :] = x2[:, s * LANES:(s + 1) * LANES]
    for cp in table_copies(i, half):
        cp.wait()

    def body(t, _):
        rows = pl.ds(pl.multiple_of(t * ROW_TILES, ROW_TILES), ROW_TILES)
        acc = acc_ref[rows, :]
        for k in range(TOP_K):
            src = pl.multiple_of(slot_smem[base + k * tm + t], ROW_TILES)
            acc = acc + gate_smem[base + k * tm + t] * ys_ref[pl.ds(src, ROW_TILES), :]
        acc_ref[rows, :] = acc
        return 0

    lax.fori_loop(0, tm, body, 0, unroll=8)
    acc = jnp.concatenate([acc_ref[pl.ds(s, tm, stride=ROW_TILES), :] for s in range(ROW_TILES)],
                          axis=-1)
    y = _rms(acc) * gfin_ref[...]

    @pl.when(i < prompt_tiles)
    def _():
        outp_ref[...] = y

    @pl.when(i >= prompt_tiles)
    def _():
        outs_ref[...] = y


def _moe_combine(slots, gates, ys, x2, gfin, t_prompt):
    t, d = x2.shape
    tm = TOKEN_TILE
    ptiles = t_prompt // tm
    return pl.pallas_call(
        functools.partial(_combine_kernel, prompt_tiles=ptiles),
        out_shape=(jax.ShapeDtypeStruct((t_prompt, d), F32),
                   jax.ShapeDtypeStruct((t - t_prompt, d), F32)),
        grid=(t // tm,),
        in_specs=[pl.BlockSpec(memory_space=pl.ANY), pl.BlockSpec(memory_space=pl.ANY),
                  pl.BlockSpec((tm * TOP_K * ROW_TILES, LANES), lambda i: (i, 0)),
                  pl.BlockSpec((tm, d), lambda i: (i, 0)),
                  pl.BlockSpec((1, d), lambda i: (0, 0))],
        out_specs=(pl.BlockSpec((tm, d), lambda i: (jnp.minimum(i, ptiles - 1), 0)),
                   pl.BlockSpec((tm, d), lambda i: (jnp.maximum(i - ptiles, 0), 0))),
        scratch_shapes=[pltpu.SMEM((2 * TOP_K * tm,), I32), pltpu.SMEM((2 * TOP_K * tm,), F32),
                        pltpu.VMEM((tm * ROW_TILES, LANES), F32),
                        pltpu.SemaphoreType.DMA((2,))],
        compiler_params=_params("arbitrary"),
        name="moe_combine",
    )(slots, gates, ys, x2, gfin)


def _lru_gate_weights(w_a, b_a, w_x, b_x):
    hpb = LRU_BLOCK // LRU_HEAD_DIM
    ncb = D_LRU // LRU_BLOCK

    def blockdiag(w):
        w = w.reshape(ncb, hpb, LRU_HEAD_DIM, LRU_HEAD_DIM)
        eye = jnp.eye(hpb, dtype=w.dtype)
        full = jnp.einsum('nhij,hg->nhigj', w, eye)
        return full.reshape(ncb, LRU_BLOCK, LRU_BLOCK)

    mats = [blockdiag(w_a[0]), blockdiag(w_a[1]), blockdiag(w_x[0]), blockdiag(w_x[1])]
    w = jnp.concatenate(mats, axis=-1).astype(BF16)
    bs = [b.reshape(ncb, 1, LRU_BLOCK) for b in (b_a[0], b_a[1], b_x[0], b_x[1])]
    return w, jnp.concatenate(bs, axis=-1).astype(F32)


def _encoder_layer(xp, xs, mem, norm_mix, w_in, conv_w, conv_b, lru_w_a, lru_b_a, lru_w_x, lru_b_x,
                   lru_lambda, norm_lru_out, norm_fft_out, w_out, norm_xattn, norm_mem, w_q, w_kv,
                   w_o, norm_ffn, w_router, b_router, w_gate, b_gate, w_up, b_up, w_down, b_down,
                   norm_final):
    nbp, seq, d = xp.shape
    nb = nbp + xs.shape[0]
    t = nb * seq
    seqz = seq // DFT_RADIX * DFT_PITCH
    ncb = D_LRU // LRU_BLOCK
    row = lambda v: v.reshape(1, -1).astype(F32)

    dft_c, f1, m2 = _dft_tables(seq)
    kt, v = _kv_proj(mem, row(norm_mem), w_kv[:, :d].T.astype(BF16), w_kv[:, d:].astype(BF16))

    xrec, gate, zr, zi = _in_proj(xp.reshape(nbp * seq, d), xs.reshape(t - nbp * seq, d),
                                  row(norm_mix), w_in.astype(BF16), dft_c.astype(BF16))

    w_gates, b_gates = _lru_gate_weights(lru_w_a, lru_b_a, lru_w_x, lru_b_x)
    cw = conv_w.reshape(CONV_WIDTH, ncb, LRU_BLOCK).transpose(1, 0, 2)
    cb = conv_b.reshape(ncb, 1, LRU_BLOCK)
    lam = lru_lambda.reshape(2, ncb, LRU_BLOCK).transpose(1, 0, 2)
    ylru = _lru_mix(xrec.reshape(nb, seq, D_LRU), gate.reshape(nb, seq, D_LRU), cw, cb,
                    w_gates, b_gates, lam)

    yfft = _seq_dft(zr.reshape(nb, seqz, D_FFT), zi.reshape(nb, seqz, D_FFT),
                    f1.astype(BF16), m2.astype(BF16))

    w_r = jnp.pad(w_router.astype(F32), ((0, 0), (0, LANES - N_EXPERTS)))
    w_r_hi = w_r.astype(BF16)
    w_r_lo = (w_r - w_r_hi.astype(F32)).astype(BF16)
    b_r = jnp.pad(b_router.astype(F32).reshape(1, -1), ((0, 0), (0, LANES - N_EXPERTS)))
    x2, xs_rows, slots, gates, tile_cnt = _attn_route(
        xp, xs, ylru, yfft, kt, v, row(norm_lru_out), row(norm_fft_out), w_out.astype(BF16),
        row(norm_xattn), w_q.astype(BF16), w_o.astype(BF16), row(norm_ffn), w_r_hi, w_r_lo, b_r)

    tmb = EXPERT_TILE
    nblocks = -(-(t * TOP_K) // tmb) + N_EXPERTS
    tile_cnt = tile_cnt.reshape(-1, N_EXPERTS)
    cum_end = jnp.cumsum(tile_cnt, axis=0)
    cum = cum_end - tile_cnt
    off = jnp.cumsum(tile_cnt, axis=1) - tile_cnt
    counts = cum_end[-1]
    padded = ((counts + tmb - 1) // tmb) * tmb
    pend = jnp.cumsum(padded)
    pstart = pend - padded
    blk_start = jnp.arange(nblocks, dtype=I32) * tmb
    bexp = jnp.minimum(jnp.sum((pend[None, :] <= blk_start[:, None]).astype(I32), axis=1),
                       N_EXPERTS - 1)
    ostart = blk_start - pstart[bexp]
    bvalid = jnp.clip(counts[bexp] - ostart, 0, tmb).astype(I32)
    nblk = (pend[-1:] // tmb).astype(I32)
    cum_b = jnp.take(cum, bexp, axis=1)
    end_b = jnp.take(cum_end, bexp, axis=1)
    ntiles = tile_cnt.shape[0]
    t0 = jnp.minimum(jnp.sum((end_b <= ostart[None, :]).astype(I32), axis=0), ntiles - 1)
    t1 = jnp.maximum(jnp.sum((cum_b < (ostart + tmb)[None, :]).astype(I32), axis=0) - 1, 0)
    tables = tuple(a.astype(I32) for a in (bexp, bvalid, nblk, ostart, t0, t1, cum.reshape(-1),
                                           tile_cnt.reshape(-1), off.reshape(-1)))

    ys = _moe_experts(tables, xs_rows,
                      w_gate, b_gate.reshape(N_EXPERTS, 1, D_FF),
                      w_up, b_up.reshape(N_EXPERTS, 1, D_FF),
                      w_down, b_down.reshape(N_EXPERTS, 1, D_MODEL))
    outp, outs = _moe_combine(slots, gates, ys, x2.reshape(t, d), row(norm_final), nbp * seq)
    return outp.reshape(nbp, seq, d), outs.reshape(nb - nbp, seq, d)


def kernel(x_prompt, x_sample, mem_prompt, mem_sample, norm_mix, w_in, conv_w, conv_b, lru_w_a,
           lru_b_a, lru_w_x, lru_b_x, lru_lambda, norm_lru_out, norm_fft_out, w_out, norm_xattn,
           norm_mem, w_q, w_kv, w_o, norm_ffn, w_router, b_router, w_gate, b_gate, w_up, b_up,
           w_down, b_down, norm_final):
    assert x_prompt.shape[1:] == x_sample.shape[1:], "both groups must share (SEQ, D_MODEL)"
    assert w_in.shape[0] == 1, "single-layer block"
    mem = jnp.concatenate([mem_prompt, mem_sample], axis=0)
    return _encoder_layer(x_prompt, x_sample, mem, norm_mix[0], w_in[0], conv_w[0], conv_b[0],
                          lru_w_a[0], lru_b_a[0], lru_w_x[0], lru_b_x[0], lru_lambda[0],
                          norm_lru_out[0], norm_fft_out[0], w_out[0], norm_xattn[0], norm_mem[0],
                          w_q[0], w_kv[0], w_o[0], norm_ffn[0], w_router[0], b_router[0],
                          w_gate[0], b_gate[0], w_up[0], b_up[0], w_down[0], b_down[0], norm_final)
```

```python
import functools
import math

import numpy as np
import jax
import jax.numpy as jnp
from jax import lax
from jax.experimental import pallas as pl
from jax.experimental.pallas import tpu as pltpu

F32, BF16, I32 = jnp.float32, jnp.bfloat16, jnp.int32

D_MODEL = 1024
D_LRU = 512
LRU_HEAD_DIM = 64
CONV_WIDTH = 4
LRU_C = 8.0
D_FFT = 512
FFT_GROUP_DIM = 128
N_XATTN_HEADS = 4
XATTN_HEAD_DIM = 256
N_EXPERTS = 32
TOP_K = 4
D_FF = 1024
SWIGLU_LIMIT = 7.0
SWIGLU_ALPHA = 1.702
EPS = 1e-6

LANES = 128
SUBLANES = 8
ROW_TILES = D_MODEL // LANES
TOKEN_TILE = 512
EXPERT_TILE = 512
FF_CHUNK = 256
LRU_BLOCK = 128
SCAN_CHUNK = 128
SCAN_PITCH = SCAN_CHUNK + SUBLANES
DFT_RADIX = 64
DFT_PITCH = DFT_RADIX + SUBLANES
DFT_UNROLL = 4
VMEM_LIMIT = 56 * 1024 * 1024


def _params(*sem):
    return pltpu.CompilerParams(dimension_semantics=sem, vmem_limit_bytes=VMEM_LIMIT)


def _rms(xf):
    return xf * lax.rsqrt(jnp.mean(xf * xf, axis=-1, keepdims=True) + EPS)


def _gelu_tanh(x):
    return 0.5 * x * (1.0 + jnp.tanh(math.sqrt(2.0 / math.pi) * (x + 0.044715 * (x * x * x))))


def _kv_kernel(mem_ref, g_ref, wkt_ref, wv_ref, kt_ref, v_ref):
    mn = (_rms(mem_ref[0]) * g_ref[...]).astype(BF16)
    kt = lax.dot_general(wkt_ref[...], mn, (((1,), (1,)), ((), ())), preferred_element_type=F32)
    kt_ref[0] = kt.astype(BF16)
    v_ref[0] = jnp.dot(mn, wv_ref[...], preferred_element_type=F32).astype(BF16)


def _kv_proj(mem, g, wkt, wv):
    nb, m, d = mem.shape
    return pl.pallas_call(
        _kv_kernel,
        out_shape=(jax.ShapeDtypeStruct((nb, d, m), BF16), jax.ShapeDtypeStruct((nb, m, d), BF16)),
        grid=(nb,),
        in_specs=[pl.BlockSpec((1, m, d), lambda b: (b, 0, 0)),
                  pl.BlockSpec((1, d), lambda b: (0, 0)),
                  pl.BlockSpec((d, d), lambda b: (0, 0)),
                  pl.BlockSpec((d, d), lambda b: (0, 0))],
        out_specs=(pl.BlockSpec((1, d, m), lambda b: (b, 0, 0)),
                   pl.BlockSpec((1, m, d), lambda b: (b, 0, 0))),
        compiler_params=_params("arbitrary"),
        name="kv_proj",
    )(mem, g, wkt, wv)


def _inproj_kernel(xp_ref, xs_ref, g_ref, w_ref, dft_ref, xrec_ref, gate_ref, zr_ref, zi_ref,
                   *, prompt_tiles):
    x = jnp.where(pl.program_id(0) < prompt_tiles, xp_ref[...], xs_ref[...])
    h = (_rms(x) * g_ref[...]).astype(BF16)
    proj = jnp.dot(h, w_ref[...], preferred_element_type=F32)
    xrec_ref[...] = proj[:, :D_LRU]
    gate_ref[...] = proj[:, D_LRU:2 * D_LRU].astype(BF16)
    z = jnp.dot(proj[:, 2 * D_LRU:].astype(BF16), dft_ref[...], preferred_element_type=F32)
    pad = jnp.zeros((DFT_PITCH - DFT_RADIX, D_FFT), F32)
    for m in range(x.shape[0] // DFT_RADIX):
        rows = slice(m * DFT_RADIX, (m + 1) * DFT_RADIX)
        dst = slice(m * DFT_PITCH, m * DFT_PITCH + DFT_RADIX)
        gap = slice(m * DFT_PITCH + DFT_RADIX, (m + 1) * DFT_PITCH)
        zr_ref[dst, :] = z[rows, :D_FFT]
        zi_ref[dst, :] = z[rows, D_FFT:]
        zr_ref[gap, :] = pad
        zi_ref[gap, :] = pad


def _in_proj(xp2d, xs2d, g, w_in, dft_c):
    tm = TOKEN_TILE
    ptiles = xp2d.shape[0] // tm
    t = xp2d.shape[0] + xs2d.shape[0]
    tz = t // DFT_RADIX * DFT_PITCH
    tmz = tm // DFT_RADIX * DFT_PITCH
    row = lambda i: (i, 0)
    full = lambda i: (0, 0)
    return pl.pallas_call(
        functools.partial(_inproj_kernel, prompt_tiles=ptiles),
        out_shape=(jax.ShapeDtypeStruct((t, D_LRU), F32), jax.ShapeDtypeStruct((t, D_LRU), BF16),
                   jax.ShapeDtypeStruct((tz, D_FFT), F32), jax.ShapeDtypeStruct((tz, D_FFT), F32)),
        grid=(t // tm,),
        in_specs=[pl.BlockSpec((tm, D_MODEL), lambda i: (jnp.minimum(i, ptiles - 1), 0)),
                  pl.BlockSpec((tm, D_MODEL), lambda i: (jnp.maximum(i - ptiles, 0), 0)),
                  pl.BlockSpec((1, D_MODEL), full),
                  pl.BlockSpec(w_in.shape, full), pl.BlockSpec(dft_c.shape, full)],
        out_specs=(pl.BlockSpec((tm, D_LRU), row), pl.BlockSpec((tm, D_LRU), row),
                   pl.BlockSpec((tmz, D_FFT), row), pl.BlockSpec((tmz, D_FFT), row)),
        compiler_params=_params("arbitrary"),
        name="in_proj",
    )(xp2d, xs2d, g, w_in, dft_c)


def _lru_kernel(x_ref, gate_ref, cw_ref, cb_ref, w_ref, b_ref, lam_ref, y_ref,
                af_ref, uf_ref, ab_ref, ub_ref, cf_ref, cbk_ref, *, seq):
    nch = seq // SCAN_CHUNK
    cw = cw_ref[0]
    cb = cb_ref[0]
    lam = lam_ref[0]
    sp = jnp.maximum(-lam, 0.0) + jnp.log1p(jnp.exp(-jnp.abs(lam)))
    neg_half_c_sp = (-0.5 * LRU_C) * sp
    bias = b_ref[0]
    c = LRU_BLOCK

    def gates_body(j, _):
        r0 = pl.multiple_of(j * SCAN_CHUNK, SCAN_CHUNK)
        main = x_ref[0, pl.ds(r0, SCAN_CHUNK), :]
        prev = x_ref[0, pl.ds(jnp.maximum(r0 - SUBLANES, 0), SUBLANES), :]
        nxt = x_ref[0, pl.ds(jnp.minimum(r0 + SCAN_CHUNK, seq - SUBLANES), SUBLANES), :]
        prev = jnp.where(j > 0, prev, 0.0)
        nxt = jnp.where(j < nch - 1, nxt, 0.0)
        win = jnp.concatenate([prev, main, nxt], axis=0)
        base = SUBLANES - CONV_WIDTH // 2
        xc = cb
        for tap in range(CONV_WIDTH):
            xc = xc + win[base + tap:base + tap + SCAN_CHUNK, :] * cw[tap:tap + 1, :]
        xcb = xc.astype(BF16)
        half_xc = 0.5 * xc
        o0 = pl.multiple_of(j * SCAN_PITCH, SUBLANES)
        for d, (a_ref, u_ref) in enumerate(((af_ref, uf_ref), (ab_ref, ub_ref))):
            ga = jnp.dot(xcb, w_ref[0, :, d * c:(d + 1) * c], preferred_element_type=F32) \
                + bias[:, d * c:(d + 1) * c]
            gx = jnp.dot(xcb, w_ref[0, :, (2 + d) * c:(3 + d) * c], preferred_element_type=F32) \
                + bias[:, (2 + d) * c:(3 + d) * c]
            log_a = neg_half_c_sp[d:d + 1, :] * (1.0 + jnp.tanh(0.5 * ga))
            ix = half_xc + half_xc * jnp.tanh(0.5 * gx)
            a = jnp.exp(log_a)
            u = jnp.sqrt(jnp.tanh(log_a) * (-1.0 - a * a)) * ix
            a_ref[pl.ds(o0, SCAN_CHUNK), :] = a
            u_ref[pl.ds(o0, SCAN_CHUNK), :] = u
        return 0

    lax.fori_loop(0, nch, gates_body, 0, unroll=2)

    def scan_body(t, carry):
        hf, pf, hb, pb = carry
        fwd = pl.ds(t, nch, stride=SCAN_PITCH)
        bwd = pl.ds(SCAN_CHUNK - 1 - t, nch, stride=SCAN_PITCH)
        a = af_ref[fwd, :]
        hf = a * hf + uf_ref[fwd, :]
        pf = a * pf
        uf_ref[fwd, :] = hf
        af_ref[fwd, :] = pf
        a = ab_ref[bwd, :]
        hb = a * hb + ub_ref[bwd, :]
        pb = a * pb
        ub_ref[bwd, :] = hb
        ab_ref[bwd, :] = pb
        return hf, pf, hb, pb

    zeros = jnp.zeros((nch, c), F32)
    ones = jnp.ones((nch, c), F32)
    hf, pf, hb, pb = lax.fori_loop(0, SCAN_CHUNK, scan_body, (zeros, ones, zeros, ones), unroll=4)

    carry = jnp.zeros((1, c), F32)
    cf_ref[0:1, :] = carry
    for j in range(1, nch):
        carry = hf[j - 1:j, :] + pf[j - 1:j, :] * carry
        cf_ref[j:j + 1, :] = carry
    carry = jnp.zeros((1, c), F32)
    cbk_ref[nch - 1:nch, :] = carry
    for j in range(nch - 2, -1, -1):
        carry = hb[j + 1:j + 2, :] + pb[j + 1:j + 2, :] * carry
        cbk_ref[j:j + 1, :] = carry

    def out_body(j, _):
        r0 = pl.multiple_of(j * SCAN_CHUNK, SCAN_CHUNK)
        o0 = pl.multiple_of(j * SCAN_PITCH, SUBLANES)
        rows = pl.ds(o0, SCAN_CHUNK)
        h = (uf_ref[rows, :] + af_ref[rows, :] * cf_ref[pl.ds(j, 1), :]
             + ub_ref[rows, :] + ab_ref[rows, :] * cbk_ref[pl.ds(j, 1), :])
        g = gate_ref[0, pl.ds(r0, SCAN_CHUNK), :].astype(F32)
        y_ref[0, pl.ds(r0, SCAN_CHUNK), :] = (h * _gelu_tanh(g)).astype(BF16)
        return 0

    lax.fori_loop(0, nch, out_body, 0, unroll=2)


def _lru_mix(xrec, gate, conv_w, conv_b, w_gates, b_gates, lam):
    nb, seq, _ = xrec.shape
    ncb = D_LRU // LRU_BLOCK
    c = LRU_BLOCK
    nch = seq // SCAN_CHUNK
    act = lambda b, k: (b, 0, k)
    par = lambda b, k: (k, 0, 0)
    scan_buf = pltpu.VMEM((nch * SCAN_PITCH, c), F32)
    return pl.pallas_call(
        functools.partial(_lru_kernel, seq=seq),
        out_shape=jax.ShapeDtypeStruct((nb, seq, D_LRU), BF16),
        grid=(nb, ncb),
        in_specs=[pl.BlockSpec((1, seq, c), act), pl.BlockSpec((1, seq, c), act),
                  pl.BlockSpec((1, CONV_WIDTH, c), par), pl.BlockSpec((1, 1, c), par),
                  pl.BlockSpec((1, c, 4 * c), par), pl.BlockSpec((1, 1, 4 * c), par),
                  pl.BlockSpec((1, 2, c), par)],
        out_specs=pl.BlockSpec((1, seq, c), act),
        scratch_shapes=[scan_buf, scan_buf, scan_buf, scan_buf,
                        pltpu.VMEM((nch, c), F32), pltpu.VMEM((nch, c), F32)],
        compiler_params=_params("arbitrary", "arbitrary"),
        name="lru_mix",
    )(xrec, gate, conv_w, conv_b, w_gates, b_gates, lam)


def _seqdft_kernel(zr_ref, zi_ref, f1_ref, m2_ref, y_ref, ar_ref, ai_ref):
    r, pitch, unroll = DFT_RADIX, DFT_PITCH, DFT_UNROLL
    f1 = f1_ref[...]

    def stage1(it, _):
        i2 = it * unroll
        slabs = []
        for q in range(unroll):
            rows = pl.ds(i2 + q, r, stride=pitch)
            slabs.append(jnp.concatenate([zr_ref[0, rows, :], zi_ref[0, rows, :]], axis=0))
        rhs = jnp.concatenate(slabs, axis=1).astype(BF16)
        a = jnp.dot(f1, rhs, preferred_element_type=F32)
        for q in range(unroll):
            rows = pl.ds(i2 + q, r, stride=pitch)
            ar_ref[rows, :] = a[:r, q * LANES:(q + 1) * LANES]
            ai_ref[rows, :] = a[r:, q * LANES:(q + 1) * LANES]
        return 0

    lax.fori_loop(0, r // unroll, stage1, 0, unroll=4)

    def stage2(it, _):
        for q in range(unroll):
            k1 = it * unroll + q
            src = pl.ds(pl.multiple_of(k1 * pitch, SUBLANES), r)
            slab = jnp.concatenate([ar_ref[src, :], ai_ref[src, :]], axis=0).astype(BF16)
            o = jnp.dot(m2_ref[k1], slab, preferred_element_type=F32)
            y_ref[0, pl.ds(k1, r, stride=pitch), :] = o
        return 0

    lax.fori_loop(0, r // unroll, stage2, 0, unroll=4)
    for gap in range(r, pitch):
        y_ref[0, pl.ds(gap, r, stride=pitch), :] = jnp.zeros((r, LANES), F32)


def _seq_dft(zr, zi, f1, m2):
    nb, rows, _ = zr.shape
    cw = LANES
    act = lambda b, k: (b, 0, k)
    return pl.pallas_call(
        _seqdft_kernel,
        out_shape=jax.ShapeDtypeStruct((nb, rows, D_FFT), F32),
        grid=(nb, D_FFT // cw),
        in_specs=[pl.BlockSpec((1, rows, cw), act), pl.BlockSpec((1, rows, cw), act),
                  pl.BlockSpec(f1.shape, lambda b, k: (0, 0)),
                  pl.BlockSpec(m2.shape, lambda b, k: (0, 0, 0))],
        out_specs=pl.BlockSpec((1, rows, cw), act),
        scratch_shapes=[pltpu.VMEM((rows, cw), F32), pltpu.VMEM((rows, cw), F32)],
        compiler_params=_params("arbitrary", "arbitrary"),
        name="seq_dft",
    )(zr, zi, f1, m2)


def _dft_tables(seq):
    n1 = n2 = DFT_RADIX
    assert n1 * n2 == seq, "sequence DFT is factored as DFT_RADIX x DFT_RADIX"
    g = FFT_GROUP_DIM
    ang = 2.0 * np.pi * np.outer(np.arange(g), np.arange(g)) / g
    cg, sg = np.cos(ang) / math.sqrt(g), np.sin(ang) / math.sqrt(g)
    ngroups = D_FFT // g
    dft_c = np.zeros((D_FFT, 2 * D_FFT))
    for q in range(ngroups):
        dft_c[q * g:(q + 1) * g, q * g:(q + 1) * g] = cg
        dft_c[q * g:(q + 1) * g, D_FFT + q * g:D_FFT + (q + 1) * g] = -sg
    ang1 = 2.0 * np.pi * np.outer(np.arange(n1), np.arange(n1)) / n1
    c1, s1 = np.cos(ang1) / math.sqrt(n1), np.sin(ang1) / math.sqrt(n1)
    f1 = np.block([[c1, s1], [-s1, c1]])
    k1 = np.arange(n1)[:, None, None]
    k2 = np.arange(n2)[None, :, None]
    i2 = np.arange(n2)[None, None, :]
    ang2 = 2.0 * np.pi * (i2 * k2 / n2 + i2 * k1 / seq)
    m2 = np.concatenate([np.cos(ang2), np.sin(ang2)], axis=2) / math.sqrt(n2)
    return (jnp.asarray(dft_c, F32), jnp.asarray(f1, F32), jnp.asarray(m2, F32))


def _flat_smem_copies(src_ref, smem_ref, base, sem):
    nk, nc, nl = src_ref.shape
    return [pltpu.make_async_copy(src_ref.at[k, c],
                                  smem_ref.at[pl.ds(base + (k * nc + c) * nl, nl)], sem)
            for k in range(nk) for c in range(nc)]


def _attn_route_kernel(xp_ref, xs_ref, ylru_ref, yfft_ref, kt_ref, v_ref, gl_ref, gf_ref, wout_ref,
                       gx_ref, wq_ref, wo_ref, gffn_ref, wrh_ref, wrl_ref, br_ref,
                       x2_ref, xs_out_ref, slot_ref, gate_ref, cnt_ref, h3tl_ref, slot_vmem, slot_smem,
                       ssem, *, prompt_tiles, tiles):
    g = pl.program_id(0)
    tm = xp_ref.shape[1]
    n = TOP_K * tm
    cur = lax.rem(g, 2)
    prev = 1 - cur

    @pl.when(g == 0)
    def _():
        h3tl_ref[1] = jnp.zeros(h3tl_ref.shape[1:], F32)

        def clear(j, _):
            slot_smem[n + j] = 0
            return 0

        lax.fori_loop(0, n, clear, 0)

    @pl.when(g >= 1)
    def _():
        for cp in _flat_smem_copies(slot_vmem.at[prev], slot_smem, prev * n, ssem.at[prev]):
            cp.wait()

    for t in range(tm):
        row = h3tl_ref[prev, t * ROW_TILES:(t + 1) * ROW_TILES, :]
        for k in range(TOP_K):
            dst = pl.multiple_of(slot_smem[prev * n + k * tm + t], ROW_TILES)
            xs_out_ref[pl.ds(dst, ROW_TILES), :] = row

    x = jnp.where(g < prompt_tiles, xp_ref[0], xs_ref[0])
    m_lru = _rms(ylru_ref[0].astype(F32)) * gl_ref[...]
    yfft = jnp.concatenate([yfft_ref[0, m * DFT_PITCH:m * DFT_PITCH + DFT_RADIX, :]
                            for m in range(tm // DFT_RADIX)], axis=0)
    m_fft = _rms(yfft) * gf_ref[...]
    mixed = jnp.concatenate([m_lru, m_fft], axis=-1).astype(BF16)
    x1 = x + jnp.dot(mixed, wout_ref[...], preferred_element_type=F32)

    hq = (_rms(x1) * gx_ref[...]).astype(BF16)
    q = jnp.dot(hq, wq_ref[...], preferred_element_type=F32) * (XATTN_HEAD_DIM ** -0.5)
    qb = q.astype(BF16)
    heads = []
    for h in range(N_XATTN_HEADS):
        sl = slice(h * XATTN_HEAD_DIM, (h + 1) * XATTN_HEAD_DIM)
        s = jnp.dot(qb[:, sl], kt_ref[0, sl, :], preferred_element_type=F32)
        p = jnp.exp(s - jnp.max(s, axis=-1, keepdims=True))
        inv = 1.0 / jnp.sum(p, axis=-1, keepdims=True)
        o = jnp.dot(p.astype(BF16), v_ref[0, :, sl], preferred_element_type=F32) * inv
        heads.append(o.astype(BF16))
    att = jnp.concatenate(heads, axis=-1)
    x2 = x1 + jnp.dot(att, wo_ref[...], preferred_element_type=F32)
    x2_ref[0] = x2

    h3 = _rms(x2) * gffn_ref[...]
    for s in range(ROW_TILES):
        h3tl_ref[cur, pl.ds(s, tm, stride=ROW_TILES), :] = h3[:, s * LANES:(s + 1) * LANES]

    h_hi = h3.astype(BF16)
    h_lo = (h3 - h_hi.astype(F32)).astype(BF16)
    logits = (jnp.dot(h_hi, wrh_ref[...], preferred_element_type=F32)
              + jnp.dot(h_lo, wrh_ref[...], preferred_element_type=F32)
              + jnp.dot(h_hi, wrl_ref[...], preferred_element_type=F32)) + br_ref[...]
    lg = logits.T[:N_EXPERTS, :]
    e_iota = lax.broadcasted_iota(I32, lg.shape, 0)
    vals, onehots = [], []
    for k in range(TOP_K):
        m = jnp.max(lg, axis=0, keepdims=True)
        idx = jnp.min(jnp.where(lg == m, e_iota, N_EXPERTS), axis=0, keepdims=True)
        oh = e_iota == idx
        vals.append(m)
        onehots.append(oh)
        lg = jnp.where(oh, -jnp.inf, lg)
    exps = [jnp.exp(v - vals[0]) for v in vals]
    inv = 1.0 / (exps[0] + exps[1] + exps[2] + exps[3])

    def lane_tiles(rows):
        return jnp.stack([jnp.concatenate([r[:, c * LANES:(c + 1) * LANES]
                                           for c in range(tm // LANES)], axis=0) for r in rows])

    gate_ref[0] = lane_tiles([e * inv for e in exps])

    member = jnp.zeros(lg.shape, F32)
    for oh in onehots:
        member = member + oh.astype(F32)
    mb = member.astype(BF16)
    r_i = lax.broadcasted_iota(I32, (tm, tm), 0)
    c_i = lax.broadcasted_iota(I32, (tm, tm), 1)
    earlier = jnp.where(r_i < c_i, 1.0, 0.0).astype(BF16)
    tok_rank = jnp.dot(mb, earlier, preferred_element_type=F32)
    er = lax.broadcasted_iota(I32, (N_EXPERTS, N_EXPERTS), 0)
    ec = lax.broadcasted_iota(I32, (N_EXPERTS, N_EXPERTS), 1)
    lower = jnp.where(ec < er, 1.0, 0.0).astype(BF16)
    exp_off = jnp.sum(jnp.dot(lower, mb, preferred_element_type=F32), axis=1, keepdims=True)
    pos = exp_off + tok_rank
    slots = [jnp.sum(jnp.where(oh, pos, 0.0), axis=0, keepdims=True) for oh in onehots]
    slot_tab = lane_tiles(slots).astype(I32) * ROW_TILES
    slot_ref[0] = slot_tab
    slot_vmem[cur] = slot_tab
    ones = jnp.ones((SUBLANES, tm), BF16)
    cnt = lax.dot_general(ones, mb, (((1,), (1,)), ((), ())), preferred_element_type=F32)
    cnt_ref[0] = cnt[0:1, :].astype(I32)

    @pl.when(g < tiles)
    def _():
        for cp in _flat_smem_copies(slot_vmem.at[cur], slot_smem, cur * n, ssem.at[cur]):
            cp.start()


def _attn_route(xp, xs, ylru, yfft, kt, v, gl, gf, w_out, gx, wq, wo, gffn, wr_hi, wr_lo, br):
    nbp, seq, d = xp.shape
    nb = nbp + xs.shape[0]
    tm = TOKEN_TILE
    tmz = tm // DFT_RADIX * DFT_PITCH
    nt = seq // tm
    m = v.shape[1]
    tiles = nb * nt
    ptiles = nbp * nt
    tile = lambda g: jnp.minimum(g, tiles - 1)
    tok = lambda g: (tile(g) // nt, tile(g) % nt, 0)
    full2 = lambda g: (0, 0)
    per_b = lambda g: (tile(g) // nt, 0, 0)
    tile_id = lambda g: (tile(g), 0, 0)
    tile_id4 = lambda g: (tile(g), 0, 0, 0)
    xp_idx = lambda g: (jnp.minimum(g, ptiles - 1) // nt, jnp.minimum(g, ptiles - 1) % nt, 0)
    xs_idx = lambda g: (jnp.maximum(tile(g) - ptiles, 0) // nt, jnp.maximum(tile(g) - ptiles, 0) % nt, 0)
    nw = wr_hi.shape[1]
    per_tok = (TOP_K, tm // LANES, LANES)
    return pl.pallas_call(
        functools.partial(_attn_route_kernel, prompt_tiles=ptiles, tiles=tiles),
        out_shape=(jax.ShapeDtypeStruct((nb, seq, d), F32),
                   jax.ShapeDtypeStruct((tiles * tm * TOP_K * ROW_TILES, LANES), F32),
                   jax.ShapeDtypeStruct((tiles,) + per_tok, I32),
                   jax.ShapeDtypeStruct((tiles,) + per_tok, F32),
                   jax.ShapeDtypeStruct((tiles, 1, N_EXPERTS), I32)),
        grid=(tiles + 1,),
        in_specs=[pl.BlockSpec((1, tm, d), xp_idx),
                  pl.BlockSpec((1, tm, d), xs_idx),
                  pl.BlockSpec((1, tm, D_LRU), tok),
                  pl.BlockSpec((1, tmz, D_FFT), tok),
                  pl.BlockSpec((1, d, m), per_b), pl.BlockSpec((1, m, d), per_b),
                  pl.BlockSpec((1, D_LRU), full2), pl.BlockSpec((1, D_FFT), full2),
                  pl.BlockSpec((d, d), full2), pl.BlockSpec((1, d), full2),
                  pl.BlockSpec((d, d), full2), pl.BlockSpec((d, d), full2),
                  pl.BlockSpec((1, d), full2), pl.BlockSpec((d, nw), full2),
                  pl.BlockSpec((d, nw), full2), pl.BlockSpec((1, nw), full2)],
        out_specs=(pl.BlockSpec((1, tm, d), tok),
                   pl.BlockSpec((tm * TOP_K * ROW_TILES, LANES), lambda g: (jnp.maximum(g - 1, 0), 0)),
                   pl.BlockSpec((1,) + per_tok, tile_id4), pl.BlockSpec((1,) + per_tok, tile_id4),
                   pl.BlockSpec((1, 1, N_EXPERTS), tile_id)),
        scratch_shapes=[pltpu.VMEM((2, tm * ROW_TILES, LANES), F32),
                        pltpu.VMEM((2,) + per_tok, I32),
                        pltpu.SMEM((2 * TOP_K * tm,), I32), pltpu.SemaphoreType.DMA((2,))],
        compiler_params=_params("arbitrary"),
        name="attn_route",
    )(xp, xs, ylru, yfft, kt, v, gl, gf, w_out, gx, wq, wo, gffn, wr_hi, wr_lo, br)


def _for_each_run(blk, bexp_ref, ostart_ref, t0_ref, t1_ref, cum_ref, cnt_ref, off_ref, fn):
    e = bexp_ref[blk]
    o0 = ostart_ref[blk]

    def body(i, _):
        j = i * N_EXPERTS + e
        c = cum_ref[j]
        lo = jnp.maximum(o0, c)
        hi = jnp.minimum(o0 + EXPERT_TILE, c + cnt_ref[j])

        @pl.when(hi > lo)
        def _():
            fn(i * (TOKEN_TILE * TOP_K) + off_ref[j] + (lo - c), lo - o0, hi - lo)
        return 0

    lax.fori_loop(t0_ref[blk], t1_ref[blk] + 1, body, 0)


def _rows(ref, row, n_rows):
    return ref.at[pl.ds(pl.multiple_of(row * ROW_TILES, ROW_TILES), n_rows * ROW_TILES), :]


def _experts_kernel(bexp_ref, bvalid_ref, nblk_ref, ostart_ref, t0_ref, t1_ref, cum_ref, cnt_ref,
                    off_ref, xs_hbm, wg_ref, bg_ref, wu_ref, bu_ref, wd_ref, bd_ref, ys_hbm,
                    xbuf, ybuf, act_ref, wgb_ref, wub_ref, wdb_ref, gsem, osem):
    b = pl.program_id(0)
    nblk = nblk_ref[0]
    slot = lax.rem(b, 2)
    tabs = (bexp_ref, ostart_ref, t0_ref, t1_ref, cum_ref, cnt_ref, off_ref)

    def gather(blk, sl):
        _for_each_run(blk, *tabs, lambda src, dst, n: pltpu.make_async_copy(
            _rows(xs_hbm, src, n), _rows(xbuf.at[sl], dst, n), gsem.at[sl]).start())

    def wait_in(blk, sl):
        n = bvalid_ref[blk]
        pltpu.make_async_copy(_rows(xs_hbm, 0, n), _rows(xbuf.at[sl], 0, n), gsem.at[sl]).wait()

    def wait_out(blk, sl):
        n = bvalid_ref[blk]
        pltpu.make_async_copy(_rows(ybuf.at[sl], 0, n), _rows(ys_hbm, 0, n), osem.at[sl]).wait()

    @pl.when(b == 0)
    def _():
        gather(0, 0)

    @pl.when(b + 1 < nblk)
    def _():
        gather(b + 1, 1 - slot)

    @pl.when(b < nblk)
    def _():
        wait_in(b, slot)

        @pl.when(b >= 2)
        def _():
            wait_out(b - 2, slot)

        @pl.when(jnp.logical_or(b == 0, bexp_ref[b] != bexp_ref[jnp.maximum(b - 1, 0)]))
        def _():
            def cast(r, _):
                rows = pl.ds(pl.multiple_of(r * LANES, LANES), LANES)
                wgb_ref[rows, :] = wg_ref[0, rows, :].astype(BF16)
                wub_ref[rows, :] = wu_ref[0, rows, :].astype(BF16)
                wdb_ref[rows, :] = wd_ref[0, rows, :].astype(BF16)
                return 0

            lax.fori_loop(0, D_MODEL // LANES, cast, 0)

        tm = EXPERT_TILE
        x = jnp.concatenate([xbuf[slot, pl.ds(s, tm, stride=ROW_TILES), :]
                             for s in range(ROW_TILES)], axis=-1)
        row = lax.broadcasted_iota(I32, (tm, 1), 0)
        xb = jnp.where(row < bvalid_ref[b], x, 0.0).astype(BF16)
        for cidx in range(D_FF // FF_CHUNK):
            sl = slice(cidx * FF_CHUNK, (cidx + 1) * FF_CHUNK)
            gt = jnp.dot(xb, wgb_ref[:, sl], preferred_element_type=F32) + bg_ref[0, :, sl]
            up = jnp.dot(xb, wub_ref[:, sl], preferred_element_type=F32) + bu_ref[0, :, sl]
            gt = jnp.minimum(gt, SWIGLU_LIMIT)
            up = jnp.clip(up, -SWIGLU_LIMIT, SWIGLU_LIMIT)
            hg = 0.5 * gt
            act_ref[:, sl] = ((hg + hg * jnp.tanh(SWIGLU_ALPHA * hg)) * (up + 1.0)).astype(BF16)
        for cidx in range(D_MODEL // FF_CHUNK):
            sl = slice(cidx * FF_CHUNK, (cidx + 1) * FF_CHUNK)
            o = jnp.dot(act_ref[...], wdb_ref[:, sl], preferred_element_type=F32) + bd_ref[0, :, sl]
            for q in range(FF_CHUNK // LANES):
                s = cidx * (FF_CHUNK // LANES) + q
                ybuf[slot, pl.ds(s, tm, stride=ROW_TILES), :] = o[:, q * LANES:(q + 1) * LANES]
        _for_each_run(b, *tabs, lambda dst, src, n: pltpu.make_async_copy(
            _rows(ybuf.at[slot], src, n), _rows(ys_hbm, dst, n), osem.at[slot]).start())

        @pl.when(b == nblk - 1)
        def _():
            wait_out(b, slot)

            @pl.when(b >= 1)
            def _():
                wait_out(b - 1, 1 - slot)


def _moe_experts(tables, xs, wg, bg, wu, bu, wd, bd):
    bexp = tables[0]
    nblocks = bexp.shape[0]
    tm = EXPERT_TILE
    wsel = lambda i, be, bv, nb, *_: (be[jnp.minimum(i, nb[0] - 1)], 0, 0)
    wspec = pl.BlockSpec((1, D_MODEL, D_FF), wsel)
    bspec = pl.BlockSpec((1, 1, D_FF), wsel)
    hbm = pl.BlockSpec(memory_space=pl.ANY)
    return pl.pallas_call(
        _experts_kernel,
        out_shape=jax.ShapeDtypeStruct(xs.shape, F32),
        grid_spec=pltpu.PrefetchScalarGridSpec(
            num_scalar_prefetch=len(tables),
            grid=(nblocks,),
            in_specs=[hbm, wspec, bspec, wspec, bspec, wspec, bspec],
            out_specs=hbm,
            scratch_shapes=[pltpu.VMEM((2, tm * ROW_TILES, LANES), F32),
                            pltpu.VMEM((2, tm * ROW_TILES, LANES), F32),
                            pltpu.VMEM((tm, D_FF), BF16),
                            pltpu.VMEM((D_MODEL, D_FF), BF16), pltpu.VMEM((D_MODEL, D_FF), BF16),
                            pltpu.VMEM((D_FF, D_MODEL), BF16),
                            pltpu.SemaphoreType.DMA((2,)), pltpu.SemaphoreType.DMA((2,))]),
        compiler_params=_params("arbitrary"),
        name="moe_experts",
    )(*tables, xs, wg, bg, wu, bu, wd, bd)


def _combine_kernel(slot_hbm, gate_hbm, ys_ref, x2_ref, gfin_ref, outp_ref, outs_ref,
                    slot_smem, gate_smem, acc_ref, isem, *, prompt_tiles):
    i = pl.program_id(0)
    tm = x2_ref.shape[0]
    n = TOP_K * tm
    half = lax.rem(i, 2)
    base = half * n

    def table_copies(tile, hf):
        return (_flat_smem_copies(slot_hbm.at[tile], slot_smem, hf * n, isem.at[hf])
                + _flat_smem_copies(gate_hbm.at[tile], gate_smem, hf * n, isem.at[hf]))

    @pl.when(i == 0)
    def _():
        for cp in table_copies(0, 0):
            cp.start()

    @pl.when(i + 1 < pl.num_programs(0))
    def _():
        for cp in table_copies(i + 1, 1 - half):
            cp.start()

    x2 = x2_ref[...]
    for s in range(ROW_TILES):
        acc_ref[pl.ds(s, tm, stride=ROW_TILES), :] = x2[:, s * LANES:(s + 1) * LANES]
    for cp in table_copies(i, half):
        cp.wait()

    def body(t, _):
        rows = pl.ds(pl.multiple_of(t * ROW_TILES, ROW_TILES), ROW_TILES)
        acc = acc_ref[rows, :]
        for k in range(TOP_K):
            src = pl.multiple_of(slot_smem[base + k * tm + t], ROW_TILES)
            acc = acc + gate_smem[base + k * tm + t] * ys_ref[pl.ds(src, ROW_TILES), :]
        acc_ref[rows, :] = acc
        return 0

    lax.fori_loop(0, tm, body, 0, unroll=8)
    acc = jnp.concatenate([acc_ref[pl.ds(s, tm, stride=ROW_TILES), :] for s in range(ROW_TILES)],
                          axis=-1)
    y = _rms(acc) * gfin_ref[...]

    @pl.when(i < prompt_tiles)
    def _():
        outp_ref[...] = y

    @pl.when(i >= prompt_tiles)
    def _():
        outs_ref[...] = y


def _moe_combine(slots, gates, ys, x2, gfin, t_prompt):
    t, d = x2.shape
    tm = TOKEN_TILE
    ptiles = t_prompt // tm
    return pl.pallas_call(
        functools.partial(_combine_kernel, prompt_tiles=ptiles),
        out_shape=(jax.ShapeDtypeStruct((t_prompt, d), F32),
                   jax.ShapeDtypeStruct((t - t_prompt, d), F32)),
        grid=(t // tm,),
        in_specs=[pl.BlockSpec(memory_space=pl.ANY), pl.BlockSpec(memory_space=pl.ANY),
                  pl.BlockSpec((tm * TOP_K * ROW_TILES, LANES), lambda i: (i, 0)),
                  pl.BlockSpec((tm, d), lambda i: (i, 0)),
                  pl.BlockSpec((1, d), lambda i: (0, 0))],
        out_specs=(pl.BlockSpec((tm, d), lambda i: (jnp.minimum(i, ptiles - 1), 0)),
                   pl.BlockSpec((tm, d), lambda i: (jnp.maximum(i - ptiles, 0), 0))),
        scratch_shapes=[pltpu.SMEM((2 * TOP_K * tm,), I32), pltpu.SMEM((2 * TOP_K * tm,), F32),
                        pltpu.VMEM((tm * ROW_TILES, LANES), F32),
                        pltpu.SemaphoreType.DMA((2,))],
        compiler_params=_params("arbitrary"),
        name="moe_combine",
    )(slots, gates, ys, x2, gfin)


def _lru_gate_weights(w_a, b_a, w_x, b_x):
    hpb = LRU_BLOCK // LRU_HEAD_DIM
    ncb = D_LRU // LRU_BLOCK

    def blockdiag(w):
        w = w.reshape(ncb, hpb, LRU_HEAD_DIM, LRU_HEAD_DIM)
        eye = jnp.eye(hpb, dtype=w.dtype)
        full = jnp.einsum('nhij,hg->nhigj', w, eye)
        return full.reshape(ncb, LRU_BLOCK, LRU_BLOCK)

    mats = [blockdiag(w_a[0]), blockdiag(w_a[1]), blockdiag(w_x[0]), blockdiag(w_x[1])]
    w = jnp.concatenate(mats, axis=-1).astype(BF16)
    bs = [b.reshape(ncb, 1, LRU_BLOCK) for b in (b_a[0], b_a[1], b_x[0], b_x[1])]
    return w, jnp.concatenate(bs, axis=-1).astype(F32)


def _encoder_layer(xp, xs, mem, norm_mix, w_in, conv_w, conv_b, lru_w_a, lru_b_a, lru_w_x, lru_b_x,
                   lru_lambda, norm_lru_out, norm_fft_out, w_out, norm_xattn, norm_mem, w_q, w_kv,
                   w_o, norm_ffn, w_router, b_router, w_gate, b_gate, w_up, b_up, w_down, b_down,
                   norm_final):
    nbp, seq, d = xp.shape
    nb = nbp + xs.shape[0]
    t = nb * seq
    seqz = seq // DFT_RADIX * DFT_PITCH
    ncb = D_LRU // LRU_BLOCK
    row = lambda v: v.reshape(1, -1).astype(F32)

    dft_c, f1, m2 = _dft_tables(seq)
    kt, v = _kv_proj(mem, row(norm_mem), w_kv[:, :d].T.astype(BF16), w_kv[:, d:].astype(BF16))

    xrec, gate, zr, zi = _in_proj(xp.reshape(nbp * seq, d), xs.reshape(t - nbp * seq, d),
                                  row(norm_mix), w_in.astype(BF16), dft_c.astype(BF16))

    w_gates, b_gates = _lru_gate_weights(lru_w_a, lru_b_a, lru_w_x, lru_b_x)
    cw = conv_w.reshape(CONV_WIDTH, ncb, LRU_BLOCK).transpose(1, 0, 2)
    cb = conv_b.reshape(ncb, 1, LRU_BLOCK)
    lam = lru_lambda.reshape(2, ncb, LRU_BLOCK).transpose(1, 0, 2)
    ylru = _lru_mix(xrec.reshape(nb, seq, D_LRU), gate.reshape(nb, seq, D_LRU), cw, cb,
                    w_gates, b_gates, lam)

    yfft = _seq_dft(zr.reshape(nb, seqz, D_FFT), zi.reshape(nb, seqz, D_FFT),
                    f1.astype(BF16), m2.astype(BF16))

    w_r = jnp.pad(w_router.astype(F32), ((0, 0), (0, LANES - N_EXPERTS)))
    w_r_hi = w_r.astype(BF16)
    w_r_lo = (w_r - w_r_hi.astype(F32)).astype(BF16)
    b_r = jnp.pad(b_router.astype(F32).reshape(1, -1), ((0, 0), (0, LANES - N_EXPERTS)))
    x2, xs_rows, slots, gates, tile_cnt = _attn_route(
        xp, xs, ylru, yfft, kt, v, row(norm_lru_out), row(norm_fft_out), w_out.astype(BF16),
        row(norm_xattn), w_q.astype(BF16), w_o.astype(BF16), row(norm_ffn), w_r_hi, w_r_lo, b_r)

    tmb = EXPERT_TILE
    nblocks = -(-(t * TOP_K) // tmb) + N_EXPERTS
    tile_cnt = tile_cnt.reshape(-1, N_EXPERTS)
    cum_end = jnp.cumsum(tile_cnt, axis=0)
    cum = cum_end - tile_cnt
    off = jnp.cumsum(tile_cnt, axis=1) - tile_cnt
    counts = cum_end[-1]
    padded = ((counts + tmb - 1) // tmb) * tmb
    pend = jnp.cumsum(padded)
    pstart = pend - padded
    blk_start = jnp.arange(nblocks, dtype=I32) * tmb
    bexp = jnp.minimum(jnp.sum((pend[None, :] <= blk_start[:, None]).astype(I32), axis=1),
                       N_EXPERTS - 1)
    ostart = blk_start - pstart[bexp]
    bvalid = jnp.clip(counts[bexp] - ostart, 0, tmb).astype(I32)
    nblk = (pend[-1:] // tmb).astype(I32)
    cum_b = jnp.take(cum, bexp, axis=1)
    end_b = jnp.take(cum_end, bexp, axis=1)
    ntiles = tile_cnt.shape[0]
    t0 = jnp.minimum(jnp.sum((end_b <= ostart[None, :]).astype(I32), axis=0), ntiles - 1)
    t1 = jnp.maximum(jnp.sum((cum_b < (ostart + tmb)[None, :]).astype(I32), axis=0) - 1, 0)
    tables = tuple(a.astype(I32) for a in (bexp, bvalid, nblk, ostart, t0, t1, cum.reshape(-1),
                                           tile_cnt.reshape(-1), off.reshape(-1)))

    ys = _moe_experts(tables, xs_rows,
                      w_gate, b_gate.reshape(N_EXPERTS, 1, D_FF),
                      w_up, b_up.reshape(N_EXPERTS, 1, D_FF),
                      w_down, b_down.reshape(N_EXPERTS, 1, D_MODEL))
    outp, outs = _moe_combine(slots, gates, ys, x2.reshape(t, d), row(norm_final), nbp * seq)
    return outp.reshape(nbp, seq, d), outs.reshape(nb - nbp, seq, d)


def kernel(x_prompt, x_sample, mem_prompt, mem_sample, norm_mix, w_in, conv_w, conv_b, lru_w_a,
           lru_b_a, lru_w_x, lru_b_x, lru_lambda, norm_lru_out, norm_fft_out, w_out, norm_xattn,
           norm_mem, w_q, w_kv, w_o, norm_ffn, w_router, b_router, w_gate, b_gate, w_up, b_up,
           w_down, b_down, norm_final):
    assert x_prompt.shape[1:] == x_sample.shape[1:], "both groups must share (SEQ, D_MODEL)"
    assert w_in.shape[0] == 1, "single-layer block"
    mem = jnp.concatenate([mem_prompt, mem_sample], axis=0)
    return _encoder_layer(x_prompt, x_sample, mem, norm_mix[0], w_in[0], conv_w[0], conv_b[0],
                          lru_w_a[0], lru_b_a[0], lru_w_x[0], lru_b_x[0], lru_lambda[0],
                          norm_lru_out[0], norm_fft_out[0], w_out[0], norm_xattn[0], norm_mem[0],
                          w_q[0], w_kv[0], w_o[0], norm_ffn[0], w_router[0], b_router[0],
                          w_gate[0], b_gate[0], w_up[0], b_up[0], w_down[0], b_down[0], norm_final)
```

```python
import functools
import math

import numpy as np
import jax
import jax.numpy as jnp
from jax import lax
from jax.experimental import pallas as pl
from jax.experimental.pallas import tpu as pltpu

F32, BF16, I32 = jnp.float32, jnp.bfloat16, jnp.int32

D_MODEL = 1024
D_LRU = 512
LRU_HEAD_DIM = 64
CONV_WIDTH = 4
LRU_C = 8.0
D_FFT = 512
FFT_GROUP_DIM = 128
N_XATTN_HEADS = 4
XATTN_HEAD_DIM = 256
N_EXPERTS = 32
TOP_K = 4
D_FF = 1024
SWIGLU_LIMIT = 7.0
SWIGLU_ALPHA = 1.702
EPS = 1e-6

LANES = 128
SUBLANES = 8
ROW_TILES = D_MODEL // LANES
TOKEN_TILE = 512
EXPERT_TILE = 512
FF_CHUNK = 256
LRU_BLOCK = 128
SCAN_CHUNK = 128
SCAN_PITCH = SCAN_CHUNK + SUBLANES
DFT_RADIX = 64
DFT_PITCH = DFT_RADIX + SUBLANES
DFT_UNROLL = 4
VMEM_LIMIT = 56 * 1024 * 1024


def _params(*sem):
    return pltpu.CompilerParams(dimension_semantics=sem, vmem_limit_bytes=VMEM_LIMIT)


def _rms(xf):
    return xf * lax.rsqrt(jnp.mean(xf * xf, axis=-1, keepdims=True) + EPS)


def _gelu_tanh(x):
    return 0.5 * x * (1.0 + jnp.tanh(math.sqrt(2.0 / math.pi) * (x + 0.044715 * (x * x * x))))


def _kv_kernel(mem_ref, g_ref, wkt_ref, wv_ref, kt_ref, v_ref):
    mn = (_rms(mem_ref[0]) * g_ref[...]).astype(BF16)
    kt = lax.dot_general(wkt_ref[...], mn, (((1,), (1,)), ((), ())), preferred_element_type=F32)
    kt_ref[0] = kt.astype(BF16)
    v_ref[0] = jnp.dot(mn, wv_ref[...], preferred_element_type=F32).astype(BF16)


def _kv_proj(mem, g, wkt, wv):
    nb, m, d = mem.shape
    return pl.pallas_call(
        _kv_kernel,
        out_shape=(jax.ShapeDtypeStruct((nb, d, m), BF16), jax.ShapeDtypeStruct((nb, m, d), BF16)),
        grid=(nb,),
        in_specs=[pl.BlockSpec((1, m, d), lambda b: (b, 0, 0)),
                  pl.BlockSpec((1, d), lambda b: (0, 0)),
                  pl.BlockSpec((d, d), lambda b: (0, 0)),
                  pl.BlockSpec((d, d), lambda b: (0, 0))],
        out_specs=(pl.BlockSpec((1, d, m), lambda b: (b, 0, 0)),
                   pl.BlockSpec((1, m, d), lambda b: (b, 0, 0))),
        compiler_params=_params("arbitrary"),
        name="kv_proj",
    )(mem, g, wkt, wv)


def _inproj_kernel(xp_ref, xs_ref, g_ref, w_ref, dft_ref, xrec_ref, gate_ref, zr_ref, zi_ref,
                   *, prompt_tiles):
    x = jnp.where(pl.program_id(0) < prompt_tiles, xp_ref[...], xs_ref[...])
    h = (_rms(x) * g_ref[...]).astype(BF16)
    proj = jnp.dot(h, w_ref[...], preferred_element_type=F32)
    xrec_ref[...] = proj[:, :D_LRU]
    gate_ref[...] = proj[:, D_LRU:2 * D_LRU].astype(BF16)
    z = jnp.dot(proj[:, 2 * D_LRU:].astype(BF16), dft_ref[...], preferred_element_type=F32)
    pad = jnp.zeros((DFT_PITCH - DFT_RADIX, D_FFT), F32)
    for m in range(x.shape[0] // DFT_RADIX):
        rows = slice(m * DFT_RADIX, (m + 1) * DFT_RADIX)
        dst = slice(m * DFT_PITCH, m * DFT_PITCH + DFT_RADIX)
        gap = slice(m * DFT_PITCH + DFT_RADIX, (m + 1) * DFT_PITCH)
        zr_ref[dst, :] = z[rows, :D_FFT]
        zi_ref[dst, :] = z[rows, D_FFT:]
        zr_ref[gap, :] = pad
        zi_ref[gap, :] = pad


def _in_proj(xp2d, xs2d, g, w_in, dft_c):
    tm = TOKEN_TILE
    ptiles = xp2d.shape[0] // tm
    t = xp2d.shape[0] + xs2d.shape[0]
    tz = t // DFT_RADIX * DFT_PITCH
    tmz = tm // DFT_RADIX * DFT_PITCH
    row = lambda i: (i, 0)
    full = lambda i: (0, 0)
    return pl.pallas_call(
        functools.partial(_inproj_kernel, prompt_tiles=ptiles),
        out_shape=(jax.ShapeDtypeStruct((t, D_LRU), F32), jax.ShapeDtypeStruct((t, D_LRU), BF16),
                   jax.ShapeDtypeStruct((tz, D_FFT), F32), jax.ShapeDtypeStruct((tz, D_FFT), F32)),
        grid=(t // tm,),
        in_specs=[pl.BlockSpec((tm, D_MODEL), lambda i: (jnp.minimum(i, ptiles - 1), 0)),
                  pl.BlockSpec((tm, D_MODEL), lambda i: (jnp.maximum(i - ptiles, 0), 0)),
                  pl.BlockSpec((1, D_MODEL), full),
                  pl.BlockSpec(w_in.shape, full), pl.BlockSpec(dft_c.shape, full)],
        out_specs=(pl.BlockSpec((tm, D_LRU), row), pl.BlockSpec((tm, D_LRU), row),
                   pl.BlockSpec((tmz, D_FFT), row), pl.BlockSpec((tmz, D_FFT), row)),
        compiler_params=_params("arbitrary"),
        name="in_proj",
    )(xp2d, xs2d, g, w_in, dft_c)


def _lru_kernel(x_ref, gate_ref, cw_ref, cb_ref, w_ref, b_ref, lam_ref, y_ref,
                af_ref, uf_ref, ab_ref, ub_ref, cf_ref, cbk_ref, *, seq):
    nch = seq // SCAN_CHUNK
    cw = cw_ref[0]
    cb = cb_ref[0]
    lam = lam_ref[0]
    sp = jnp.maximum(-lam, 0.0) + jnp.log1p(jnp.exp(-jnp.abs(lam)))
    neg_half_c_sp = (-0.5 * LRU_C) * sp
    bias = b_ref[0]
    c = LRU_BLOCK

    def gates_body(j, _):
        r0 = pl.multiple_of(j * SCAN_CHUNK, SCAN_CHUNK)
        main = x_ref[0, pl.ds(r0, SCAN_CHUNK), :]
        prev = x_ref[0, pl.ds(jnp.maximum(r0 - SUBLANES, 0), SUBLANES), :]
        nxt = x_ref[0, pl.ds(jnp.minimum(r0 + SCAN_CHUNK, seq - SUBLANES), SUBLANES), :]
        prev = jnp.where(j > 0, prev, 0.0)
        nxt = jnp.where(j < nch - 1, nxt, 0.0)
        win = jnp.concatenate([prev, main, nxt], axis=0)
        xc = cb
        for tap in range(CONV_WIDTH):
            shift = (CONV_WIDTH // 2 - tap) % win.shape[0]
            rolled = pltpu.roll(win, shift, 0) if shift else win
            xc = xc + rolled[SUBLANES:SUBLANES + SCAN_CHUNK, :] * cw[tap:tap + 1, :]
        xcb = xc.astype(BF16)
        half_xc = 0.5 * xc
        o0 = pl.multiple_of(j * SCAN_PITCH, SUBLANES)
        for d, (a_ref, u_ref) in enumerate(((af_ref, uf_ref), (ab_ref, ub_ref))):
            ga = jnp.dot(xcb, w_ref[0, :, d * c:(d + 1) * c], preferred_element_type=F32) \
                + bias[:, d * c:(d + 1) * c]
            gx = jnp.dot(xcb, w_ref[0, :, (2 + d) * c:(3 + d) * c], preferred_element_type=F32) \
                + bias[:, (2 + d) * c:(3 + d) * c]
            log_a = neg_half_c_sp[d:d + 1, :] * (1.0 + jnp.tanh(ga))
            ix = half_xc + half_xc * jnp.tanh(gx)
            a = jnp.exp(log_a)
            z = jnp.tanh(log_a) * (-1.0 - a * a)
            u = jnp.where(z > 0.0, z * lax.rsqrt(z), 0.0) * ix
            a_ref[pl.ds(o0, SCAN_CHUNK), :] = a
            u_ref[pl.ds(o0, SCAN_CHUNK), :] = u
        return 0

    lax.fori_loop(0, nch, gates_body, 0, unroll=2)

    def scan_body(t, carry):
        hf, pf, hb, pb = carry
        fwd = pl.ds(t, nch, stride=SCAN_PITCH)
        bwd = pl.ds(SCAN_CHUNK - 1 - t, nch, stride=SCAN_PITCH)
        a = af_ref[fwd, :]
        hf = a * hf + uf_ref[fwd, :]
        pf = a * pf
        uf_ref[fwd, :] = hf
        af_ref[fwd, :] = pf
        a = ab_ref[bwd, :]
        hb = a * hb + ub_ref[bwd, :]
        pb = a * pb
        ub_ref[bwd, :] = hb
        ab_ref[bwd, :] = pb
        return hf, pf, hb, pb

    zeros = jnp.zeros((nch, c), F32)
    ones = jnp.ones((nch, c), F32)
    hf, pf, hb, pb = lax.fori_loop(0, SCAN_CHUNK, scan_body, (zeros, ones, zeros, ones), unroll=4)

    carry = jnp.zeros((1, c), F32)
    cf_ref[0:1, :] = carry
    for j in range(1, nch):
        carry = hf[j - 1:j, :] + pf[j - 1:j, :] * carry
        cf_ref[j:j + 1, :] = carry
    carry = jnp.zeros((1, c), F32)
    cbk_ref[nch - 1:nch, :] = carry
    for j in range(nch - 2, -1, -1):
        carry = hb[j + 1:j + 2, :] + pb[j + 1:j + 2, :] * carry
        cbk_ref[j:j + 1, :] = carry

    def out_body(j, _):
        r0 = pl.multiple_of(j * SCAN_CHUNK, SCAN_CHUNK)
        o0 = pl.multiple_of(j * SCAN_PITCH, SUBLANES)
        rows = pl.ds(o0, SCAN_CHUNK)
        h = (uf_ref[rows, :] + af_ref[rows, :] * cf_ref[pl.ds(j, 1), :]
             + ub_ref[rows, :] + ab_ref[rows, :] * cbk_ref[pl.ds(j, 1), :])
        g = gate_ref[0, pl.ds(r0, SCAN_CHUNK), :].astype(F32)
        y_ref[0, pl.ds(r0, SCAN_CHUNK), :] = (h * _gelu_tanh(g)).astype(BF16)
        return 0

    lax.fori_loop(0, nch, out_body, 0, unroll=2)


def _lru_mix(xrec, gate, conv_w, conv_b, w_gates, b_gates, lam):
    nb, seq, _ = xrec.shape
    ncb = D_LRU // LRU_BLOCK
    c = LRU_BLOCK
    nch = seq // SCAN_CHUNK
    act = lambda b, k: (b, 0, k)
    par = lambda b, k: (k, 0, 0)
    scan_buf = pltpu.VMEM((nch * SCAN_PITCH, c), F32)
    return pl.pallas_call(
        functools.partial(_lru_kernel, seq=seq),
        out_shape=jax.ShapeDtypeStruct((nb, seq, D_LRU), BF16),
        grid=(nb, ncb),
        in_specs=[pl.BlockSpec((1, seq, c), act), pl.BlockSpec((1, seq, c), act),
                  pl.BlockSpec((1, CONV_WIDTH, c), par), pl.BlockSpec((1, 1, c), par),
                  pl.BlockSpec((1, c, 4 * c), par), pl.BlockSpec((1, 1, 4 * c), par),
                  pl.BlockSpec((1, 2, c), par)],
        out_specs=pl.BlockSpec((1, seq, c), act),
        scratch_shapes=[scan_buf, scan_buf, scan_buf, scan_buf,
                        pltpu.VMEM((nch, c), F32), pltpu.VMEM((nch, c), F32)],
        compiler_params=_params("arbitrary", "arbitrary"),
        name="lru_mix",
    )(xrec, gate, conv_w, conv_b, w_gates, b_gates, lam)


def _seqdft_kernel(zr_ref, zi_ref, f1_ref, m2_ref, y_ref, ar_ref, ai_ref):
    r, pitch, unroll = DFT_RADIX, DFT_PITCH, DFT_UNROLL
    f1 = f1_ref[...]

    def stage1(it, _):
        i2 = it * unroll
        slabs = []
        for q in range(unroll):
            rows = pl.ds(i2 + q, r, stride=pitch)
            slabs.append(jnp.concatenate([zr_ref[0, rows, :], zi_ref[0, rows, :]], axis=0))
        rhs = jnp.concatenate(slabs, axis=1).astype(BF16)
        a = jnp.dot(f1, rhs, preferred_element_type=F32)
        for q in range(unroll):
            rows = pl.ds(i2 + q, r, stride=pitch)
            ar_ref[rows, :] = a[:r, q * LANES:(q + 1) * LANES]
            ai_ref[rows, :] = a[r:, q * LANES:(q + 1) * LANES]
        return 0

    lax.fori_loop(0, r // unroll, stage1, 0, unroll=4)

    def stage2(it, _):
        for q in range(unroll):
            k1 = it * unroll + q
            src = pl.ds(pl.multiple_of(k1 * pitch, SUBLANES), r)
            slab = jnp.concatenate([ar_ref[src, :], ai_ref[src, :]], axis=0).astype(BF16)
            o = jnp.dot(m2_ref[k1], slab, preferred_element_type=F32)
            y_ref[0, pl.ds(k1, r, stride=pitch), :] = o
        return 0

    lax.fori_loop(0, r // unroll, stage2, 0, unroll=4)
    for gap in range(r, pitch):
        y_ref[0, pl.ds(gap, r, stride=pitch), :] = jnp.zeros((r, LANES), F32)


def _seq_dft(zr, zi, f1, m2):
    nb, rows, _ = zr.shape
    cw = LANES
    act = lambda b, k: (b, 0, k)
    return pl.pallas_call(
        _seqdft_kernel,
        out_shape=jax.ShapeDtypeStruct((nb, rows, D_FFT), F32),
        grid=(nb, D_FFT // cw),
        in_specs=[pl.BlockSpec((1, rows, cw), act), pl.BlockSpec((1, rows, cw), act),
                  pl.BlockSpec(f1.shape, lambda b, k: (0, 0)),
                  pl.BlockSpec(m2.shape, lambda b, k: (0, 0, 0))],
        out_specs=pl.BlockSpec((1, rows, cw), act),
        scratch_shapes=[pltpu.VMEM((rows, cw), F32), pltpu.VMEM((rows, cw), F32)],
        compiler_params=_params("arbitrary", "arbitrary"),
        name="seq_dft",
    )(zr, zi, f1, m2)


def _dft_tables(seq):
    n1 = n2 = DFT_RADIX
    assert n1 * n2 == seq, "sequence DFT is factored as DFT_RADIX x DFT_RADIX"
    g = FFT_GROUP_DIM
    ang = 2.0 * np.pi * np.outer(np.arange(g), np.arange(g)) / g
    cg, sg = np.cos(ang) / math.sqrt(g), np.sin(ang) / math.sqrt(g)
    ngroups = D_FFT // g
    dft_c = np.zeros((D_FFT, 2 * D_FFT))
    for q in range(ngroups):
        dft_c[q * g:(q + 1) * g, q * g:(q + 1) * g] = cg
        dft_c[q * g:(q + 1) * g, D_FFT + q * g:D_FFT + (q + 1) * g] = -sg
    ang1 = 2.0 * np.pi * np.outer(np.arange(n1), np.arange(n1)) / n1
    c1, s1 = np.cos(ang1) / math.sqrt(n1), np.sin(ang1) / math.sqrt(n1)
    f1 = np.block([[c1, s1], [-s1, c1]])
    k1 = np.arange(n1)[:, None, None]
    k2 = np.arange(n2)[None, :, None]
    i2 = np.arange(n2)[None, None, :]
    ang2 = 2.0 * np.pi * (i2 * k2 / n2 + i2 * k1 / seq)
    m2 = np.concatenate([np.cos(ang2), np.sin(ang2)], axis=2) / math.sqrt(n2)
    return (jnp.asarray(dft_c, F32), jnp.asarray(f1, F32), jnp.asarray(m2, F32))


def _flat_smem_copies(src_ref, smem_ref, base, sem):
    nk, nc, nl = src_ref.shape
    return [pltpu.make_async_copy(src_ref.at[k, c],
                                  smem_ref.at[pl.ds(base + (k * nc + c) * nl, nl)], sem)
            for k in range(nk) for c in range(nc)]


def _attn_route_kernel(xp_ref, xs_ref, ylru_ref, yfft_ref, kt_ref, v_ref, gl_ref, gf_ref, wout_ref,
                       gx_ref, wq_ref, wo_ref, gffn_ref, wrh_ref, wrl_ref, br_ref,
                       x2_ref, xs_out_ref, slot_ref, gate_ref, cnt_ref, h3tl_ref, slot_vmem, slot_smem,
                       ssem, *, prompt_tiles, tiles):
    g = pl.program_id(0)
    tm = xp_ref.shape[1]
    n = TOP_K * tm
    cur = lax.rem(g, 2)
    prev = 1 - cur

    @pl.when(g == 0)
    def _():
        h3tl_ref[1] = jnp.zeros(h3tl_ref.shape[1:], F32)

        def clear(j, _):
            slot_smem[n + j] = 0
            return 0

        lax.fori_loop(0, n, clear, 0)

    @pl.when(g >= 1)
    def _():
        for cp in _flat_smem_copies(slot_vmem.at[prev], slot_smem, prev * n, ssem.at[prev]):
            cp.wait()

    for t in range(tm):
        row = h3tl_ref[prev, t * ROW_TILES:(t + 1) * ROW_TILES, :]
        for k in range(TOP_K):
            dst = pl.multiple_of(slot_smem[prev * n + k * tm + t], ROW_TILES)
            xs_out_ref[pl.ds(dst, ROW_TILES), :] = row

    x = jnp.where(g < prompt_tiles, xp_ref[0], xs_ref[0])
    m_lru = _rms(ylru_ref[0].astype(F32)) * gl_ref[...]
    yfft = jnp.concatenate([yfft_ref[0, m * DFT_PITCH:m * DFT_PITCH + DFT_RADIX, :]
                            for m in range(tm // DFT_RADIX)], axis=0)
    m_fft = _rms(yfft) * gf_ref[...]
    mixed = jnp.concatenate([m_lru, m_fft], axis=-1).astype(BF16)
    x1 = x + jnp.dot(mixed, wout_ref[...], preferred_element_type=F32)

    hq = (_rms(x1) * gx_ref[...]).astype(BF16)
    q = jnp.dot(hq, wq_ref[...], preferred_element_type=F32) * (XATTN_HEAD_DIM ** -0.5)
    qb = q.astype(BF16)
    heads = []
    for h in range(N_XATTN_HEADS):
        sl = slice(h * XATTN_HEAD_DIM, (h + 1) * XATTN_HEAD_DIM)
        s = jnp.dot(qb[:, sl], kt_ref[0, sl, :], preferred_element_type=F32)
        p = jnp.exp(s - jnp.max(s, axis=-1, keepdims=True))
        inv = 1.0 / jnp.sum(p, axis=-1, keepdims=True)
        o = jnp.dot(p.astype(BF16), v_ref[0, :, sl], preferred_element_type=F32) * inv
        heads.append(o.astype(BF16))
    att = jnp.concatenate(heads, axis=-1)
    x2 = x1 + jnp.dot(att, wo_ref[...], preferred_element_type=F32)
    x2_ref[0] = x2

    h3 = _rms(x2) * gffn_ref[...]
    for s in range(ROW_TILES):
        h3tl_ref[cur, pl.ds(s, tm, stride=ROW_TILES), :] = h3[:, s * LANES:(s + 1) * LANES]

    h_hi = h3.astype(BF16)
    h_lo = (h3 - h_hi.astype(F32)).astype(BF16)
    logits = (jnp.dot(h_hi, wrh_ref[...], preferred_element_type=F32)
              + jnp.dot(h_lo, wrh_ref[...], preferred_element_type=F32)
              + jnp.dot(h_hi, wrl_ref[...], preferred_element_type=F32)) + br_ref[...]
    lg = logits.T[:N_EXPERTS, :]
    e_iota = lax.broadcasted_iota(I32, lg.shape, 0)
    vals, onehots = [], []
    for k in range(TOP_K):
        m = jnp.max(lg, axis=0, keepdims=True)
        idx = jnp.min(jnp.where(lg == m, e_iota, N_EXPERTS), axis=0, keepdims=True)
        oh = e_iota == idx
        vals.append(m)
        onehots.append(oh)
        lg = jnp.where(oh, -jnp.inf, lg)
    exps = [jnp.exp(v - vals[0]) for v in vals]
    inv = 1.0 / (exps[0] + exps[1] + exps[2] + exps[3])

    def lane_tiles(rows):
        return jnp.stack([jnp.concatenate([r[:, c * LANES:(c + 1) * LANES]
                                           for c in range(tm // LANES)], axis=0) for r in rows])

    gate_ref[0] = lane_tiles([e * inv for e in exps])

    member = jnp.zeros(lg.shape, F32)
    for oh in onehots:
        member = member + oh.astype(F32)
    mb = member.astype(BF16)
    r_i = lax.broadcasted_iota(I32, (tm, tm), 0)
    c_i = lax.broadcasted_iota(I32, (tm, tm), 1)
    earlier = jnp.where(r_i < c_i, 1.0, 0.0).astype(BF16)
    tok_rank = jnp.dot(mb, earlier, preferred_element_type=F32)
    er = lax.broadcasted_iota(I32, (N_EXPERTS, N_EXPERTS), 0)
    ec = lax.broadcasted_iota(I32, (N_EXPERTS, N_EXPERTS), 1)
    lower = jnp.where(ec < er, 1.0, 0.0).astype(BF16)
    exp_off = jnp.sum(jnp.dot(lower, mb, preferred_element_type=F32), axis=1, keepdims=True)
    pos = exp_off + tok_rank
    slots = [jnp.sum(jnp.where(oh, pos, 0.0), axis=0, keepdims=True) for oh in onehots]
    slot_tab = lane_tiles(slots).astype(I32) * ROW_TILES
    slot_ref[0] = slot_tab
    slot_vmem[cur] = slot_tab
    ones = jnp.ones((SUBLANES, tm), BF16)
    cnt = lax.dot_general(ones, mb, (((1,), (1,)), ((), ())), preferred_element_type=F32)
    cnt_ref[0] = cnt[0:1, :].astype(I32)

    @pl.when(g < tiles)
    def _():
        for cp in _flat_smem_copies(slot_vmem.at[cur], slot_smem, cur * n, ssem.at[cur]):
            cp.start()


def _attn_route(xp, xs, ylru, yfft, kt, v, gl, gf, w_out, gx, wq, wo, gffn, wr_hi, wr_lo, br):
    nbp, seq, d = xp.shape
    nb = nbp + xs.shape[0]
    tm = TOKEN_TILE
    tmz = tm // DFT_RADIX * DFT_PITCH
    nt = seq // tm
    m = v.shape[1]
    tiles = nb * nt
    ptiles = nbp * nt
    tile = lambda g: jnp.minimum(g, tiles - 1)
    tok = lambda g: (tile(g) // nt, tile(g) % nt, 0)
    full2 = lambda g: (0, 0)
    per_b = lambda g: (tile(g) // nt, 0, 0)
    tile_id = lambda g: (tile(g), 0, 0)
    tile_id4 = lambda g: (tile(g), 0, 0, 0)
    xp_idx = lambda g: (jnp.minimum(g, ptiles - 1) // nt, jnp.minimum(g, ptiles - 1) % nt, 0)
    xs_idx = lambda g: (jnp.maximum(tile(g) - ptiles, 0) // nt, jnp.maximum(tile(g) - ptiles, 0) % nt, 0)
    nw = wr_hi.shape[1]
    per_tok = (TOP_K, tm // LANES, LANES)
    return pl.pallas_call(
        functools.partial(_attn_route_kernel, prompt_tiles=ptiles, tiles=tiles),
        out_shape=(jax.ShapeDtypeStruct((nb, seq, d), F32),
                   jax.ShapeDtypeStruct((tiles * tm * TOP_K * ROW_TILES, LANES), F32),
                   jax.ShapeDtypeStruct((tiles,) + per_tok, I32),
                   jax.ShapeDtypeStruct((tiles,) + per_tok, F32),
                   jax.ShapeDtypeStruct((tiles, 1, N_EXPERTS), I32)),
        grid=(tiles + 1,),
        in_specs=[pl.BlockSpec((1, tm, d), xp_idx),
                  pl.BlockSpec((1, tm, d), xs_idx),
                  pl.BlockSpec((1, tm, D_LRU), tok),
                  pl.BlockSpec((1, tmz, D_FFT), tok),
                  pl.BlockSpec((1, d, m), per_b), pl.BlockSpec((1, m, d), per_b),
                  pl.BlockSpec((1, D_LRU), full2), pl.BlockSpec((1, D_FFT), full2),
                  pl.BlockSpec((d, d), full2), pl.BlockSpec((1, d), full2),
                  pl.BlockSpec((d, d), full2), pl.BlockSpec((d, d), full2),
                  pl.BlockSpec((1, d), full2), pl.BlockSpec((d, nw), full2),
                  pl.BlockSpec((d, nw), full2), pl.BlockSpec((1, nw), full2)],
        out_specs=(pl.BlockSpec((1, tm, d), tok),
                   pl.BlockSpec((tm * TOP_K * ROW_TILES, LANES), lambda g: (jnp.maximum(g - 1, 0), 0)),
                   pl.BlockSpec((1,) + per_tok, tile_id4), pl.BlockSpec((1,) + per_tok, tile_id4),
                   pl.BlockSpec((1, 1, N_EXPERTS), tile_id)),
        scratch_shapes=[pltpu.VMEM((2, tm * ROW_TILES, LANES), F32),
                        pltpu.VMEM((2,) + per_tok, I32),
                        pltpu.SMEM((2 * TOP_K * tm,), I32), pltpu.SemaphoreType.DMA((2,))],
        compiler_params=_params("arbitrary"),
        name="attn_route",
    )(xp, xs, ylru, yfft, kt, v, gl, gf, w_out, gx, wq, wo, gffn, wr_hi, wr_lo, br)


def _for_each_run(blk, bexp_ref, ostart_ref, t0_ref, t1_ref, cum_ref, cnt_ref, off_ref, fn):
    e = bexp_ref[blk]
    o0 = ostart_ref[blk]

    def body(i, _):
        j = i * N_EXPERTS + e
        c = cum_ref[j]
        lo = jnp.maximum(o0, c)
        hi = jnp.minimum(o0 + EXPERT_TILE, c + cnt_ref[j])

        @pl.when(hi > lo)
        def _():
            fn(i * (TOKEN_TILE * TOP_K) + off_ref[j] + (lo - c), lo - o0, hi - lo)
        return 0

    lax.fori_loop(t0_ref[blk], t1_ref[blk] + 1, body, 0)


def _rows(ref, row, n_rows):
    return ref.at[pl.ds(pl.multiple_of(row * ROW_TILES, ROW_TILES), n_rows * ROW_TILES), :]


def _experts_kernel(bexp_ref, bvalid_ref, nblk_ref, ostart_ref, t0_ref, t1_ref, cum_ref, cnt_ref,
                    off_ref, xs_hbm, wg_ref, bg_ref, wu_ref, bu_ref, wd_ref, bd_ref, ys_hbm,
                    xbuf, ybuf, act_ref, wgb_ref, wub_ref, wdb_ref, gsem, osem):
    b = pl.program_id(0)
    nblk = nblk_ref[0]
    slot = lax.rem(b, 2)
    tabs = (bexp_ref, ostart_ref, t0_ref, t1_ref, cum_ref, cnt_ref, off_ref)

    def gather(blk, sl):
        _for_each_run(blk, *tabs, lambda src, dst, n: pltpu.make_async_copy(
            _rows(xs_hbm, src, n), _rows(xbuf.at[sl], dst, n), gsem.at[sl]).start())

    def wait_in(blk, sl):
        n = bvalid_ref[blk]
        pltpu.make_async_copy(_rows(xs_hbm, 0, n), _rows(xbuf.at[sl], 0, n), gsem.at[sl]).wait()

    def wait_out(blk, sl):
        n = bvalid_ref[blk]
        pltpu.make_async_copy(_rows(ybuf.at[sl], 0, n), _rows(ys_hbm, 0, n), osem.at[sl]).wait()

    @pl.when(b == 0)
    def _():
        gather(0, 0)

    @pl.when(b + 1 < nblk)
    def _():
        gather(b + 1, 1 - slot)

    @pl.when(b < nblk)
    def _():
        wait_in(b, slot)

        @pl.when(b >= 2)
        def _():
            wait_out(b - 2, slot)

        @pl.when(jnp.logical_or(b == 0, bexp_ref[b] != bexp_ref[jnp.maximum(b - 1, 0)]))
        def _():
            def cast(r, _):
                rows = pl.ds(pl.multiple_of(r * LANES, LANES), LANES)
                wgb_ref[rows, :] = wg_ref[0, rows, :].astype(BF16)
                wub_ref[rows, :] = wu_ref[0, rows, :].astype(BF16)
                wdb_ref[rows, :] = wd_ref[0, rows, :].astype(BF16)
                return 0

            lax.fori_loop(0, D_MODEL // LANES, cast, 0)

        tm = EXPERT_TILE
        x = jnp.concatenate([xbuf[slot, pl.ds(s, tm, stride=ROW_TILES), :]
                             for s in range(ROW_TILES)], axis=-1)
        row = lax.broadcasted_iota(I32, (tm, 1), 0)
        xb = jnp.where(row < bvalid_ref[b], x, 0.0).astype(BF16)
        for cidx in range(D_FF // FF_CHUNK):
            sl = slice(cidx * FF_CHUNK, (cidx + 1) * FF_CHUNK)
            gt = jnp.dot(xb, wgb_ref[:, sl], preferred_element_type=F32) + bg_ref[0, :, sl]
            up = jnp.dot(xb, wub_ref[:, sl], preferred_element_type=F32) + bu_ref[0, :, sl]
            gt = jnp.minimum(gt, SWIGLU_LIMIT)
            up = jnp.clip(up, -SWIGLU_LIMIT, SWIGLU_LIMIT)
            hg = 0.5 * gt
            act_ref[:, sl] = ((hg + hg * jnp.tanh(SWIGLU_ALPHA * hg)) * (up + 1.0)).astype(BF16)
        for cidx in range(D_MODEL // FF_CHUNK):
            sl = slice(cidx * FF_CHUNK, (cidx + 1) * FF_CHUNK)
            o = jnp.dot(act_ref[...], wdb_ref[:, sl], preferred_element_type=F32) + bd_ref[0, :, sl]
            for q in range(FF_CHUNK // LANES):
                s = cidx * (FF_CHUNK // LANES) + q
                ybuf[slot, pl.ds(s, tm, stride=ROW_TILES), :] = o[:, q * LANES:(q + 1) * LANES]
        _for_each_run(b, *tabs, lambda dst, src, n: pltpu.make_async_copy(
            _rows(ybuf.at[slot], src, n), _rows(ys_hbm, dst, n), osem.at[slot]).start())

        @pl.when(b == nblk - 1)
        def _():
            wait_out(b, slot)

            @pl.when(b >= 1)
            def _():
                wait_out(b - 1, 1 - slot)


def _moe_experts(tables, xs, wg, bg, wu, bu, wd, bd):
    bexp = tables[0]
    nblocks = bexp.shape[0]
    tm = EXPERT_TILE
    wsel = lambda i, be, bv, nb, *_: (be[jnp.minimum(i, nb[0] - 1)], 0, 0)
    wspec = pl.BlockSpec((1, D_MODEL, D_FF), wsel)
    bspec = pl.BlockSpec((1, 1, D_FF), wsel)
    hbm = pl.BlockSpec(memory_space=pl.ANY)
    return pl.pallas_call(
        _experts_kernel,
        out_shape=jax.ShapeDtypeStruct(xs.shape, F32),
        grid_spec=pltpu.PrefetchScalarGridSpec(
            num_scalar_prefetch=len(tables),
            grid=(nblocks,),
            in_specs=[hbm, wspec, bspec, wspec, bspec, wspec, bspec],
            out_specs=hbm,
            scratch_shapes=[pltpu.VMEM((2, tm * ROW_TILES, LANES), F32),
                            pltpu.VMEM((2, tm * ROW_TILES, LANES), F32),
                            pltpu.VMEM((tm, D_FF), BF16),
                            pltpu.VMEM((D_MODEL, D_FF), BF16), pltpu.VMEM((D_MODEL, D_FF), BF16),
                            pltpu.VMEM((D_FF, D_MODEL), BF16),
                            pltpu.SemaphoreType.DMA((2,)), pltpu.SemaphoreType.DMA((2,))]),
        compiler_params=_params("arbitrary"),
        name="moe_experts",
    )(*tables, xs, wg, bg, wu, bu, wd, bd)


def _combine_kernel(slot_hbm, gate_hbm, ys_ref, x2_ref, gfin_ref, outp_ref, outs_ref,
                    slot_smem, gate_smem, acc_ref, isem, *, prompt_tiles):
    i = pl.program_id(0)
    tm = x2_ref.shape[0]
    n = TOP_K * tm
    half = lax.rem(i, 2)
    base = half * n

    def table_copies(tile, hf):
        return (_flat_smem_copies(slot_hbm.at[tile], slot_smem, hf * n, isem.at[hf])
                + _flat_smem_copies(gate_hbm.at[tile], gate_smem, hf * n, isem.at[hf]))

    @pl.when(i == 0)
    def _():
        for cp in table_copies(0, 0):
            cp.start()

    @pl.when(i + 1 < pl.num_programs(0))
    def _():
        for cp in table_copies(i + 1, 1 - half):
            cp.start()

    x2 = x2_ref[...]
    for s in range(ROW_TILES):
        acc_ref[pl.ds(s, tm, stride=ROW_TILES), :] = x2[:, s * LANES:(s + 1) * LANES]
    for cp in table_copies(i, half):
        cp.wait()

    def body(t, _):
        rows = pl.ds(pl.multiple_of(t * ROW_TILES, ROW_TILES), ROW_TILES)
        acc = acc_ref[rows, :]
        for k in range(TOP_K):
            src = pl.multiple_of(slot_smem[base + k * tm + t], ROW_TILES)
            acc = acc + gate_smem[base + k * tm + t] * ys_ref[pl.ds(src, ROW_TILES), :]
        acc_ref[rows, :] = acc
        return 0

    lax.fori_loop(0, tm, body, 0, unroll=8)
    acc = jnp.concatenate([acc_ref[pl.ds(s, tm, stride=ROW_TILES), :] for s in range(ROW_TILES)],
                          axis=-1)
    y = _rms(acc) * gfin_ref[...]

    @pl.when(i < prompt_tiles)
    def _():
        outp_ref[...] = y

    @pl.when(i >= prompt_tiles)
    def _():
        outs_ref[...] = y


def _moe_combine(slots, gates, ys, x2, gfin, t_prompt):
    t, d = x2.shape
    tm = TOKEN_TILE
    ptiles = t_prompt // tm
    return pl.pallas_call(
        functools.partial(_combine_kernel, prompt_tiles=ptiles),
        out_shape=(jax.ShapeDtypeStruct((t_prompt, d), F32),
                   jax.ShapeDtypeStruct((t - t_prompt, d), F32)),
        grid=(t // tm,),
        in_specs=[pl.BlockSpec(memory_space=pl.ANY), pl.BlockSpec(memory_space=pl.ANY),
                  pl.BlockSpec((tm * TOP_K * ROW_TILES, LANES), lambda i: (i, 0)),
                  pl.BlockSpec((tm, d), lambda i: (i, 0)),
                  pl.BlockSpec((1, d), lambda i: (0, 0))],
        out_specs=(pl.BlockSpec((tm, d), lambda i: (jnp.minimum(i, ptiles - 1), 0)),
                   pl.BlockSpec((tm, d), lambda i: (jnp.maximum(i - ptiles, 0), 0))),
        scratch_shapes=[pltpu.SMEM((2 * TOP_K * tm,), I32), pltpu.SMEM((2 * TOP_K * tm,), F32),
                        pltpu.VMEM((tm * ROW_TILES, LANES), F32),
                        pltpu.SemaphoreType.DMA((2,))],
        compiler_params=_params("arbitrary"),
        name="moe_combine",
    )(slots, gates, ys, x2, gfin)


def _lru_gate_weights(w_a, b_a, w_x, b_x):
    hpb = LRU_BLOCK // LRU_HEAD_DIM
    ncb = D_LRU // LRU_BLOCK

    def blockdiag(w):
        w = w.reshape(ncb, hpb, LRU_HEAD_DIM, LRU_HEAD_DIM)
        eye = jnp.eye(hpb, dtype=w.dtype)
        full = jnp.einsum('nhij,hg->nhigj', w, eye)
        return full.reshape(ncb, LRU_BLOCK, LRU_BLOCK)

    mats = [blockdiag(w_a[0]), blockdiag(w_a[1]), blockdiag(w_x[0]), blockdiag(w_x[1])]
    w = (0.5 * jnp.concatenate(mats, axis=-1)).astype(BF16)
    bs = [b.reshape(ncb, 1, LRU_BLOCK) for b in (b_a[0], b_a[1], b_x[0], b_x[1])]
    return w, (0.5 * jnp.concatenate(bs, axis=-1)).astype(F32)


def _encoder_layer(xp, xs, mem, norm_mix, w_in, conv_w, conv_b, lru_w_a, lru_b_a, lru_w_x, lru_b_x,
                   lru_lambda, norm_lru_out, norm_fft_out, w_out, norm_xattn, norm_mem, w_q, w_kv,
                   w_o, norm_ffn, w_router, b_router, w_gate, b_gate, w_up, b_up, w_down, b_down,
                   norm_final):
    nbp, seq, d = xp.shape
    nb = nbp + xs.shape[0]
    t = nb * seq
    seqz = seq // DFT_RADIX * DFT_PITCH
    ncb = D_LRU // LRU_BLOCK
    row = lambda v: v.reshape(1, -1).astype(F32)

    dft_c, f1, m2 = _dft_tables(seq)
    kt, v = _kv_proj(mem, row(norm_mem), w_kv[:, :d].T.astype(BF16), w_kv[:, d:].astype(BF16))

    xrec, gate, zr, zi = _in_proj(xp.reshape(nbp * seq, d), xs.reshape(t - nbp * seq, d),
                                  row(norm_mix), w_in.astype(BF16), dft_c.astype(BF16))

    w_gates, b_gates = _lru_gate_weights(lru_w_a, lru_b_a, lru_w_x, lru_b_x)
    cw = conv_w.reshape(CONV_WIDTH, ncb, LRU_BLOCK).transpose(1, 0, 2)
    cb = conv_b.reshape(ncb, 1, LRU_BLOCK)
    lam = lru_lambda.reshape(2, ncb, LRU_BLOCK).transpose(1, 0, 2)
    ylru = _lru_mix(xrec.reshape(nb, seq, D_LRU), gate.reshape(nb, seq, D_LRU), cw, cb,
                    w_gates, b_gates, lam)

    yfft = _seq_dft(zr.reshape(nb, seqz, D_FFT), zi.reshape(nb, seqz, D_FFT),
                    f1.astype(BF16), m2.astype(BF16))

    w_r = jnp.pad(w_router.astype(F32), ((0, 0), (0, LANES - N_EXPERTS)))
    w_r_hi = w_r.astype(BF16)
    w_r_lo = (w_r - w_r_hi.astype(F32)).astype(BF16)
    b_r = jnp.pad(b_router.astype(F32).reshape(1, -1), ((0, 0), (0, LANES - N_EXPERTS)))
    x2, xs_rows, slots, gates, tile_cnt = _attn_route(
        xp, xs, ylru, yfft, kt, v, row(norm_lru_out), row(norm_fft_out), w_out.astype(BF16),
        row(norm_xattn), w_q.astype(BF16), w_o.astype(BF16), row(norm_ffn), w_r_hi, w_r_lo, b_r)

    tmb = EXPERT_TILE
    nblocks = -(-(t * TOP_K) // tmb) + N_EXPERTS
    tile_cnt = tile_cnt.reshape(-1, N_EXPERTS)
    cum_end = jnp.cumsum(tile_cnt, axis=0)
    cum = cum_end - tile_cnt
    off = jnp.cumsum(tile_cnt, axis=1) - tile_cnt
    counts = cum_end[-1]
    padded = ((counts + tmb - 1) // tmb) * tmb
    pend = jnp.cumsum(padded)
    pstart = pend - padded
    blk_start = jnp.arange(nblocks, dtype=I32) * tmb
    bexp = jnp.minimum(jnp.sum((pend[None, :] <= blk_start[:, None]).astype(I32), axis=1),
                       N_EXPERTS - 1)
    ostart = blk_start - pstart[bexp]
    bvalid = jnp.clip(counts[bexp] - ostart, 0, tmb).astype(I32)
    nblk = (pend[-1:] // tmb).astype(I32)
    cum_b = jnp.take(cum, bexp, axis=1)
    end_b = jnp.take(cum_end, bexp, axis=1)
    ntiles = tile_cnt.shape[0]
    t0 = jnp.minimum(jnp.sum((end_b <= ostart[None, :]).astype(I32), axis=0), ntiles - 1)
    t1 = jnp.maximum(jnp.sum((cum_b < (ostart + tmb)[None, :]).astype(I32), axis=0) - 1, 0)
    tables = tuple(a.astype(I32) for a in (bexp, bvalid, nblk, ostart, t0, t1, cum.reshape(-1),
                                           tile_cnt.reshape(-1), off.reshape(-1)))

    ys = _moe_experts(tables, xs_rows,
                      w_gate, b_gate.reshape(N_EXPERTS, 1, D_FF),
                      w_up, b_up.reshape(N_EXPERTS, 1, D_FF),
                      w_down, b_down.reshape(N_EXPERTS, 1, D_MODEL))
    outp, outs = _moe_combine(slots, gates, ys, x2.reshape(t, d), row(norm_final), nbp * seq)
    return outp.reshape(nbp, seq, d), outs.reshape(nb - nbp, seq, d)


def kernel(x_prompt, x_sample, mem_prompt, mem_sample, norm_mix, w_in, conv_w, conv_b, lru_w_a,
           lru_b_a, lru_w_x, lru_b_x, lru_lambda, norm_lru_out, norm_fft_out, w_out, norm_xattn,
           norm_mem, w_q, w_kv, w_o, norm_ffn, w_router, b_router, w_gate, b_gate, w_up, b_up,
           w_down, b_down, norm_final):
    assert x_prompt.shape[1:] == x_sample.shape[1:], "both groups must share (SEQ, D_MODEL)"
    assert w_in.shape[0] == 1, "single-layer block"
    mem = jnp.concatenate([mem_prompt, mem_sample], axis=0)
    return _encoder_layer(x_prompt, x_sample, mem, norm_mix[0], w_in[0], conv_w[0], conv_b[0],
                          lru_w_a[0], lru_b_a[0], lru_w_x[0], lru_b_x[0], lru_lambda[0],
                          norm_lru_out[0], norm_fft_out[0], w_out[0], norm_xattn[0], norm_mem[0],
                          w_q[0], w_kv[0], w_o[0], norm_ffn[0], w_router[0], b_router[0],
                          w_gate[0], b_gate[0], w_up[0], b_up[0], w_down[0], b_down[0], norm_final)
```

```python
import functools
import math

import numpy as np
import jax
import jax.numpy as jnp
from jax import lax
from jax.experimental import pallas as pl
from jax.experimental.pallas import tpu as pltpu

F32, BF16, I32 = jnp.float32, jnp.bfloat16, jnp.int32

D_MODEL = 1024
D_LRU = 512
LRU_HEAD_DIM = 64
CONV_WIDTH = 4
LRU_C = 8.0
D_FFT = 512
FFT_GROUP_DIM = 128
N_XATTN_HEADS = 4
XATTN_HEAD_DIM = 256
N_EXPERTS = 32
TOP_K = 4
D_FF = 1024
SWIGLU_LIMIT = 7.0
SWIGLU_ALPHA = 1.702
EPS = 1e-6

LANES = 128
SUBLANES = 8
ROW_TILES = D_MODEL // LANES
TOKEN_TILE = 512
EXPERT_TILE = 512
FF_CHUNK = 256
LRU_BLOCK = 128
SCAN_CHUNK = 128
SCAN_PITCH = SCAN_CHUNK + SUBLANES
DFT_RADIX = 64
DFT_PITCH = DFT_RADIX + SUBLANES
DFT_UNROLL = 4
VMEM_LIMIT = 56 * 1024 * 1024


def _params(*sem):
    return pltpu.CompilerParams(dimension_semantics=sem, vmem_limit_bytes=VMEM_LIMIT)


def _rms(xf):
    return xf * lax.rsqrt(jnp.mean(xf * xf, axis=-1, keepdims=True) + EPS)


def _gelu_tanh(x):
    return 0.5 * x * (1.0 + jnp.tanh(math.sqrt(2.0 / math.pi) * (x + 0.044715 * (x * x * x))))


def _kv_kernel(mem_ref, g_ref, wkt_ref, wv_ref, kt_ref, v_ref):
    mn = (_rms(mem_ref[0]) * g_ref[...]).astype(BF16)
    kt = lax.dot_general(wkt_ref[...], mn, (((1,), (1,)), ((), ())), preferred_element_type=F32)
    kt_ref[0] = kt.astype(BF16)
    v_ref[0] = jnp.dot(mn, wv_ref[...], preferred_element_type=F32).astype(BF16)


def _kv_proj(mem, g, wkt, wv):
    nb, m, d = mem.shape
    return pl.pallas_call(
        _kv_kernel,
        out_shape=(jax.ShapeDtypeStruct((nb, d, m), BF16), jax.ShapeDtypeStruct((nb, m, d), BF16)),
        grid=(nb,),
        in_specs=[pl.BlockSpec((1, m, d), lambda b: (b, 0, 0)),
                  pl.BlockSpec((1, d), lambda b: (0, 0)),
                  pl.BlockSpec((d, d), lambda b: (0, 0)),
                  pl.BlockSpec((d, d), lambda b: (0, 0))],
        out_specs=(pl.BlockSpec((1, d, m), lambda b: (b, 0, 0)),
                   pl.BlockSpec((1, m, d), lambda b: (b, 0, 0))),
        compiler_params=_params("arbitrary"),
        name="kv_proj",
    )(mem, g, wkt, wv)


def _inproj_kernel(xp_ref, xs_ref, g_ref, w_ref, dft_ref, xrec_ref, gate_ref, zr_ref, zi_ref,
                   *, prompt_tiles):
    x = jnp.where(pl.program_id(0) < prompt_tiles, xp_ref[...], xs_ref[...])
    h = (_rms(x) * g_ref[...]).astype(BF16)
    proj = jnp.dot(h, w_ref[...], preferred_element_type=F32)
    xrec_ref[...] = proj[:, :D_LRU]
    gate_ref[...] = proj[:, D_LRU:2 * D_LRU].astype(BF16)
    z = jnp.dot(proj[:, 2 * D_LRU:].astype(BF16), dft_ref[...], preferred_element_type=F32)
    pad = jnp.zeros((DFT_PITCH - DFT_RADIX, D_FFT), F32)
    for m in range(x.shape[0] // DFT_RADIX):
        rows = slice(m * DFT_RADIX, (m + 1) * DFT_RADIX)
        dst = slice(m * DFT_PITCH, m * DFT_PITCH + DFT_RADIX)
        gap = slice(m * DFT_PITCH + DFT_RADIX, (m + 1) * DFT_PITCH)
        zr_ref[dst, :] = z[rows, :D_FFT]
        zi_ref[dst, :] = z[rows, D_FFT:]
        zr_ref[gap, :] = pad
        zi_ref[gap, :] = pad


def _in_proj(xp2d, xs2d, g, w_in, dft_c):
    tm = TOKEN_TILE
    ptiles = xp2d.shape[0] // tm
    t = xp2d.shape[0] + xs2d.shape[0]
    tz = t // DFT_RADIX * DFT_PITCH
    tmz = tm // DFT_RADIX * DFT_PITCH
    row = lambda i: (i, 0)
    full = lambda i: (0, 0)
    return pl.pallas_call(
        functools.partial(_inproj_kernel, prompt_tiles=ptiles),
        out_shape=(jax.ShapeDtypeStruct((t, D_LRU), F32), jax.ShapeDtypeStruct((t, D_LRU), BF16),
                   jax.ShapeDtypeStruct((tz, D_FFT), F32), jax.ShapeDtypeStruct((tz, D_FFT), F32)),
        grid=(t // tm,),
        in_specs=[pl.BlockSpec((tm, D_MODEL), lambda i: (jnp.minimum(i, ptiles - 1), 0)),
                  pl.BlockSpec((tm, D_MODEL), lambda i: (jnp.maximum(i - ptiles, 0), 0)),
                  pl.BlockSpec((1, D_MODEL), full),
                  pl.BlockSpec(w_in.shape, full), pl.BlockSpec(dft_c.shape, full)],
        out_specs=(pl.BlockSpec((tm, D_LRU), row), pl.BlockSpec((tm, D_LRU), row),
                   pl.BlockSpec((tmz, D_FFT), row), pl.BlockSpec((tmz, D_FFT), row)),
        compiler_params=_params("arbitrary"),
        name="in_proj",
    )(xp2d, xs2d, g, w_in, dft_c)


def _lru_kernel(x_ref, gate_ref, cw_ref, cb_ref, w_ref, b_ref, lam_ref, y_ref,
                af_ref, uf_ref, ab_ref, ub_ref, cf_ref, cbk_ref, *, seq):
    nch = seq // SCAN_CHUNK
    cw = cw_ref[0]
    cb = cb_ref[0]
    lam = lam_ref[0]
    sp = jnp.maximum(-lam, 0.0) + jnp.log1p(jnp.exp(-jnp.abs(lam)))
    neg_half_c_sp = (-0.5 * LRU_C) * sp
    bias = b_ref[0]
    c = LRU_BLOCK

    def gates_body(j, _):
        r0 = pl.multiple_of(j * SCAN_CHUNK, SCAN_CHUNK)
        main = x_ref[0, pl.ds(r0, SCAN_CHUNK), :]
        prev = x_ref[0, pl.ds(jnp.maximum(r0 - SUBLANES, 0), SUBLANES), :]
        nxt = x_ref[0, pl.ds(jnp.minimum(r0 + SCAN_CHUNK, seq - SUBLANES), SUBLANES), :]
        prev = jnp.where(j > 0, prev, 0.0)
        nxt = jnp.where(j < nch - 1, nxt, 0.0)
        win = jnp.concatenate([prev, main, nxt], axis=0)
        xc = cb
        for tap in range(CONV_WIDTH):
            shift = (CONV_WIDTH // 2 - tap) % win.shape[0]
            rolled = pltpu.roll(win, shift, 0) if shift else win
            xc = xc + rolled[SUBLANES:SUBLANES + SCAN_CHUNK, :] * cw[tap:tap + 1, :]
        xcb = xc.astype(BF16)
        half_xc = 0.5 * xc
        o0 = pl.multiple_of(j * SCAN_PITCH, SUBLANES)
        for d, (a_ref, u_ref) in enumerate(((af_ref, uf_ref), (ab_ref, ub_ref))):
            ga = jnp.dot(xcb, w_ref[0, :, d * c:(d + 1) * c], preferred_element_type=F32) \
                + bias[:, d * c:(d + 1) * c]
            gx = jnp.dot(xcb, w_ref[0, :, (2 + d) * c:(3 + d) * c], preferred_element_type=F32) \
                + bias[:, (2 + d) * c:(3 + d) * c]
            log_a = neg_half_c_sp[d:d + 1, :] * (1.0 + jnp.tanh(ga))
            ix = half_xc + half_xc * jnp.tanh(gx)
            a = jnp.exp(log_a)
            z = jnp.tanh(log_a) * (-1.0 - a * a)
            u = jnp.where(z > 0.0, z * lax.rsqrt(z), 0.0) * ix
            a_ref[pl.ds(o0, SCAN_CHUNK), :] = a
            u_ref[pl.ds(o0, SCAN_CHUNK), :] = u
        return 0

    lax.fori_loop(0, nch, gates_body, 0, unroll=2)

    def scan_body(t, carry):
        hf, pf, hb, pb = carry
        fwd = pl.ds(t, nch, stride=SCAN_PITCH)
        bwd = pl.ds(SCAN_CHUNK - 1 - t, nch, stride=SCAN_PITCH)
        a = af_ref[fwd, :]
        hf = a * hf + uf_ref[fwd, :]
        pf = a * pf
        uf_ref[fwd, :] = hf
        af_ref[fwd, :] = pf
        a = ab_ref[bwd, :]
        hb = a * hb + ub_ref[bwd, :]
        pb = a * pb
        ub_ref[bwd, :] = hb
        ab_ref[bwd, :] = pb
        return hf, pf, hb, pb

    zeros = jnp.zeros((nch, c), F32)
    ones = jnp.ones((nch, c), F32)
    hf, pf, hb, pb = lax.fori_loop(0, SCAN_CHUNK, scan_body, (zeros, ones, zeros, ones), unroll=4)

    carry = jnp.zeros((1, c), F32)
    cf_ref[0:1, :] = carry
    for j in range(1, nch):
        carry = hf[j - 1:j, :] + pf[j - 1:j, :] * carry
        cf_ref[j:j + 1, :] = carry
    carry = jnp.zeros((1, c), F32)
    cbk_ref[nch - 1:nch, :] = carry
    for j in range(nch - 2, -1, -1):
        carry = hb[j + 1:j + 2, :] + pb[j + 1:j + 2, :] * carry
        cbk_ref[j:j + 1, :] = carry

    def out_body(j, _):
        r0 = pl.multiple_of(j * SCAN_CHUNK, SCAN_CHUNK)
        o0 = pl.multiple_of(j * SCAN_PITCH, SUBLANES)
        rows = pl.ds(o0, SCAN_CHUNK)
        h = (uf_ref[rows, :] + af_ref[rows, :] * cf_ref[pl.ds(j, 1), :]
             + ub_ref[rows, :] + ab_ref[rows, :] * cbk_ref[pl.ds(j, 1), :])
        g = gate_ref[0, pl.ds(r0, SCAN_CHUNK), :].astype(F32)
        y_ref[0, pl.ds(r0, SCAN_CHUNK), :] = (h * _gelu_tanh(g)).astype(BF16)
        return 0

    lax.fori_loop(0, nch, out_body, 0, unroll=2)


def _lru_mix(xrec, gate, conv_w, conv_b, w_gates, b_gates, lam):
    nb, seq, _ = xrec.shape
    ncb = D_LRU // LRU_BLOCK
    c = LRU_BLOCK
    nch = seq // SCAN_CHUNK
    act = lambda b, k: (b, 0, k)
    par = lambda b, k: (k, 0, 0)
    scan_buf = pltpu.VMEM((nch * SCAN_PITCH, c), F32)
    return pl.pallas_call(
        functools.partial(_lru_kernel, seq=seq),
        out_shape=jax.ShapeDtypeStruct((nb, seq, D_LRU), BF16),
        grid=(nb, ncb),
        in_specs=[pl.BlockSpec((1, seq, c), act), pl.BlockSpec((1, seq, c), act),
                  pl.BlockSpec((1, CONV_WIDTH, c), par), pl.BlockSpec((1, 1, c), par),
                  pl.BlockSpec((1, c, 4 * c), par), pl.BlockSpec((1, 1, 4 * c), par),
                  pl.BlockSpec((1, 2, c), par)],
        out_specs=pl.BlockSpec((1, seq, c), act),
        scratch_shapes=[scan_buf, scan_buf, scan_buf, scan_buf,
                        pltpu.VMEM((nch, c), F32), pltpu.VMEM((nch, c), F32)],
        compiler_params=_params("arbitrary", "arbitrary"),
        name="lru_mix",
    )(xrec, gate, conv_w, conv_b, w_gates, b_gates, lam)


def _seqdft_kernel(zr_ref, zi_ref, f1_ref, m2_ref, y_ref, ar_ref, ai_ref):
    r, pitch, unroll = DFT_RADIX, DFT_PITCH, DFT_UNROLL
    f1 = f1_ref[...]

    def stage1(it, _):
        i2 = it * unroll
        slabs = []
        for q in range(unroll):
            rows = pl.ds(i2 + q, r, stride=pitch)
            slabs.append(jnp.concatenate([zr_ref[0, rows, :], zi_ref[0, rows, :]], axis=0))
        rhs = jnp.concatenate(slabs, axis=1).astype(BF16)
        a = jnp.dot(f1, rhs, preferred_element_type=F32)
        for q in range(unroll):
            rows = pl.ds(i2 + q, r, stride=pitch)
            ar_ref[rows, :] = a[:r, q * LANES:(q + 1) * LANES]
            ai_ref[rows, :] = a[r:, q * LANES:(q + 1) * LANES]
        return 0

    lax.fori_loop(0, r // unroll, stage1, 0, unroll=4)

    def stage2(it, _):
        for q in range(unroll):
            k1 = it * unroll + q
            src = pl.ds(pl.multiple_of(k1 * pitch, SUBLANES), r)
            slab = jnp.concatenate([ar_ref[src, :], ai_ref[src, :]], axis=0).astype(BF16)
            o = jnp.dot(m2_ref[k1], slab, preferred_element_type=F32)
            y_ref[0, pl.ds(k1, r, stride=pitch), :] = o
        return 0

    lax.fori_loop(0, r // unroll, stage2, 0, unroll=4)
    for gap in range(r, pitch):
        y_ref[0, pl.ds(gap, r, stride=pitch), :] = jnp.zeros((r, LANES), F32)


def _seq_dft(zr, zi, f1, m2):
    nb, rows, _ = zr.shape
    cw = LANES
    act = lambda b, k: (b, 0, k)
    return pl.pallas_call(
        _seqdft_kernel,
        out_shape=jax.ShapeDtypeStruct((nb, rows, D_FFT), F32),
        grid=(nb, D_FFT // cw),
        in_specs=[pl.BlockSpec((1, rows, cw), act), pl.BlockSpec((1, rows, cw), act),
                  pl.BlockSpec(f1.shape, lambda b, k: (0, 0)),
                  pl.BlockSpec(m2.shape, lambda b, k: (0, 0, 0))],
        out_specs=pl.BlockSpec((1, rows, cw), act),
        scratch_shapes=[pltpu.VMEM((rows, cw), F32), pltpu.VMEM((rows, cw), F32)],
        compiler_params=_params("arbitrary", "arbitrary"),
        name="seq_dft",
    )(zr, zi, f1, m2)


def _dft_tables(seq):
    n1 = n2 = DFT_RADIX
    assert n1 * n2 == seq, "sequence DFT is factored as DFT_RADIX x DFT_RADIX"
    g = FFT_GROUP_DIM
    ang = 2.0 * np.pi * np.outer(np.arange(g), np.arange(g)) / g
    cg, sg = np.cos(ang) / math.sqrt(g), np.sin(ang) / math.sqrt(g)
    ngroups = D_FFT // g
    dft_c = np.zeros((D_FFT, 2 * D_FFT))
    for q in range(ngroups):
        dft_c[q * g:(q + 1) * g, q * g:(q + 1) * g] = cg
        dft_c[q * g:(q + 1) * g, D_FFT + q * g:D_FFT + (q + 1) * g] = -sg
    ang1 = 2.0 * np.pi * np.outer(np.arange(n1), np.arange(n1)) / n1
    c1, s1 = np.cos(ang1) / math.sqrt(n1), np.sin(ang1) / math.sqrt(n1)
    f1 = np.block([[c1, s1], [-s1, c1]])
    k1 = np.arange(n1)[:, None, None]
    k2 = np.arange(n2)[None, :, None]
    i2 = np.arange(n2)[None, None, :]
    ang2 = 2.0 * np.pi * (i2 * k2 / n2 + i2 * k1 / seq)
    m2 = np.concatenate([np.cos(ang2), np.sin(ang2)], axis=2) / math.sqrt(n2)
    return (jnp.asarray(dft_c, F32), jnp.asarray(f1, F32), jnp.asarray(m2, F32))


def _flat_smem_copies(src_ref, smem_ref, base, sem):
    nk, nc, nl = src_ref.shape
    return [pltpu.make_async_copy(src_ref.at[k, c],
                                  smem_ref.at[pl.ds(base + (k * nc + c) * nl, nl)], sem)
            for k in range(nk) for c in range(nc)]


def _attn_route_kernel(xp_ref, xs_ref, ylru_ref, yfft_ref, kt_ref, v_ref, gl_ref, gf_ref, wout_ref,
                       gx_ref, wq_ref, wo_ref, gffn_ref, wrh_ref, wrl_ref, br_ref,
                       x2_ref, xs_out_ref, slot_ref, gate_ref, cnt_ref, h3tl_ref, slot_vmem, slot_smem,
                       ssem, *, prompt_tiles, tiles):
    g = pl.program_id(0)
    tm = xp_ref.shape[1]
    cur = lax.rem(g, 2)
    prev = 1 - cur

    @pl.when(g == 0)
    def _():
        h3tl_ref[1] = jnp.zeros(h3tl_ref.shape[1:], F32)

        def clear(j, _):
            slot_smem[j] = 0
            return 0

        lax.fori_loop(0, TOP_K * tm, clear, 0)

    @pl.when(g >= 1)
    def _():
        for cp in _flat_smem_copies(slot_vmem, slot_smem, 0, ssem):
            cp.wait()

    for t in range(tm):
        row = h3tl_ref[prev, t * ROW_TILES:(t + 1) * ROW_TILES, :]
        for k in range(TOP_K):
            dst = pl.multiple_of(slot_smem[k * tm + t], ROW_TILES)
            xs_out_ref[pl.ds(dst, ROW_TILES), :] = row

    x = jnp.where(g < prompt_tiles, xp_ref[0], xs_ref[0])
    m_lru = _rms(ylru_ref[0].astype(F32)) * gl_ref[...]
    yfft = jnp.concatenate([yfft_ref[0, m * DFT_PITCH:m * DFT_PITCH + DFT_RADIX, :]
                            for m in range(tm // DFT_RADIX)], axis=0)
    m_fft = _rms(yfft) * gf_ref[...]
    mixed = jnp.concatenate([m_lru, m_fft], axis=-1).astype(BF16)
    x1 = x + jnp.dot(mixed, wout_ref[...], preferred_element_type=F32)

    hq = (_rms(x1) * gx_ref[...]).astype(BF16)
    q = jnp.dot(hq, wq_ref[...], preferred_element_type=F32) * (XATTN_HEAD_DIM ** -0.5)
    qb = q.astype(BF16)
    heads = []
    for h in range(N_XATTN_HEADS):
        sl = slice(h * XATTN_HEAD_DIM, (h + 1) * XATTN_HEAD_DIM)
        s = jnp.dot(qb[:, sl], kt_ref[0, sl, :], preferred_element_type=F32)
        p = jnp.exp(s - jnp.max(s, axis=-1, keepdims=True))
        inv = 1.0 / jnp.sum(p, axis=-1, keepdims=True)
        o = jnp.dot(p.astype(BF16), v_ref[0, :, sl], preferred_element_type=F32) * inv
        heads.append(o.astype(BF16))
    att = jnp.concatenate(heads, axis=-1)
    x2 = x1 + jnp.dot(att, wo_ref[...], preferred_element_type=F32)
    x2_ref[0] = x2

    h3 = _rms(x2) * gffn_ref[...]
    for s in range(ROW_TILES):
        h3tl_ref[cur, pl.ds(s, tm, stride=ROW_TILES), :] = h3[:, s * LANES:(s + 1) * LANES]

    h_hi = h3.astype(BF16)
    h_lo = (h3 - h_hi.astype(F32)).astype(BF16)
    logits = (jnp.dot(h_hi, wrh_ref[...], preferred_element_type=F32)
              + jnp.dot(h_lo, wrh_ref[...], preferred_element_type=F32)
              + jnp.dot(h_hi, wrl_ref[...], preferred_element_type=F32)) + br_ref[...]
    lg = logits.T[:N_EXPERTS, :]
    e_iota = lax.broadcasted_iota(I32, lg.shape, 0)
    vals, onehots = [], []
    for k in range(TOP_K):
        m = jnp.max(lg, axis=0, keepdims=True)
        idx = jnp.min(jnp.where(lg == m, e_iota, N_EXPERTS), axis=0, keepdims=True)
        oh = e_iota == idx
        vals.append(m)
        onehots.append(oh)
        lg = jnp.where(oh, -jnp.inf, lg)
    exps = [jnp.exp(v - vals[0]) for v in vals]
    inv = 1.0 / (exps[0] + exps[1] + exps[2] + exps[3])

    def lane_tiles(rows):
        return jnp.stack([jnp.concatenate([r[:, c * LANES:(c + 1) * LANES]
                                           for c in range(tm // LANES)], axis=0) for r in rows])

    gate_ref[0] = lane_tiles([e * inv for e in exps])

    member = jnp.zeros(lg.shape, F32)
    for oh in onehots:
        member = member + oh.astype(F32)
    mb = member.astype(BF16)
    r_i = lax.broadcasted_iota(I32, (tm, tm), 0)
    c_i = lax.broadcasted_iota(I32, (tm, tm), 1)
    earlier = jnp.where(r_i < c_i, 1.0, 0.0).astype(BF16)
    tok_rank = jnp.dot(mb, earlier, preferred_element_type=F32)
    er = lax.broadcasted_iota(I32, (N_EXPERTS, N_EXPERTS), 0)
    ec = lax.broadcasted_iota(I32, (N_EXPERTS, N_EXPERTS), 1)
    lower = jnp.where(ec < er, 1.0, 0.0).astype(BF16)
    exp_off = jnp.sum(jnp.dot(lower, mb, preferred_element_type=F32), axis=1, keepdims=True)
    pos = exp_off + tok_rank
    slots = [jnp.sum(jnp.where(oh, pos, 0.0), axis=0, keepdims=True) for oh in onehots]
    slot_tab = lane_tiles(slots).astype(I32) * ROW_TILES
    slot_ref[0] = slot_tab
    slot_vmem[...] = slot_tab
    ones = jnp.ones((SUBLANES, tm), BF16)
    cnt = lax.dot_general(ones, mb, (((1,), (1,)), ((), ())), preferred_element_type=F32)
    cnt_ref[0] = cnt[0:1, :].astype(I32)

    @pl.when(g < tiles)
    def _():
        for cp in _flat_smem_copies(slot_vmem, slot_smem, 0, ssem):
            cp.start()


def _attn_route(xp, xs, ylru, yfft, kt, v, gl, gf, w_out, gx, wq, wo, gffn, wr_hi, wr_lo, br):
    nbp, seq, d = xp.shape
    nb = nbp + xs.shape[0]
    tm = TOKEN_TILE
    tmz = tm // DFT_RADIX * DFT_PITCH
    nt = seq // tm
    m = v.shape[1]
    tiles = nb * nt
    ptiles = nbp * nt
    tile = lambda g: jnp.minimum(g, tiles - 1)
    tok = lambda g: (tile(g) // nt, tile(g) % nt, 0)
    full2 = lambda g: (0, 0)
    per_b = lambda g: (tile(g) // nt, 0, 0)
    tile_id = lambda g: (tile(g), 0, 0)
    tile_id4 = lambda g: (tile(g), 0, 0, 0)
    xp_idx = lambda g: (jnp.minimum(g, ptiles - 1) // nt, jnp.minimum(g, ptiles - 1) % nt, 0)
    xs_idx = lambda g: (jnp.maximum(tile(g) - ptiles, 0) // nt, jnp.maximum(tile(g) - ptiles, 0) % nt, 0)
    nw = wr_hi.shape[1]
    per_tok = (TOP_K, tm // LANES, LANES)
    return pl.pallas_call(
        functools.partial(_attn_route_kernel, prompt_tiles=ptiles, tiles=tiles),
        out_shape=(jax.ShapeDtypeStruct((nb, seq, d), F32),
                   jax.ShapeDtypeStruct((tiles * tm * TOP_K * ROW_TILES, LANES), F32),
                   jax.ShapeDtypeStruct((tiles,) + per_tok, I32),
                   jax.ShapeDtypeStruct((tiles,) + per_tok, F32),
                   jax.ShapeDtypeStruct((tiles, 1, N_EXPERTS), I32)),
        grid=(tiles + 1,),
        in_specs=[pl.BlockSpec((1, tm, d), xp_idx),
                  pl.BlockSpec((1, tm, d), xs_idx),
                  pl.BlockSpec((1, tm, D_LRU), tok),
                  pl.BlockSpec((1, tmz, D_FFT), tok),
                  pl.BlockSpec((1, d, m), per_b), pl.BlockSpec((1, m, d), per_b),
                  pl.BlockSpec((1, D_LRU), full2), pl.BlockSpec((1, D_FFT), full2),
                  pl.BlockSpec((d, d), full2), pl.BlockSpec((1, d), full2),
                  pl.BlockSpec((d, d), full2), pl.BlockSpec((d, d), full2),
                  pl.BlockSpec((1, d), full2), pl.BlockSpec((d, nw), full2),
                  pl.BlockSpec((d, nw), full2), pl.BlockSpec((1, nw), full2)],
        out_specs=(pl.BlockSpec((1, tm, d), tok),
                   pl.BlockSpec((tm * TOP_K * ROW_TILES, LANES), lambda g: (jnp.maximum(g - 1, 0), 0)),
                   pl.BlockSpec((1,) + per_tok, tile_id4), pl.BlockSpec((1,) + per_tok, tile_id4),
                   pl.BlockSpec((1, 1, N_EXPERTS), tile_id)),
        scratch_shapes=[pltpu.VMEM((2, tm * ROW_TILES, LANES), F32),
                        pltpu.VMEM(per_tok, I32),
                        pltpu.SMEM((TOP_K * tm,), I32), pltpu.SemaphoreType.DMA],
        compiler_params=_params("arbitrary"),
        name="attn_route",
    )(xp, xs, ylru, yfft, kt, v, gl, gf, w_out, gx, wq, wo, gffn, wr_hi, wr_lo, br)


def _for_each_run(blk, bexp_ref, ostart_ref, t0_ref, t1_ref, cum_ref, cnt_ref, off_ref, fn):
    e = bexp_ref[blk]
    o0 = ostart_ref[blk]

    def body(i, _):
        j = i * N_EXPERTS + e
        c = cum_ref[j]
        lo = jnp.maximum(o0, c)
        hi = jnp.minimum(o0 + EXPERT_TILE, c + cnt_ref[j])

        @pl.when(hi > lo)
        def _():
            fn(i * (TOKEN_TILE * TOP_K) + off_ref[j] + (lo - c), lo - o0, hi - lo)
        return 0

    lax.fori_loop(t0_ref[blk], t1_ref[blk] + 1, body, 0)


def _rows(ref, row, n_rows):
    return ref.at[pl.ds(pl.multiple_of(row * ROW_TILES, ROW_TILES), n_rows * ROW_TILES), :]


def _experts_kernel(bexp_ref, bvalid_ref, nblk_ref, ostart_ref, t0_ref, t1_ref, cum_ref, cnt_ref,
                    off_ref, xs_hbm, wg_ref, bg_ref, wu_ref, bu_ref, wd_ref, bd_ref, ys_hbm,
                    xbuf, ybuf, act_ref, wgb_ref, wub_ref, wdb_ref, gsem, osem):
    b = pl.program_id(0)
    nblk = nblk_ref[0]
    slot = lax.rem(b, 2)
    tabs = (bexp_ref, ostart_ref, t0_ref, t1_ref, cum_ref, cnt_ref, off_ref)

    def gather(blk, sl):
        _for_each_run(blk, *tabs, lambda src, dst, n: pltpu.make_async_copy(
            _rows(xs_hbm, src, n), _rows(xbuf.at[sl], dst, n), gsem.at[sl]).start())

    def wait_in(blk, sl):
        n = bvalid_ref[blk]
        pltpu.make_async_copy(_rows(xs_hbm, 0, n), _rows(xbuf.at[sl], 0, n), gsem.at[sl]).wait()

    def wait_out(blk, sl):
        n = bvalid_ref[blk]
        pltpu.make_async_copy(_rows(ybuf.at[sl], 0, n), _rows(ys_hbm, 0, n), osem.at[sl]).wait()

    @pl.when(b == 0)
    def _():
        gather(0, 0)

    @pl.when(b + 1 < nblk)
    def _():
        gather(b + 1, 1 - slot)

    @pl.when(b < nblk)
    def _():
        wait_in(b, slot)

        @pl.when(b >= 2)
        def _():
            wait_out(b - 2, slot)

        @pl.when(jnp.logical_or(b == 0, bexp_ref[b] != bexp_ref[jnp.maximum(b - 1, 0)]))
        def _():
            def cast(r, _):
                rows = pl.ds(pl.multiple_of(r * LANES, LANES), LANES)
                wgb_ref[rows, :] = wg_ref[0, rows, :].astype(BF16)
                wub_ref[rows, :] = wu_ref[0, rows, :].astype(BF16)
                wdb_ref[rows, :] = wd_ref[0, rows, :].astype(BF16)
                return 0

            lax.fori_loop(0, D_MODEL // LANES, cast, 0)

        tm = EXPERT_TILE
        x = jnp.concatenate([xbuf[slot, pl.ds(s, tm, stride=ROW_TILES), :]
                             for s in range(ROW_TILES)], axis=-1)
        row = lax.broadcasted_iota(I32, (tm, 1), 0)
        xb = jnp.where(row < bvalid_ref[b], x, 0.0).astype(BF16)
        for cidx in range(D_FF // FF_CHUNK):
            sl = slice(cidx * FF_CHUNK, (cidx + 1) * FF_CHUNK)
            gt = jnp.dot(xb, wgb_ref[:, sl], preferred_element_type=F32) + bg_ref[0, :, sl]
            up = jnp.dot(xb, wub_ref[:, sl], preferred_element_type=F32) + bu_ref[0, :, sl]
            gt = jnp.minimum(gt, SWIGLU_LIMIT)
            up = jnp.clip(up, -SWIGLU_LIMIT, SWIGLU_LIMIT)
            hg = 0.5 * gt
            act_ref[:, sl] = ((hg + hg * jnp.tanh(SWIGLU_ALPHA * hg)) * (up + 1.0)).astype(BF16)
        for cidx in range(D_MODEL // FF_CHUNK):
            sl = slice(cidx * FF_CHUNK, (cidx + 1) * FF_CHUNK)
            o = jnp.dot(act_ref[...], wdb_ref[:, sl], preferred_element_type=F32) + bd_ref[0, :, sl]
            for q in range(FF_CHUNK // LANES):
                s = cidx * (FF_CHUNK // LANES) + q
                ybuf[slot, pl.ds(s, tm, stride=ROW_TILES), :] = o[:, q * LANES:(q + 1) * LANES]
        _for_each_run(b, *tabs, lambda dst, src, n: pltpu.make_async_copy(
            _rows(ybuf.at[slot], src, n), _rows(ys_hbm, dst, n), osem.at[slot]).start())

        @pl.when(b == nblk - 1)
        def _():
            wait_out(b, slot)

            @pl.when(b >= 1)
            def _():
                wait_out(b - 1, 1 - slot)


def _moe_experts(tables, xs, wg, bg, wu, bu, wd, bd):
    bexp = tables[0]
    nblocks = bexp.shape[0]
    tm = EXPERT_TILE
    wsel = lambda i, be, bv, nb, *_: (be[jnp.minimum(i, nb[0] - 1)], 0, 0)
    wspec = pl.BlockSpec((1, D_MODEL, D_FF), wsel)
    bspec = pl.BlockSpec((1, 1, D_FF), wsel)
    hbm = pl.BlockSpec(memory_space=pl.ANY)
    return pl.pallas_call(
        _experts_kernel,
        out_shape=jax.ShapeDtypeStruct(xs.shape, F32),
        grid_spec=pltpu.PrefetchScalarGridSpec(
            num_scalar_prefetch=len(tables),
            grid=(nblocks,),
            in_specs=[hbm, wspec, bspec, wspec, bspec, wspec, bspec],
            out_specs=hbm,
            scratch_shapes=[pltpu.VMEM((2, tm * ROW_TILES, LANES), F32),
                            pltpu.VMEM((2, tm * ROW_TILES, LANES), F32),
                            pltpu.VMEM((tm, D_FF), BF16),
                            pltpu.VMEM((D_MODEL, D_FF), BF16), pltpu.VMEM((D_MODEL, D_FF), BF16),
                            pltpu.VMEM((D_FF, D_MODEL), BF16),
                            pltpu.SemaphoreType.DMA((2,)), pltpu.SemaphoreType.DMA((2,))]),
        compiler_params=_params("arbitrary"),
        name="moe_experts",
    )(*tables, xs, wg, bg, wu, bu, wd, bd)


def _combine_kernel(slot_hbm, gate_hbm, ys_ref, x2_ref, gfin_ref, outp_ref, outs_ref,
                    slot_smem, gate_smem, acc_ref, isem, *, prompt_tiles):
    i = pl.program_id(0)
    tm = x2_ref.shape[0]
    n = TOP_K * tm
    half = lax.rem(i, 2)
    base = half * n

    def table_copies(tile, hf):
        return (_flat_smem_copies(slot_hbm.at[tile], slot_smem, hf * n, isem.at[hf])
                + _flat_smem_copies(gate_hbm.at[tile], gate_smem, hf * n, isem.at[hf]))

    @pl.when(i == 0)
    def _():
        for cp in table_copies(0, 0):
            cp.start()

    @pl.when(i + 1 < pl.num_programs(0))
    def _():
        for cp in table_copies(i + 1, 1 - half):
            cp.start()

    x2 = x2_ref[...]
    for s in range(ROW_TILES):
        acc_ref[pl.ds(s, tm, stride=ROW_TILES), :] = x2[:, s * LANES:(s + 1) * LANES]
    for cp in table_copies(i, half):
        cp.wait()

    for t in range(tm):
        rows = slice(t * ROW_TILES, (t + 1) * ROW_TILES)
        acc = acc_ref[rows, :]
        for k in range(TOP_K):
            src = pl.multiple_of(slot_smem[base + k * tm + t], ROW_TILES)
            acc = acc + gate_smem[base + k * tm + t] * ys_ref[pl.ds(src, ROW_TILES), :]
        acc_ref[rows, :] = acc
    acc = jnp.concatenate([acc_ref[pl.ds(s, tm, stride=ROW_TILES), :] for s in range(ROW_TILES)],
                          axis=-1)
    y = _rms(acc) * gfin_ref[...]

    @pl.when(i < prompt_tiles)
    def _():
        outp_ref[...] = y

    @pl.when(i >= prompt_tiles)
    def _():
        outs_ref[...] = y


def _moe_combine(slots, gates, ys, x2, gfin, t_prompt):
    t, d = x2.shape
    tm = TOKEN_TILE
    ptiles = t_prompt // tm
    return pl.pallas_call(
        functools.partial(_combine_kernel, prompt_tiles=ptiles),
        out_shape=(jax.ShapeDtypeStruct((t_prompt, d), F32),
                   jax.ShapeDtypeStruct((t - t_prompt, d), F32)),
        grid=(t // tm,),
        in_specs=[pl.BlockSpec(memory_space=pl.ANY), pl.BlockSpec(memory_space=pl.ANY),
                  pl.BlockSpec((tm * TOP_K * ROW_TILES, LANES), lambda i: (i, 0)),
                  pl.BlockSpec((tm, d), lambda i: (i, 0)),
                  pl.BlockSpec((1, d), lambda i: (0, 0))],
        out_specs=(pl.BlockSpec((tm, d), lambda i: (jnp.minimum(i, ptiles - 1), 0)),
                   pl.BlockSpec((tm, d), lambda i: (jnp.maximum(i - ptiles, 0), 0))),
        scratch_shapes=[pltpu.SMEM((2 * TOP_K * tm,), I32), pltpu.SMEM((2 * TOP_K * tm,), F32),
                        pltpu.VMEM((tm * ROW_TILES, LANES), F32),
                        pltpu.SemaphoreType.DMA((2,))],
        compiler_params=_params("arbitrary"),
        name="moe_combine",
    )(slots, gates, ys, x2, gfin)


def _lru_gate_weights(w_a, b_a, w_x, b_x):
    hpb = LRU_BLOCK // LRU_HEAD_DIM
    ncb = D_LRU // LRU_BLOCK

    def blockdiag(w):
        w = w.reshape(ncb, hpb, LRU_HEAD_DIM, LRU_HEAD_DIM)
        eye = jnp.eye(hpb, dtype=w.dtype)
        full = jnp.einsum('nhij,hg->nhigj', w, eye)
        return full.reshape(ncb, LRU_BLOCK, LRU_BLOCK)

    mats = [blockdiag(w_a[0]), blockdiag(w_a[1]), blockdiag(w_x[0]), blockdiag(w_x[1])]
    w = (0.5 * jnp.concatenate(mats, axis=-1)).astype(BF16)
    bs = [b.reshape(ncb, 1, LRU_BLOCK) for b in (b_a[0], b_a[1], b_x[0], b_x[1])]
    return w, (0.5 * jnp.concatenate(bs, axis=-1)).astype(F32)


def _encoder_layer(xp, xs, mem, norm_mix, w_in, conv_w, conv_b, lru_w_a, lru_b_a, lru_w_x, lru_b_x,
                   lru_lambda, norm_lru_out, norm_fft_out, w_out, norm_xattn, norm_mem, w_q, w_kv,
                   w_o, norm_ffn, w_router, b_router, w_gate, b_gate, w_up, b_up, w_down, b_down,
                   norm_final):
    nbp, seq, d = xp.shape
    nb = nbp + xs.shape[0]
    t = nb * seq
    seqz = seq // DFT_RADIX * DFT_PITCH
    ncb = D_LRU // LRU_BLOCK
    row = lambda v: v.reshape(1, -1).astype(F32)

    dft_c, f1, m2 = _dft_tables(seq)
    kt, v = _kv_proj(mem, row(norm_mem), w_kv[:, :d].T.astype(BF16), w_kv[:, d:].astype(BF16))

    xrec, gate, zr, zi = _in_proj(xp.reshape(nbp * seq, d), xs.reshape(t - nbp * seq, d),
                                  row(norm_mix), w_in.astype(BF16), dft_c.astype(BF16))

    w_gates, b_gates = _lru_gate_weights(lru_w_a, lru_b_a, lru_w_x, lru_b_x)
    cw = conv_w.reshape(CONV_WIDTH, ncb, LRU_BLOCK).transpose(1, 0, 2)
    cb = conv_b.reshape(ncb, 1, LRU_BLOCK)
    lam = lru_lambda.reshape(2, ncb, LRU_BLOCK).transpose(1, 0, 2)
    ylru = _lru_mix(xrec.reshape(nb, seq, D_LRU), gate.reshape(nb, seq, D_LRU), cw, cb,
                    w_gates, b_gates, lam)

    yfft = _seq_dft(zr.reshape(nb, seqz, D_FFT), zi.reshape(nb, seqz, D_FFT),
                    f1.astype(BF16), m2.astype(BF16))

    w_r = jnp.pad(w_router.astype(F32), ((0, 0), (0, LANES - N_EXPERTS)))
    w_r_hi = w_r.astype(BF16)
    w_r_lo = (w_r - w_r_hi.astype(F32)).astype(BF16)
    b_r = jnp.pad(b_router.astype(F32).reshape(1, -1), ((0, 0), (0, LANES - N_EXPERTS)))
    x2, xs_rows, slots, gates, tile_cnt = _attn_route(
        xp, xs, ylru, yfft, kt, v, row(norm_lru_out), row(norm_fft_out), w_out.astype(BF16),
        row(norm_xattn), w_q.astype(BF16), w_o.astype(BF16), row(norm_ffn), w_r_hi, w_r_lo, b_r)

    tmb = EXPERT_TILE
    nblocks = -(-(t * TOP_K) // tmb) + N_EXPERTS
    tile_cnt = tile_cnt.reshape(-1, N_EXPERTS)
    cum_end = jnp.cumsum(tile_cnt, axis=0)
    cum = cum_end - tile_cnt
    off = jnp.cumsum(tile_cnt, axis=1) - tile_cnt
    counts = cum_end[-1]
    padded = ((counts + tmb - 1) // tmb) * tmb
    pend = jnp.cumsum(padded)
    pstart = pend - padded
    blk_start = jnp.arange(nblocks, dtype=I32) * tmb
    bexp = jnp.minimum(jnp.sum((pend[None, :] <= blk_start[:, None]).astype(I32), axis=1),
                       N_EXPERTS - 1)
    ostart = blk_start - pstart[bexp]
    bvalid = jnp.clip(counts[bexp] - ostart, 0, tmb).astype(I32)
    nblk = (pend[-1:] // tmb).astype(I32)
    cum_b = jnp.take(cum, bexp, axis=1)
    end_b = jnp.take(cum_end, bexp, axis=1)
    ntiles = tile_cnt.shape[0]
    t0 = jnp.minimum(jnp.sum((end_b <= ostart[None, :]).astype(I32), axis=0), ntiles - 1)
    t1 = jnp.maximum(jnp.sum((cum_b < (ostart + tmb)[None, :]).astype(I32), axis=0) - 1, 0)
    tables = tuple(a.astype(I32) for a in (bexp, bvalid, nblk, ostart, t0, t1, cum.reshape(-1),
                                           tile_cnt.reshape(-1), off.reshape(-1)))

    ys = _moe_experts(tables, xs_rows,
                      w_gate, b_gate.reshape(N_EXPERTS, 1, D_FF),
                      w_up, b_up.reshape(N_EXPERTS, 1, D_FF),
                      w_down, b_down.reshape(N_EXPERTS, 1, D_MODEL))
    outp, outs = _moe_combine(slots, gates, ys, x2.reshape(t, d), row(norm_final), nbp * seq)
    return outp.reshape(nbp, seq, d), outs.reshape(nb - nbp, seq, d)


def kernel(x_prompt, x_sample, mem_prompt, mem_sample, norm_mix, w_in, conv_w, conv_b, lru_w_a,
           lru_b_a, lru_w_x, lru_b_x, lru_lambda, norm_lru_out, norm_fft_out, w_out, norm_xattn,
           norm_mem, w_q, w_kv, w_o, norm_ffn, w_router, b_router, w_gate, b_gate, w_up, b_up,
           w_down, b_down, norm_final):
    assert x_prompt.shape[1:] == x_sample.shape[1:], "both groups must share (SEQ, D_MODEL)"
    assert w_in.shape[0] == 1, "single-layer block"
    mem = jnp.concatenate([mem_prompt, mem_sample], axis=0)
    return _encoder_layer(x_prompt, x_sample, mem, norm_mix[0], w_in[0], conv_w[0], conv_b[0],
                          lru_w_a[0], lru_b_a[0], lru_w_x[0], lru_b_x[0], lru_lambda[0],
                          norm_lru_out[0], norm_fft_out[0], w_out[0], norm_xattn[0], norm_mem[0],
                          w_q[0], w_kv[0], w_o[0], norm_ffn[0], w_router[0], b_router[0],
                          w_gate[0], b_gate[0], w_up[0], b_up[0], w_down[0], b_down[0], norm_final)
```
